```python
import math
import jax
import jax.numpy as jnp
from jax import lax
import numpy as np

D_MODEL = 1024
BATCH = 8
SEQ = 2048
DEPTH = 2

GRID_W = 64
CTX_LEN = 256
F32 = jnp.float32

N_MOD = 6
DEEPNORM_ALPHA = (2 * DEPTH) ** 0.25
DEEPNORM_BETA = (8 * DEPTH) ** -0.25
LN_EPS = 1e-5
RMS_EPS = 1e-6

MIX_WIDTH = D_MODEL
POOL_WINDOWS = (2, 4, 8, 16)
POOL_GROUPS = 4
POOL_WIDTH = MIX_WIDTH // 4
POOL_GROUP_DIM = POOL_WIDTH // POOL_GROUPS
GDN_HEAD_DIM = 128
GDN_HEADS = (MIX_WIDTH - POOL_WIDTH) // GDN_HEAD_DIM
GDN_WIDTH = GDN_HEADS * GDN_HEAD_DIM
GDN_CONV = 4
GDN_CONV_PAD = (2, 1)
GDN_CHUNK = 64
AB_IN = POOL_WIDTH + 4 * GDN_WIDTH + 4 * GDN_HEADS

MLA_HEADS = 8
MLA_NOPE = 128
MLA_ROPE = 64
MLA_V = 128
MLA_Q_LORA = 384
MLA_KV_LORA = 256
MLA_KVW = MLA_KV_LORA + MLA_ROPE
MLA_IN = MLA_KVW + MLA_Q_LORA
MLA_WIDTH = MLA_HEADS * MLA_V
Q_BLOCK = 128
ROPE_BASE = 10000.0
ROPE_AXIS_FREQS = MLA_ROPE // 4

MOE_GROUPS = 4
MOE_EXPERTS_PER_GROUP = 8
MOE_EXPERTS = MOE_GROUPS * MOE_EXPERTS_PER_GROUP
MOE_TOPK = 2
MOE_FF = 512

kernel_name = 'hybrid_pool_gdn_mla_hmoe_dit'


def layer_norm(t, g, b):
    tf = t.astype(F32)
    mu = jnp.mean(tf, axis=-1, keepdims=True)
    var = jnp.mean(jnp.square(tf - mu), axis=-1, keepdims=True)
    y = (tf - mu) * lax.rsqrt(var + LN_EPS)
    return (y * g.astype(F32) + b.astype(F32)).astype(t.dtype)


def rms_norm(t, w):
    tf = t.astype(F32)
    y = tf * lax.rsqrt(jnp.mean(tf * tf, axis=-1, keepdims=True) + RMS_EPS)
    return (y * w.astype(F32)).astype(t.dtype)


def l2_normalize(t):
    return t * lax.rsqrt(jnp.sum(t * t, axis=-1, keepdims=True) + RMS_EPS)


def modulate(h, shift, scale):
    return h * (1 + scale) + shift


def axial_rope(n_tokens):
    rows = n_tokens // GRID_W
    row = jnp.repeat(jnp.arange(rows, dtype=F32), GRID_W)
    col = (jnp.arange(n_tokens) % GRID_W).astype(F32)
    inv_freq = ROPE_BASE ** (-jnp.arange(ROPE_AXIS_FREQS, dtype=F32) / ROPE_AXIS_FREQS)
    ang = jnp.stack([row[:, None] * inv_freq, col[:, None] * inv_freq], axis=1)
    return jnp.cos(ang), jnp.sin(ang)


def apply_rope(t, cos, sin):
    ts = t.astype(F32).reshape(t.shape[:-1] + (2, 2, ROPE_AXIS_FREQS))
    t1, t2 = ts[..., 0, :], ts[..., 1, :]
    out = jnp.stack([t1 * cos - t2 * sin, t2 * cos + t1 * sin], axis=-2)
    return out.reshape(t.shape).astype(t.dtype)


def multiscale_pool(u, w_grp, scale):
    B, L, _ = u.shape
    uf = u.astype(F32).reshape(B, L, POOL_GROUPS, POOL_GROUP_DIM)
    csum = jnp.pad(jnp.cumsum(uf, axis=1), ((0, 0), (1, 0), (0, 0), (0, 0)))
    t = jnp.arange(L)
    win = jnp.array(POOL_WINDOWS)
    left = win // 2
    right = win - 1 - left
    lo = jnp.clip(t[:, None] - left[None, :], 0, L)
    hi = jnp.clip(t[:, None] + right[None, :] + 1, 0, L)
    g_idx = jnp.arange(POOL_GROUPS)[None, :]
    mean = (csum[:, hi, g_idx, :] - csum[:, lo, g_idx, :]) / (hi - lo).astype(F32)[None, :, :, None]
    d = (mean - uf).astype(u.dtype)
    y = jnp.einsum('blgc,gce->blge', d, w_grp).reshape(B, L, POOL_WIDTH)
    return y * scale


def short_conv(u, w):
    out = lax.conv_general_dilated(u, w[:, None, :], window_strides=(1,), padding=[GDN_CONV_PAD],
                                   dimension_numbers=('NWC', 'WIO', 'NWC'),
                                   feature_group_count=u.shape[-1])
    return jax.nn.silu(out)


def gdn_inputs(p, conv_w, a_log, dt_bias):
    B, L, _ = p.shape
    o = POOL_WIDTH
    qkv = short_conv(p[..., o:o + 3 * GDN_WIDTH], conv_w).astype(F32)
    qkv = qkv.reshape(B, L, 3, GDN_HEADS, GDN_HEAD_DIM)
    q = l2_normalize(qkv[:, :, 0]) * (GDN_HEAD_DIM ** -0.5)
    k = l2_normalize(qkv[:, :, 1])
    v = qkv[:, :, 2]
    o += 3 * GDN_WIDTH
    z = p[..., o:o + GDN_WIDTH].reshape(B, L, GDN_HEADS, GDN_HEAD_DIM)
    o += GDN_WIDTH
    a = p[..., o:o + 2 * GDN_HEADS].astype(F32).reshape(B, L, 2, GDN_HEADS)
    b = p[..., o + 2 * GDN_HEADS:o + 4 * GDN_HEADS].astype(F32).reshape(B, L, 2, GDN_HEADS)
    g = -jnp.exp(a_log.astype(F32)) * jax.nn.softplus(a + dt_bias.astype(F32))
    beta = jax.nn.sigmoid(b)
    return q, k, v, z, g, beta


def chunk_gated_delta(q, k, v, g, beta, s0):
    B, L, H, DK = q.shape
    DV = v.shape[-1]
    C = GDN_CHUNK
    N = L // C

    def chunks(t):
        return jnp.swapaxes(t.reshape((B, N, C) + t.shape[2:]), 2, 3)

    qc, kc, vc = chunks(q), chunks(k), chunks(v)
    gc = jnp.cumsum(chunks(g), axis=-1)
    bc = chunks(beta)
    lower = jnp.tril(jnp.ones((C, C), bool))
    strict = jnp.tril(jnp.ones((C, C), bool), -1)
    diff = gc[..., :, None] - gc[..., None, :]
    decay = jnp.where(lower, jnp.exp(jnp.where(lower, diff, 0.0)), 0.0)
    kb = kc * bc[..., None]
    lmat = jnp.where(strict, jnp.einsum('bnhid,bnhjd->bnhij', kb, kc) * decay, 0.0)
    rhs = jnp.concatenate([vc * bc[..., None], kb * jnp.exp(gc)[..., None]], axis=-1)
    uw = lax.linalg.triangular_solve(lmat, rhs, left_side=True, lower=True, unit_diagonal=True)
    u, w = uw[..., :DV], uw[..., DV:]
    attn = jnp.where(lower, jnp.einsum('bnhid,bnhjd->bnhij', qc, kc) * decay, 0.0)
    q_dec = qc * jnp.exp(gc)[..., None]
    g_last = gc[..., -1]
    k_dec = kc * jnp.exp(g_last[..., None] - gc)[..., None]

    def step(s, xs):
        u_i, w_i, a_i, qd_i, kd_i, gl_i = xs
        v_new = u_i - jnp.einsum('bhcd,bhde->bhce', w_i, s)
        o_i = jnp.einsum('bhcd,bhde->bhce', qd_i, s) + jnp.einsum('bhij,bhje->bhie', a_i, v_new)
        s = s * jnp.exp(gl_i)[..., None, None] + jnp.einsum('bhcd,bhce->bhde', kd_i, v_new)
        return s, o_i

    xs = tuple(jnp.moveaxis(t, 1, 0) for t in (u, w, attn, q_dec, k_dec, g_last))
    s_final, o = lax.scan(step, s0, xs)
    o = jnp.swapaxes(jnp.moveaxis(o, 0, 1), 2, 3).reshape(B, L, H, DV)
    return o, s_final


def gdn_scan(q, k, v, g, beta, s0, reverse):
    if reverse:
        rev = lambda t: jnp.flip(t, axis=1)
        o, s = chunk_gated_delta(rev(q), rev(k), rev(v), rev(g), rev(beta), s0)
        return rev(o), s
    return chunk_gated_delta(q, k, v, g, beta, s0)


def gdn_output(o, z, norm_w):
    B, L = o.shape[:2]
    y = rms_norm(o, norm_w) * jax.nn.silu(z.astype(F32))
    return y.reshape(B, L, GDN_WIDTH).astype(z.dtype)


def pool_gdn_mixer(h_lat, h_ctx, w_in, pool_w, pool_scale, conv_w, a_log, dt_bias, norm_w, w_out,
                   need_ctx):
    p_lat = h_lat @ w_in
    p_ctx = h_ctx @ w_in
    q_l, k_l, v_l, z_l, g_l, b_l = gdn_inputs(p_lat, conv_w, a_log, dt_bias)
    q_c, k_c, v_c, z_c, g_c, b_c = gdn_inputs(p_ctx, conv_w, a_log, dt_bias)
    s_zero = jnp.zeros((h_lat.shape[0], GDN_HEADS, GDN_HEAD_DIM, GDN_HEAD_DIM), F32)
    o_cf, s_cf = gdn_scan(q_c, k_c, v_c, g_c[:, :, 0], b_c[:, :, 0], s_zero, False)
    o_cb, s_cb = gdn_scan(q_c, k_c, v_c, g_c[:, :, 1], b_c[:, :, 1], s_zero, True)
    o_lf, _ = gdn_scan(q_l, k_l, v_l, g_l[:, :, 0], b_l[:, :, 0], s_cf, False)
    o_lb, _ = gdn_scan(q_l, k_l, v_l, g_l[:, :, 1], b_l[:, :, 1], s_cb, True)
    y_lat = jnp.concatenate([multiscale_pool(p_lat[..., :POOL_WIDTH], pool_w, pool_scale),
                             gdn_output(o_lf + o_lb, z_l, norm_w)], axis=-1) @ w_out
    y_ctx = None
    if need_ctx:
        y_ctx = jnp.concatenate([multiscale_pool(p_ctx[..., :POOL_WIDTH], pool_w, pool_scale),
                                 gdn_output(o_cf + o_cb, z_c, norm_w)], axis=-1) @ w_out
    return y_lat, y_ctx


def mla_keys(p_kv, kv_norm, w_ukv):
    B, L, _ = p_kv.shape
    ckv = rms_norm(p_kv[..., :MLA_KV_LORA], kv_norm)
    kv = (ckv @ w_ukv).reshape(B, L, MLA_HEADS, MLA_NOPE + MLA_V)
    return kv[..., :MLA_NOPE], p_kv[..., MLA_KV_LORA:], kv[..., MLA_NOPE:]


def mla_queries(p_q, q_norm, w_uq):
    B, L, _ = p_q.shape
    q = (rms_norm(p_q, q_norm) @ w_uq).reshape(B, L, MLA_HEADS, MLA_NOPE + MLA_ROPE)
    return q[..., :MLA_NOPE], q[..., MLA_NOPE:]


def block_attention(q_nope, q_rope, k_nope, k_rope, v):
    B, Lq = q_nope.shape[:2]
    nb = Lq // Q_BLOCK
    scale = (MLA_NOPE + MLA_ROPE) ** -0.5

    def blocks(t):
        return jnp.moveaxis(t.reshape((B, nb, Q_BLOCK) + t.shape[2:]), 1, 0)

    def one(qs):
        qn, qr = qs
        s = (jnp.einsum('bqhd,bkhd->bhqk', qn, k_nope)
             + jnp.einsum('bqhr,bkr->bhqk', qr, k_rope)).astype(F32) * scale
        pr = jax.nn.softmax(s, axis=-1).astype(v.dtype)
        return jnp.einsum('bhqk,bkhd->bqhd', pr, v)

    o = lax.map(one, (blocks(q_nope), blocks(q_rope)))
    return jnp.moveaxis(o, 0, 1).reshape(B, Lq, MLA_WIDTH)


def mla_mixer(h_lat, h_ctx, w_in, kv_norm, w_ukv, q_norm, w_uq, w_out, cos, sin, need_ctx):
    p_lat = h_lat @ w_in
    kn_l, kr_l, v_l = mla_keys(p_lat[..., :MLA_KVW], kv_norm, w_ukv)
    qn_l, qr_l = mla_queries(p_lat[..., MLA_KVW:], q_norm, w_uq)
    qr_l = apply_rope(qr_l, cos[:, None], sin[:, None])
    kr_l = apply_rope(kr_l, cos, sin)
    p_ctx = h_ctx @ (w_in if need_ctx else w_in[:, :MLA_KVW])
    kn_c, kr_c, v_c = mla_keys(p_ctx[..., :MLA_KVW], kv_norm, w_ukv)
    k_nope = jnp.concatenate([kn_c, kn_l], axis=1)
    k_rope = jnp.concatenate([kr_c, kr_l], axis=1)
    v = jnp.concatenate([v_c, v_l], axis=1)
    y_lat = block_attention(qn_l, qr_l, k_nope, k_rope, v) @ w_out
    y_ctx = None
    if need_ctx:
        qn_c, qr_c = mla_queries(p_ctx[..., MLA_KVW:], q_norm, w_uq)
        y_ctx = block_attention(qn_c, qr_c, kn_c, kr_c, v_c) @ w_out
    return y_lat, y_ctx


def hier_moe(h, w_group, b_group, w_expert, b_expert, w_gate, w_up, w_down):
    N = h.shape[0]
    g_prob = jax.nn.softmax((h @ w_group).astype(F32) + b_group.astype(F32), axis=-1)
    g_val, g_idx = lax.top_k(g_prob, 1)
    e_logits = ((h @ w_expert).astype(F32) + b_expert.astype(F32)).reshape(
        N, MOE_GROUPS, MOE_EXPERTS_PER_GROUP)
    e_prob = jax.nn.softmax(e_logits[jnp.arange(N), g_idx[:, 0]], axis=-1)
    e_val, e_idx = lax.top_k(e_prob, MOE_TOPK)
    weights = g_val * e_val / jnp.sum(e_val, axis=-1, keepdims=True)
    expert_id = g_idx * MOE_EXPERTS_PER_GROUP + e_idx
    gates = jnp.einsum('nk,nke->ne', weights,
                       jax.nn.one_hot(expert_id, MOE_EXPERTS, dtype=F32)).astype(h.dtype)
    out = jnp.zeros_like(h)
    for grp in range(MOE_GROUPS):
        sl = slice(grp * MOE_EXPERTS_PER_GROUP, (grp + 1) * MOE_EXPERTS_PER_GROUP)
        a = jnp.einsum('nd,edf->nef', h, w_gate[sl])
        b = jnp.einsum('nd,edf->nef', h, w_up[sl])
        act = jax.nn.silu(a) * b * gates[:, sl, None]
        out = out + jnp.einsum('nef,efd->nd', act, w_down[sl])
    return out


def setup_inputs(seed: int = 0) -> dict:
    key = jax.random.key(seed)
    ks = iter(jax.random.split(key, 40))
    D = D_MODEL
    n_even = (DEPTH + 1) // 2
    n_odd = DEPTH // 2

    def nrm(shape, s):
        return jax.random.normal(next(ks), shape, F32) * s

    a_log = jnp.log(jax.random.uniform(next(ks), (n_even, 2, GDN_HEADS), F32, 1.0, 16.0))
    dt = jnp.exp(jax.random.uniform(next(ks), (n_even, 2, GDN_HEADS), F32,
                                    math.log(1e-3), math.log(1e-1)))
    dt_bias = dt + jnp.log(-jnp.expm1(-dt))
    return {
        'x': nrm((BATCH, SEQ, D), 1.0),
        'c': nrm((BATCH, D), 1.0),
        'ctx': nrm((BATCH, CTX_LEN, D), 1.0),
        'c_ctx': nrm((D,), 1.0),
        'ada_w': nrm((DEPTH, D, N_MOD * D), 0.5 * D ** -0.5),
        'ada_b': nrm((DEPTH, N_MOD * D), 0.02),
        'ln_g': 1.0 + nrm((DEPTH, 2, D), 0.02),
        'ln_b': nrm((DEPTH, 2, D), 0.02),
        'ab_w_in': nrm((n_even, D, AB_IN), D ** -0.5),
        'pool_w': nrm((n_even, POOL_GROUPS, POOL_GROUP_DIM, POOL_GROUP_DIM), POOL_GROUP_DIM ** -0.5),
        'pool_scale': 1.0 + nrm((n_even, POOL_WIDTH), 0.02),
        'gdn_conv_w': nrm((n_even, GDN_CONV, 3 * GDN_WIDTH), GDN_CONV ** -0.5),
        'gdn_a_log': a_log,
        'gdn_dt_bias': dt_bias,
        'gdn_norm_w': 1.0 + nrm((n_even, GDN_HEAD_DIM), 0.02),
        'ab_w_out': nrm((n_even, MIX_WIDTH, D), MIX_WIDTH ** -0.5 * DEEPNORM_BETA),
        'mla_w_in': nrm((n_odd, D, MLA_IN), D ** -0.5),
        'mla_kv_norm': 1.0 + nrm((n_odd, MLA_KV_LORA), 0.02),
        'mla_w_ukv': nrm((n_odd, MLA_KV_LORA, MLA_HEADS * (MLA_NOPE + MLA_V)), MLA_KV_LORA ** -0.5),
        'mla_q_norm': 1.0 + nrm((n_odd, MLA_Q_LORA), 0.02),
        'mla_w_uq': nrm((n_odd, MLA_Q_LORA, MLA_HEADS * (MLA_NOPE + MLA_ROPE)), MLA_Q_LORA ** -0.5),
        'mla_w_out': nrm((n_odd, MLA_WIDTH, D), MLA_WIDTH ** -0.5 * DEEPNORM_BETA),
        'moe_w_group': nrm((DEPTH, D, MOE_GROUPS), D ** -0.5),
        'moe_b_group': nrm((DEPTH, MOE_GROUPS), 0.01),
        'moe_w_expert': nrm((DEPTH, D, MOE_EXPERTS), D ** -0.5),
        'moe_b_expert': nrm((DEPTH, MOE_EXPERTS), 0.01),
        'moe_w_gate': nrm((DEPTH, MOE_EXPERTS, D, MOE_FF), D ** -0.5),
        'moe_w_up': nrm((DEPTH, MOE_EXPERTS, D, MOE_FF), D ** -0.5),
        'moe_w_down': nrm((DEPTH, MOE_EXPERTS, MOE_FF, D), MOE_FF ** -0.5 * DEEPNORM_BETA),
    }


def reference(x, c, ctx, c_ctx, ada_w, ada_b, ln_g, ln_b, ab_w_in, pool_w, pool_scale, gdn_conv_w,
              gdn_a_log, gdn_dt_bias, gdn_norm_w, ab_w_out, mla_w_in, mla_kv_norm, mla_w_ukv,
              mla_q_norm, mla_w_uq, mla_w_out, moe_w_group, moe_b_group, moe_w_expert, moe_b_expert,
              moe_w_gate, moe_w_up, moe_w_down):
    B, L, D = x.shape
    Lc = ctx.shape[1]
    cos, sin = axial_rope(L)
    for l in range(DEPTH):
        need_ctx = l < DEPTH - 1
        i = l // 2
        mod = (jax.nn.silu(c) @ ada_w[l] + ada_b[l])[:, None, :]
        mod_c = jax.nn.silu(c_ctx) @ ada_w[l] + ada_b[l]
        sh1, sc1, g1, sh2, sc2, g2 = jnp.split(mod, N_MOD, axis=-1)
        csh1, csc1, cg1, csh2, csc2, cg2 = jnp.split(mod_c, N_MOD, axis=-1)
        h_lat = modulate(x, sh1, sc1)
        h_ctx = modulate(ctx, csh1, csc1)
        if l % 2 == 0:
            y_lat, y_ctx = pool_gdn_mixer(h_lat, h_ctx, ab_w_in[i], pool_w[i], pool_scale[i],
                                          gdn_conv_w[i], gdn_a_log[i], gdn_dt_bias[i],
                                          gdn_norm_w[i], ab_w_out[i], need_ctx)
        else:
            y_lat, y_ctx = mla_mixer(h_lat, h_ctx, mla_w_in[i], mla_kv_norm[i], mla_w_ukv[i],
                                     mla_q_norm[i], mla_w_uq[i], mla_w_out[i], cos, sin, need_ctx)
        x = layer_norm(DEEPNORM_ALPHA * x + g1 * y_lat, ln_g[l, 0], ln_b[l, 0])
        h_lat = modulate(x, sh2, sc2).reshape(B * L, D)
        moe_args = (moe_w_group[l], moe_b_group[l], moe_w_expert[l], moe_b_expert[l],
                    moe_w_gate[l], moe_w_up[l], moe_w_down[l])
        if need_ctx:
            ctx = layer_norm(DEEPNORM_ALPHA * ctx + cg1 * y_ctx, ln_g[l, 0], ln_b[l, 0])
            h_ctx = modulate(ctx, csh2, csc2).reshape(B * Lc, D)
            f = hier_moe(jnp.concatenate([h_lat, h_ctx], axis=0), *moe_args)
            f_lat = f[:B * L].reshape(B, L, D)
            ctx = layer_norm(DEEPNORM_ALPHA * ctx + cg2 * f[B * L:].reshape(B, Lc, D),
                             ln_g[l, 1], ln_b[l, 1])
        else:
            f_lat = hier_moe(h_lat, *moe_args).reshape(B, L, D)
        x = layer_norm(DEEPNORM_ALPHA * x + g2 * f_lat, ln_g[l, 1], ln_b[l, 1])
    return x
```

```python
import functools
import math

import jax
import jax.numpy as jnp
from jax import lax
from jax.experimental import pallas as pl
from jax.experimental.pallas import tpu as pltpu

F32 = jnp.float32
BF16 = jnp.bfloat16
I32 = jnp.int32

D = 1024
NB = 8
SEQ = 2048
CTX = 256
T = SEQ + CTX
DEPTH = 2
N_MOD = 6
ALPHA = (2 * DEPTH) ** 0.25
LN_EPS = 1e-5
RMS_EPS = 1e-6

POOL_WINDOWS = (2, 4, 8, 16)
POOL_WIDTH = 256
POOL_GROUP_DIM = 64
GDN_HEADS = 6
GDN_DIM = 128
GDN_WIDTH = GDN_HEADS * GDN_DIM
GDN_CONV = 4
CHUNK = 64

MLA_HEADS = 8
MLA_NOPE = 128
MLA_ROPE = 64
MLA_V = 128
MLA_QK = MLA_NOPE + MLA_ROPE
MLA_Q_LORA = 384
MLA_KV_LORA = 256
GRID_W = 64
ROPE_BASE = 10000.0
ROPE_F = MLA_ROPE // 4

MOE_GROUPS = 4
MOE_PER_GROUP = 8
MOE_EXPERTS = 32
MOE_FF = 512

TT = CTX
NT = T // TT
NTL = SEQ // TT
TM = 256
PB = 512
HB = 2
NG = GDN_HEADS // HB
NC = T // CHUNK
NCC = CTX // CHUNK
TQ = 512
LANES = 128
VMEM_LIMIT = 56 * 1024 * 1024


def _dot(a, b):
    return jnp.dot(a, b, preferred_element_type=F32)


def _dot_nt(a, b):
    return lax.dot_general(a, b, (((1,), (1,)), ((), ())), preferred_element_type=F32)


def _dot_tn(a, b):
    return lax.dot_general(a, b, (((0,), (0,)), ((), ())), preferred_element_type=F32)


def _split2(x):
    hi = x.astype(BF16)
    lo = (x - hi.astype(F32)).astype(BF16)
    return hi, lo


def _dot3(a, b):
    ah, al = _split2(a)
    bh, bl = _split2(b)
    return _dot(ah, bh) + (_dot(ah, bl) + _dot(al, bh))


def _silu(x):
    return x * jax.nn.sigmoid(x)


def _softplus(x):
    return jnp.maximum(x, 0.0) + jnp.log1p(jnp.exp(-jnp.abs(x)))


def _layer_norm(v, g, b):
    mu = jnp.mean(v, axis=-1, keepdims=True)
    c = v - mu
    var = jnp.mean(c * c, axis=-1, keepdims=True)
    return c * lax.rsqrt(var + LN_EPS) * g + b


def _mod_slice(mod_ref, row, k):
    return mod_ref[pl.ds(row, 1), k * D:(k + 1) * D]


def _cparams(sem, vmem=VMEM_LIMIT):
    return pltpu.CompilerParams(dimension_semantics=sem, vmem_limit_bytes=vmem)


def _ada_kernel(cv_ref, w_ref, b_ref, o_ref):
    s = _silu(cv_ref[...])
    o_ref[0] = _dot3(s, w_ref[0]) + b_ref[0]


def _ada(cv, ada_w, ada_b):
    nblk = N_MOD
    return pl.pallas_call(
        _ada_kernel,
        out_shape=jax.ShapeDtypeStruct((DEPTH, 16, N_MOD * D), F32),
        grid=(DEPTH, nblk),
        in_specs=[
            pl.BlockSpec((16, D), lambda l, j: (0, 0)),
            pl.BlockSpec((1, D, D), lambda l, j: (l, 0, j)),
            pl.BlockSpec((1, 1, D), lambda l, j: (l, 0, j)),
        ],
        out_specs=pl.BlockSpec((1, 16, D), lambda l, j: (l, 0, j)),
        compiler_params=_cparams(("arbitrary", "arbitrary")),
        name="ada_mod",
    )(cv, ada_w, ada_b.reshape(DEPTH, 1, N_MOD * D))


W_MAIN = POOL_WIDTH + 4 * GDN_WIDTH
W_AB = NG * LANES


def _inproj0_kernel(x_ref, mod_ref, w_ref, wab_ref, pool_ref, main_ref, ab_ref):
    b = pl.program_id(0)
    t = pl.program_id(1)
    row = jnp.where(t == 0, NB, b)
    h = x_ref[0] * (1.0 + _mod_slice(mod_ref, row, 1)) + _mod_slice(mod_ref, row, 0)
    p = _dot(h.astype(BF16), w_ref[...])
    pool_ref[0] = p[:, :POOL_WIDTH]
    main_ref[0] = p[:, POOL_WIDTH:]
    ab_ref[0] = _dot3(h, wab_ref[...])


def _inproj0(xs, mod, w_main, w_ab):
    return pl.pallas_call(
        _inproj0_kernel,
        out_shape=(
            jax.ShapeDtypeStruct((NB, T, POOL_WIDTH), F32),
            jax.ShapeDtypeStruct((NB, T, 4 * GDN_WIDTH), F32),
            jax.ShapeDtypeStruct((NB, T, W_AB), F32),
        ),
        grid=(NB, NT),
        in_specs=[
            pl.BlockSpec((1, TT, D), lambda b, t: (b, t, 0)),
            pl.BlockSpec((16, N_MOD * D), lambda b, t: (0, 0)),
            pl.BlockSpec((D, W_MAIN), lambda b, t: (0, 0)),
            pl.BlockSpec((D, W_AB), lambda b, t: (0, 0)),
        ],
        out_specs=(
            pl.BlockSpec((1, TT, POOL_WIDTH), lambda b, t: (b, t, 0)),
            pl.BlockSpec((1, TT, 4 * GDN_WIDTH), lambda b, t: (b, t, 0)),
            pl.BlockSpec((1, TT, W_AB), lambda b, t: (b, t, 0)),
        ),
        compiler_params=_cparams(("arbitrary", "arbitrary")),
        name="inproj0",
    )(xs, mod, w_main, w_ab)


PAD_GAP = 16
PAD_CTX = PAD_GAP
PAD_LAT = PAD_CTX + CTX + 2 * PAD_GAP
PAD_ROWS = PAD_LAT + SEQ + PAD_GAP


def _fill_padded(pad_ref, src):
    w = pad_ref.shape[1]
    pad_ref[0:PAD_CTX, :] = jnp.zeros((PAD_CTX, w), F32)
    pad_ref[PAD_CTX + CTX:PAD_LAT, :] = jnp.zeros((2 * PAD_GAP, w), F32)
    pad_ref[PAD_LAT + SEQ:PAD_ROWS, :] = jnp.zeros((PAD_GAP, w), F32)
    pad_ref[PAD_CTX:PAD_CTX + CTX, :] = src(0, CTX)
    pad_ref[PAD_LAT:PAD_LAT + SEQ, :] = src(CTX, SEQ)


def _tile_pad_row(ti):
    return PAD_CTX if ti == 0 else PAD_LAT + (ti - 1) * TT


def _pool_kernel(u_ref, wbd_ref, scale_ref, o_ref, pad_ref):
    _fill_padded(pad_ref, lambda s, n: u_ref[0, s:s + n, :])
    lane = lax.broadcasted_iota(I32, (1, POOL_WIDTH), 1)
    grp = lane // POOL_GROUP_DIM
    win = jnp.zeros((1, POOL_WIDTH), I32)
    for g, w in enumerate(POOL_WINDOWS):
        win = jnp.where(grp == g, w, win)
    left = win // 2
    right = win - 1 - left
    for ti in range(NT):
        seg_len = CTX if ti == 0 else SEQ
        seg_t0 = 0 if ti == 0 else (ti - 1) * TT
        prow = _tile_pad_row(ti)
        tpos = seg_t0 + lax.broadcasted_iota(I32, (TT, 1), 0)
        acc = jnp.zeros((TT, POOL_WIDTH), F32)
        for j in range(-max(POOL_WINDOWS) // 2, max(POOL_WINDOWS) // 2):
            inwin = (j >= -left) & (j <= right)
            acc = acc + jnp.where(inwin, pad_ref[pl.ds(prow + j, TT), :], 0.0)
        cnt = jnp.minimum(tpos + right + 1, seg_len) - jnp.maximum(tpos - left, 0)
        dlt = acc / cnt.astype(F32) - pad_ref[pl.ds(prow, TT), :]
        y = _dot(dlt.astype(BF16), wbd_ref[...]) * scale_ref[...]
        o_ref[0, ti * TT:(ti + 1) * TT, :] = y.astype(BF16)


def _pool(pool_u, wbd, scale):
    return pl.pallas_call(
        _pool_kernel,
        out_shape=jax.ShapeDtypeStruct((NB, T, POOL_WIDTH), BF16),
        grid=(NB,),
        in_specs=[
            pl.BlockSpec((1, T, POOL_WIDTH), lambda b: (b, 0, 0)),
            pl.BlockSpec((POOL_WIDTH, POOL_WIDTH), lambda b: (0, 0)),
            pl.BlockSpec((1, POOL_WIDTH), lambda b: (0, 0)),
        ],
        out_specs=pl.BlockSpec((1, T, POOL_WIDTH), lambda b: (b, 0, 0)),
        scratch_shapes=[pltpu.VMEM((PAD_ROWS, POOL_WIDTH), F32)],
        compiler_params=_cparams(("arbitrary",)),
        name="pool",
    )(pool_u, wbd, scale)


HW = HB * GDN_DIM


def _tri_inverse(lm, eye):
    x = eye - lm
    p = lm
    for _ in range(5):
        p = _dot3(p, p)
        x = x + _dot3(x, p)
    return x


def _gdn_kernel(q_ref, k_ref, v_ref, z_ref, cwq_ref, cwk_ref, cwv_ref, ab_ref, alog_ref, dtb_ref, nw_ref,
                y_ref, pad_ref, qn_ref, kn_ref, vv_ref, u_ref, wq_ref, kd_ref, a_ref, egl_ref, s_ref):
    def conv(x_ref, cw_ref, dst_ref, l2, scale):
        _fill_padded(pad_ref, lambda s, n: x_ref[0, s:s + n, :])
        cw = cw_ref[...]
        for ti in range(NT):
            prow = _tile_pad_row(ti)
            acc = jnp.zeros((TT, HW), F32)
            for j in range(GDN_CONV):
                acc = acc + pad_ref[pl.ds(prow - 2 + j, TT), :] * cw[j:j + 1, :]
            y = _silu(acc)
            if l2:
                for hh in range(HB):
                    yh = y[:, hh * GDN_DIM:(hh + 1) * GDN_DIM]
                    yh = yh * lax.rsqrt(jnp.sum(yh * yh, axis=-1, keepdims=True) + RMS_EPS)
                    dst_ref[ti * TT:(ti + 1) * TT, hh * GDN_DIM:(hh + 1) * GDN_DIM] = yh * scale
            else:
                dst_ref[ti * TT:(ti + 1) * TT, :] = y

    conv(q_ref, cwq_ref, qn_ref, True, GDN_DIM ** -0.5)
    conv(k_ref, cwk_ref, kn_ref, True, 1.0)
    conv(v_ref, cwv_ref, vv_ref, False, 1.0)

    ri = lax.broadcasted_iota(I32, (CHUNK, CHUNK), 0)
    ci = lax.broadcasted_iota(I32, (CHUNK, CHUNK), 1)
    eye = (ri == ci).astype(F32)
    tri_f = (ci <= ri).astype(BF16)
    tri_b = (ci >= ri).astype(BF16)
    alog = alog_ref[0]
    dtb = dtb_ref[0]

    def phase1(c, carry):
        r0 = pl.multiple_of(c * CHUNK, CHUNK)
        abt = ab_ref[0, pl.ds(r0, CHUNK), :]
        g_all = -jnp.exp(alog) * _softplus(abt + dtb)
        beta_all = jax.nn.sigmoid(abt)
        g1 = g_all.astype(BF16)
        rem = g_all - g1.astype(F32)
        g2 = rem.astype(BF16)
        g3 = (rem - g2.astype(F32)).astype(BF16)
        gcs = (_dot(tri_f, g1) + (_dot(tri_f, g2) + _dot(tri_f, g3)),
               _dot(tri_b, g1) + (_dot(tri_b, g2) + _dot(tri_b, g3)))
        for d in range(2):
            gc = gcs[d]
            gct = gc.T
            incl = (ci <= ri) if d == 0 else (ci >= ri)
            strict = (ci < ri) if d == 0 else (ci > ri)
            for hh in range(HB):
                j = d * HB + hh
                hs = slice(hh * GDN_DIM, (hh + 1) * GDN_DIM)
                gcol = gc[:, j:j + 1]
                grow = gct[j:j + 1, :]
                glast = gc[CHUNK - 1:CHUNK, j:j + 1] if d == 0 else gc[0:1, j:j + 1]
                dec = jnp.where(incl, jnp.exp(jnp.where(incl, gcol - grow, 0.0)), 0.0)
                bcol = beta_all[:, 2 * HB + j:2 * HB + j + 1]
                kh = kn_ref[pl.ds(r0, CHUNK), hs]
                qh = qn_ref[pl.ds(r0, CHUNK), hs]
                vh = vv_ref[pl.ds(r0, CHUNK), hs]
                kb = kh * bcol
                khb = kh.astype(BF16)
                kk = _dot_nt(kb.astype(BF16), khb)
                qk = _dot_nt(qh.astype(BF16), khb)
                lm = jnp.where(strict, kk * dec, 0.0)
                tm = _tri_inverse(lm, eye)
                egc = jnp.exp(gcol)
                rhs = jnp.concatenate([vh * bcol, kb * egc], axis=1)
                uw = _dot3(tm, rhs)
                u_ref[d, pl.ds(r0, CHUNK), hs] = uw[:, :GDN_DIM]
                wq_ref[d, hh, pl.ds(2 * r0, CHUNK), :] = uw[:, GDN_DIM:].astype(BF16)
                wq_ref[d, hh, pl.ds(2 * r0 + CHUNK, CHUNK), :] = (qh * egc).astype(BF16)
                kd_ref[d, pl.ds(r0, CHUNK), hs] = (kh * jnp.exp(glast - gcol)).astype(BF16)
                a_ref[d, pl.ds(r0, CHUNK), hh * CHUNK:(hh + 1) * CHUNK] = (qk * dec).astype(BF16)
                egl_ref[d, hh, pl.ds(pl.multiple_of(c * 8, 8), 8), :] = jnp.broadcast_to(jnp.exp(glast), (8, LANES))
        return carry

    lax.fori_loop(0, NC, phase1, 0)

    s_ref[...] = jnp.zeros(s_ref.shape, F32)

    def phase2(step, carry):
        cb = jnp.where(step < NCC, NCC - 1 - step, NC + NCC - 1 - step)
        for d, c in ((0, step), (1, cb)):
            r0 = pl.multiple_of(c * CHUNK, CHUNK)
            for hh in range(HB):
                hs = slice(hh * GDN_DIM, (hh + 1) * GDN_DIM)
                st = s_ref[d, hh]
                stb = st.astype(BF16)
                r = _dot(wq_ref[d, hh, pl.ds(2 * r0, 2 * CHUNK), :], stb)
                vnew = (u_ref[d, pl.ds(r0, CHUNK), hs] - r[:CHUNK]).astype(BF16)
                o = r[CHUNK:] + _dot(a_ref[d, pl.ds(r0, CHUNK), hh * CHUNK:(hh + 1) * CHUNK], vnew)
                egl = egl_ref[d, hh, pl.ds(pl.multiple_of(c * 8, 8), 1), :]
                s_ref[d, hh] = st * egl + _dot_tn(kd_ref[d, pl.ds(r0, CHUNK), hs], vnew)
                u_ref[d, pl.ds(r0, CHUNK), hs] = o
        return carry

    lax.fori_loop(0, NC, phase2, 0)

    nw = nw_ref[...]
    for ti in range(NT):
        rs = slice(ti * TT, (ti + 1) * TT)
        o = u_ref[0, rs, :] + u_ref[1, rs, :]
        zz = _silu(z_ref[0, rs, :])
        for hh in range(HB):
            hs = slice(hh * GDN_DIM, (hh + 1) * GDN_DIM)
            oh = o[:, hs]
            oh = oh * lax.rsqrt(jnp.mean(oh * oh, axis=-1, keepdims=True) + RMS_EPS) * nw
            y_ref[0, rs, hs] = (oh * zz[:, hs]).astype(BF16)


def _gdn(p_main, conv_w, ab, alog_g, dtb_g, norm_w):
    nhb = GDN_WIDTH // HW
    blk = lambda off: pl.BlockSpec((1, T, HW), lambda b, g: (b, 0, off * nhb + g))
    cblk = lambda off: pl.BlockSpec((GDN_CONV, HW), lambda b, g: (0, off * nhb + g))
    return pl.pallas_call(
        _gdn_kernel,
        out_shape=jax.ShapeDtypeStruct((NB, T, GDN_WIDTH), BF16),
        grid=(NB, NG),
        in_specs=[
            blk(0), blk(1), blk(2), blk(3),
            cblk(0), cblk(1), cblk(2),
            pl.BlockSpec((1, T, LANES), lambda b, g: (b, 0, g)),
            pl.BlockSpec((1, 1, LANES), lambda b, g: (g, 0, 0)),
            pl.BlockSpec((1, 1, LANES), lambda b, g: (g, 0, 0)),
            pl.BlockSpec((1, GDN_DIM), lambda b, g: (0, 0)),
        ],
        out_specs=pl.BlockSpec((1, T, HW), lambda b, g: (b, 0, g)),
        scratch_shapes=[
            pltpu.VMEM((PAD_ROWS, HW), F32),
            pltpu.VMEM((T, HW), F32),
            pltpu.VMEM((T, HW), F32),
            pltpu.VMEM((T, HW), F32),
            pltpu.VMEM((2, T, HW), F32),
            pltpu.VMEM((2, HB, 2 * T, GDN_DIM), BF16),
            pltpu.VMEM((2, T, HW), BF16),
            pltpu.VMEM((2, T, HB * CHUNK), BF16),
            pltpu.VMEM((2, HB, NC * 8, LANES), F32),
            pltpu.VMEM((2, HB, GDN_DIM, GDN_DIM), F32),
        ],
        compiler_params=_cparams(("arbitrary", "arbitrary")),
        name="gdn",
    )(p_main, p_main, p_main, p_main, conv_w, conv_w, conv_w, ab, alog_g, dtb_g, norm_w)


NEG = -1e30


def _mix_kernel(n_act, ctx_tile0, *refs):
    act_refs = refs[:n_act]
    w_refs = refs[n_act:2 * n_act]
    x_ref, mod_ref, lng_ref, lnb_ref, wr_ref, br_ref, x1_ref, h2_ref, route_ref = refs[2 * n_act:]
    b = pl.program_id(0)
    t = pl.program_id(1)
    row = jnp.where(t == 0, NB, b) if ctx_tile0 else b
    y = _dot(act_refs[0][0], w_refs[0][...])
    for a_ref, w_ref in zip(act_refs[1:], w_refs[1:]):
        y = y + _dot(a_ref[0], w_ref[...])
    x1 = _layer_norm(ALPHA * x_ref[0] + _mod_slice(mod_ref, row, 2) * y, lng_ref[...], lnb_ref[...])
    x1_ref[0] = x1
    h2 = x1 * (1.0 + _mod_slice(mod_ref, row, 4)) + _mod_slice(mod_ref, row, 3)
    h2_ref[...] = h2
    logits = _dot3(h2, wr_ref[...]) + br_ref[...]
    lane = lax.broadcasted_iota(I32, (TT, LANES), 1)
    gl = jnp.where(lane < MOE_GROUPS, logits, NEG)
    gm = jnp.max(gl, axis=-1, keepdims=True)
    gsum = jnp.sum(jnp.exp(gl - gm), axis=-1, keepdims=True)
    g_val = 1.0 / gsum
    g_idx = jnp.min(jnp.where(gl == gm, lane, LANES), axis=-1, keepdims=True)
    sel = (lane >= MOE_GROUPS) & (lane < MOE_GROUPS + MOE_EXPERTS) & (((lane - MOE_GROUPS) // MOE_PER_GROUP) == g_idx)
    el = jnp.where(sel, logits, NEG)
    m1 = jnp.max(el, axis=-1, keepdims=True)
    i1 = jnp.min(jnp.where(el == m1, lane, LANES), axis=-1, keepdims=True)
    el2 = jnp.where(lane == i1, NEG, el)
    m2 = jnp.max(el2, axis=-1, keepdims=True)
    i2 = jnp.min(jnp.where(el2 == m2, lane, LANES), axis=-1, keepdims=True)
    ex2 = jnp.exp(m2 - m1)
    w1 = g_val / (1.0 + ex2)
    w2 = w1 * ex2
    e1 = (i1 - MOE_GROUPS).astype(F32)
    e2 = (i2 - MOE_GROUPS).astype(F32)
    route_ref[...] = jnp.where(lane == 0, e1, jnp.where(lane == 1, e2, jnp.where(lane == 2, w1, jnp.where(lane == 3, w2, 0.0))))


def _mix(acts, ws, xs, mod, ln_g, ln_b, wr, br, ctx_tile0):
    ntl = NT if ctx_tile0 else NTL
    toff = 0 if ctx_tile0 else 1
    n_tok = NB * ntl * TT
    act_specs = [pl.BlockSpec((1, TT, a.shape[2]), lambda b, t: (b, t, 0)) for a in acts]
    w_specs = [pl.BlockSpec(w.shape, lambda b, t: (0, 0)) for w in ws]
    return pl.pallas_call(
        functools.partial(_mix_kernel, len(acts), ctx_tile0),
        out_shape=(
            jax.ShapeDtypeStruct((NB, ntl * TT, D), F32),
            jax.ShapeDtypeStruct((n_tok, D), F32),
            jax.ShapeDtypeStruct((n_tok, LANES), F32),
        ),
        grid=(NB, ntl),
        in_specs=act_specs + w_specs + [
            pl.BlockSpec((1, TT, D), lambda b, t: (b, t + toff, 0)),
            pl.BlockSpec((16, N_MOD * D), lambda b, t: (0, 0)),
            pl.BlockSpec((1, D), lambda b, t: (0, 0)),
            pl.BlockSpec((1, D), lambda b, t: (0, 0)),
            pl.BlockSpec((D, LANES), lambda b, t: (0, 0)),
            pl.BlockSpec((1, LANES), lambda b, t: (0, 0)),
        ],
        out_specs=(
            pl.BlockSpec((1, TT, D), lambda b, t: (b, t, 0)),
            pl.BlockSpec((TT, D), lambda b, t: (b * ntl + t, 0)),
            pl.BlockSpec((TT, LANES), lambda b, t: (b * ntl + t, 0)),
        ),
        compiler_params=_cparams(("arbitrary", "arbitrary")),
        name="mix_ctx" if ctx_tile0 else "mix_lat",
    )(*acts, *ws, xs, mod, ln_g, ln_b, wr, br)


def _plan_kernel(route_ref, rank_ref, cnt_ref, carry_ref):
    i = pl.program_id(0)

    @pl.when(i == 0)
    def _():
        carry_ref[...] = jnp.zeros(carry_ref.shape, F32)

    r = route_ref[...]
    lane = lax.broadcasted_iota(I32, (PB, LANES), 1)
    lanef = lane.astype(F32)
    ri = lax.broadcasted_iota(I32, (PB, PB), 0)
    ci = lax.broadcasted_iota(I32, (PB, PB), 1)
    before = (ci < ri).astype(BF16)
    oh1 = jnp.where(lanef == r[:, 0:1], 1.0, 0.0)
    oh2 = jnp.where(lanef == r[:, 1:2], 1.0, 0.0)
    c0 = carry_ref[0:1, :]
    tot1 = jnp.sum(oh1, axis=0, keepdims=True)
    tot2 = jnp.sum(oh2, axis=0, keepdims=True)
    r1 = _dot(before, oh1.astype(BF16)) + c0
    r2 = _dot(before, oh2.astype(BF16)) + (c0 + tot1)
    rank1 = jnp.sum(r1 * oh1, axis=-1, keepdims=True)
    rank2 = jnp.sum(r2 * oh2, axis=-1, keepdims=True)
    rank_ref[...] = jnp.where(lane == 0, rank1, jnp.where(lane == 1, rank2, 0.0))
    cnew = jnp.broadcast_to(c0 + tot1 + tot2, carry_ref.shape)
    carry_ref[...] = cnew
    cnt_ref[...] = cnew


def _plan(route):
    n_tok = route.shape[0]
    return pl.pallas_call(
        _plan_kernel,
        out_shape=(jax.ShapeDtypeStruct((n_tok, LANES), F32), jax.ShapeDtypeStruct((8, LANES), F32)),
        grid=(n_tok // PB,),
        in_specs=[pl.BlockSpec((PB, LANES), lambda i: (i, 0))],
        out_specs=(pl.BlockSpec((PB, LANES), lambda i: (i, 0)), pl.BlockSpec((8, LANES), lambda i: (0, 0))),
        scratch_shapes=[pltpu.VMEM((8, LANES), F32)],
        compiler_params=_cparams(("arbitrary",)),
        name="moe_plan",
    )(route)


def _invmap_kernel(pos_ref, src_ref):
    n_rows = src_ref.shape[0]
    n_asg = pos_ref.shape[0]

    def clear(i, c):
        src_ref[i] = 0
        return c

    lax.fori_loop(0, n_rows, clear, 0, unroll=8)

    def put(a, c):
        src_ref[pos_ref[a]] = a // 2
        return c

    lax.fori_loop(0, n_asg, put, 0, unroll=8)


def _invmap(pos_flat, n_rows):
    return pl.pallas_call(
        _invmap_kernel,
        out_shape=jax.ShapeDtypeStruct((n_rows,), I32),
        in_specs=[pl.BlockSpec(memory_space=pltpu.SMEM)],
        out_specs=pl.BlockSpec(memory_space=pltpu.SMEM),
        name="moe_invmap",
    )(pos_flat)


def _ffn_kernel(te_ref, nu_ref, src_ref, h_hbm, wg_ref, wu_ref, wd_ref, y_ref, buf, wgb, wub, wdb, sem):
    i = pl.program_id(0)
    nu = nu_ref[0]

    def issue(tile, slot):
        base = tile * TM

        def body(r, c):
            tok = src_ref[base + r]
            pltpu.make_async_copy(h_hbm.at[pl.ds(tok, 1), :], buf.at[slot, pl.ds(r, 1), :], sem.at[slot]).start()
            return c

        lax.fori_loop(0, TM, body, 0, unroll=8)

    @pl.when(i == 0)
    def _():
        issue(0, 0)

    @pl.when(i + 1 < nu)
    def _():
        issue(i + 1, (i + 1) % 2)

    @pl.when(i < nu)
    def _():
        slot = i % 2
        pltpu.make_async_copy(h_hbm.at[pl.ds(0, TM), :], buf.at[slot], sem.at[slot]).wait()
        e = te_ref[i]
        ep = te_ref[jnp.maximum(i - 1, 0)]

        @pl.when((i == 0) | (e != ep))
        def _():
            wgb[...] = wg_ref[0].astype(BF16)
            wub[...] = wu_ref[0].astype(BF16)
            wdb[...] = wd_ref[0].astype(BF16)

        x = buf[slot].astype(BF16)
        a = _dot(x, wgb[...])
        bb = _dot(x, wub[...])
        act = (_silu(a) * bb).astype(BF16)
        y_ref[...] = _dot(act, wdb[...])

    @pl.when(i >= nu)
    def _():
        y_ref[...] = jnp.zeros(y_ref.shape, F32)


def _ffn(tile_expert, n_used, src, h2, wg, wu, wd):
    n_rows = src.shape[0]
    n_tiles = n_rows // TM
    return pl.pallas_call(
        _ffn_kernel,
        out_shape=jax.ShapeDtypeStruct((n_rows, D), F32),
        grid_spec=pltpu.PrefetchScalarGridSpec(
            num_scalar_prefetch=3,
            grid=(n_tiles,),
            in_specs=[
                pl.BlockSpec(memory_space=pl.ANY),
                pl.BlockSpec((1, D, MOE_FF), lambda i, te, nu, src: (te[i], 0, 0)),
                pl.BlockSpec((1, D, MOE_FF), lambda i, te, nu, src: (te[i], 0, 0)),
                pl.BlockSpec((1, MOE_FF, D), lambda i, te, nu, src: (te[i], 0, 0)),
            ],
            out_specs=pl.BlockSpec((TM, D), lambda i, te, nu, src: (i, 0)),
            scratch_shapes=[
                pltpu.VMEM((2, TM, D), F32),
                pltpu.VMEM((D, MOE_FF), BF16),
                pltpu.VMEM((D, MOE_FF), BF16),
                pltpu.VMEM((MOE_FF, D), BF16),
                pltpu.SemaphoreType.DMA((2,)),
            ],
        ),
        compiler_params=_cparams(("arbitrary",)),
        name="moe_ffn",
    )(tile_expert, n_used, src, h2, wg, wu, wd)


def _combine_kernel(ntl, ctx_tile0, pos_ref, y_hbm, x1_ref, route_ref, mod_ref, lng_ref, lnb_ref, o_ref, buf, sem):
    b = pl.program_id(0)
    t = pl.program_id(1)
    flat = b * ntl + t
    n_tiles = NB * ntl

    def issue(tile, slot):
        base = tile * (2 * TT)

        def body(r, c):
            for k in range(2):
                p = pos_ref[base + 2 * r + k]
                pltpu.make_async_copy(y_hbm.at[pl.ds(p, 1), :], buf.at[slot, pl.ds(k * TT + r, 1), :], sem.at[slot]).start()
            return c

        lax.fori_loop(0, TT, body, 0, unroll=4)

    @pl.when(flat == 0)
    def _():
        issue(0, 0)

    @pl.when(flat + 1 < n_tiles)
    def _():
        issue(flat + 1, (flat + 1) % 2)

    slot = flat % 2
    pltpu.make_async_copy(y_hbm.at[pl.ds(0, 2 * TT), :], buf.at[slot], sem.at[slot]).wait()
    row = jnp.where(t == 0, NB, b) if ctx_tile0 else b
    r = route_ref[...]
    f = r[:, 2:3] * buf[slot, 0:TT, :] + r[:, 3:4] * buf[slot, TT:2 * TT, :]
    o_ref[0] = _layer_norm(ALPHA * x1_ref[0] + _mod_slice(mod_ref, row, 5) * f, lng_ref[...], lnb_ref[...])


def _combine(pos_flat, ys, x1, route, mod, ln_g, ln_b, ctx_tile0):
    ntl = NT if ctx_tile0 else NTL
    return pl.pallas_call(
        functools.partial(_combine_kernel, ntl, ctx_tile0),
        out_shape=jax.ShapeDtypeStruct((NB, ntl * TT, D), F32),
        grid_spec=pltpu.PrefetchScalarGridSpec(
            num_scalar_prefetch=1,
            grid=(NB, ntl),
            in_specs=[
                pl.BlockSpec(memory_space=pl.ANY),
                pl.BlockSpec((1, TT, D), lambda b, t, pos: (b, t, 0)),
                pl.BlockSpec((TT, LANES), lambda b, t, pos: (b * ntl + t, 0)),
                pl.BlockSpec((16, N_MOD * D), lambda b, t, pos: (0, 0)),
                pl.BlockSpec((1, D), lambda b, t, pos: (0, 0)),
                pl.BlockSpec((1, D), lambda b, t, pos: (0, 0)),
            ],
            out_specs=pl.BlockSpec((1, TT, D), lambda b, t, pos: (b, t, 0)),
            scratch_shapes=[pltpu.VMEM((2, 2 * TT, D), F32), pltpu.SemaphoreType.DMA((2,))],
        ),
        compiler_params=_cparams(("arbitrary", "arbitrary")),
        name="moe_combine_ctx" if ctx_tile0 else "moe_combine_lat",
    )(pos_flat, ys, x1, route, mod, ln_g, ln_b)


def _moe(h2, route, x1, mod, ln_g, ln_b, wg, wu, wd, ctx_tile0):
    n_tok = h2.shape[0]
    n_tiles = (2 * n_tok) // TM + MOE_EXPERTS
    rank, cnt = _plan(route)
    counts = cnt[0, :MOE_EXPERTS].astype(I32)
    tiles_e = (counts + TM - 1) // TM
    tile_end = jnp.cumsum(tiles_e)
    tile_start = tile_end - tiles_e
    n_used = tile_end[-1]
    tj = jnp.arange(n_tiles, dtype=I32)
    te = jnp.minimum(jnp.sum((tile_end[None, :] <= tj[:, None]).astype(I32), axis=1), MOE_EXPERTS - 1)
    te_last = jnp.max(jnp.where(tj < n_used, te, 0))
    tile_expert = jnp.where(tj < n_used, te, te_last)
    eid = route[:, 0:2].astype(I32)
    pos = tile_start[eid] * TM + rank[:, 0:2].astype(I32)
    pos_flat = pos.reshape(-1)
    src = _invmap(pos_flat, n_tiles * TM)
    ys = _ffn(tile_expert, n_used.reshape(1), src, h2, wg, wu, wd)
    return _combine(pos_flat, ys, x1, route, mod, ln_g, ln_b, ctx_tile0)


W_IN1 = MLA_KV_LORA + MLA_ROPE + MLA_Q_LORA + MLA_ROPE
Q_OFF = MLA_KV_LORA + MLA_ROPE
KRS_OFF = Q_OFF + MLA_Q_LORA
SM_SCALE = MLA_QK ** -0.5


def _rms(v, w):
    return v * lax.rsqrt(jnp.mean(v * v, axis=-1, keepdims=True) + RMS_EPS) * w


def _mla_proj_kernel(x_ref, mod_ref, win_ref, kvn_ref, qnw_ref, wkn_ref, wkv_ref, wqn_ref, wqr_ref, wqrs_ref,
                     cosk_ref, sink_ref, cosq_ref, sinq_ref, q_out, k_out, v_out):
    b = pl.program_id(0)
    t = pl.program_id(1)
    row = jnp.where(t == 0, NB, b)
    h = x_ref[0] * (1.0 + _mod_slice(mod_ref, row, 1)) + _mod_slice(mod_ref, row, 0)
    p = _dot(h.astype(BF16), win_ref[...])
    ckv = _rms(p[:, :MLA_KV_LORA], kvn_ref[...]).astype(BF16)
    kn = _dot(ckv, wkn_ref[...])
    vv = _dot(ckv, wkv_ref[...])
    kr = p[:, MLA_KV_LORA:Q_OFF]
    krs = p[:, KRS_OFF:KRS_OFF + MLA_ROPE]
    kr = jnp.where(t > 0, kr * cosk_ref[...] + krs * sink_ref[...], kr).astype(BF16)
    ql = _rms(p[:, Q_OFF:KRS_OFF], qnw_ref[...]).astype(BF16)
    qn = _dot(ql, wqn_ref[...]) * SM_SCALE
    qr = (_dot(ql, wqr_ref[...]) * cosq_ref[...] + _dot(ql, wqrs_ref[...]) * sinq_ref[...]) * SM_SCALE
    for hd in range(MLA_HEADS):
        k_out[0, hd, :, 0:MLA_NOPE] = kn[:, hd * MLA_NOPE:(hd + 1) * MLA_NOPE].astype(BF16)
        k_out[0, hd, :, MLA_NOPE:MLA_QK] = kr
        v_out[0, hd] = vv[:, hd * MLA_V:(hd + 1) * MLA_V].astype(BF16)
        q_out[0, hd, :, 0:MLA_NOPE] = qn[:, hd * MLA_NOPE:(hd + 1) * MLA_NOPE].astype(BF16)
        q_out[0, hd, :, MLA_NOPE:MLA_QK] = qr[:, hd * MLA_ROPE:(hd + 1) * MLA_ROPE].astype(BF16)


def _mla_proj(xs, mod, win, kvn, qnw, wkn, wkv, wqn, wqr, wqrs, cosk, sink, cosq, sinq):
    full = lambda a: pl.BlockSpec(a.shape, lambda b, t: (0, 0))
    lat = lambda w: pl.BlockSpec((TT, w), lambda b, t: (jnp.maximum(t - 1, 0), 0))
    return pl.pallas_call(
        _mla_proj_kernel,
        out_shape=(
            jax.ShapeDtypeStruct((NB, MLA_HEADS, SEQ, MLA_QK), BF16),
            jax.ShapeDtypeStruct((NB, MLA_HEADS, T, MLA_QK), BF16),
            jax.ShapeDtypeStruct((NB, MLA_HEADS, T, MLA_V), BF16),
        ),
        grid=(NB, NT),
        in_specs=[
            pl.BlockSpec((1, TT, D), lambda b, t: (b, t, 0)),
            pl.BlockSpec((16, N_MOD * D), lambda b, t: (0, 0)),
            full(win), full(kvn), full(qnw), full(wkn), full(wkv), full(wqn), full(wqr), full(wqrs),
            lat(MLA_ROPE), lat(MLA_ROPE), lat(MLA_HEADS * MLA_ROPE), lat(MLA_HEADS * MLA_ROPE),
        ],
        out_specs=(
            pl.BlockSpec((1, MLA_HEADS, TT, MLA_QK), lambda b, t: (b, 0, jnp.maximum(t - 1, 0), 0)),
            pl.BlockSpec((1, MLA_HEADS, TT, MLA_QK), lambda b, t: (b, 0, t, 0)),
            pl.BlockSpec((1, MLA_HEADS, TT, MLA_V), lambda b, t: (b, 0, t, 0)),
        ),
        compiler_params=_cparams(("arbitrary", "arbitrary")),
        name="mla_proj",
    )(xs, mod, win, kvn, qnw, wkn, wkv, wqn, wqr, wqrs, cosk, sink, cosq, sinq)


def _attn_kernel(q_ref, k_ref, v_ref, o_ref):
    s = _dot_nt(q_ref[0, 0], k_ref[0, 0])
    m = jnp.max(s, axis=-1, keepdims=True)
    p = jnp.exp(s - m)
    l = jnp.sum(p, axis=-1, keepdims=True)
    o = _dot(p.astype(BF16), v_ref[0, 0])
    o_ref[0] = (o / l).astype(BF16)


def _attn(q, k, v):
    return pl.pallas_call(
        _attn_kernel,
        out_shape=jax.ShapeDtypeStruct((NB, SEQ, MLA_HEADS * MLA_V), BF16),
        grid=(NB, MLA_HEADS, SEQ // TQ),
        in_specs=[
            pl.BlockSpec((1, 1, TQ, MLA_QK), lambda b, h, i: (b, h, i, 0)),
            pl.BlockSpec((1, 1, T, MLA_QK), lambda b, h, i: (b, h, 0, 0)),
            pl.BlockSpec((1, 1, T, MLA_V), lambda b, h, i: (b, h, 0, 0)),
        ],
        out_specs=pl.BlockSpec((1, TQ, MLA_V), lambda b, h, i: (b, i, h)),
        compiler_params=_cparams(("arbitrary", "arbitrary", "arbitrary")),
        name="mla_attn",
    )(q, k, v)


def _rope_tables():
    n = SEQ
    rowp = (jnp.arange(n) // GRID_W).astype(F32)
    colp = (jnp.arange(n) % GRID_W).astype(F32)
    inv_freq = ROPE_BASE ** (-jnp.arange(ROPE_F, dtype=F32) / ROPE_F)
    ar = rowp[:, None] * inv_freq
    ac = colp[:, None] * inv_freq
    cos = jnp.concatenate([jnp.cos(ar), jnp.cos(ar), jnp.cos(ac), jnp.cos(ac)], axis=1)
    sin = jnp.concatenate([-jnp.sin(ar), jnp.sin(ar), -jnp.sin(ac), jnp.sin(ac)], axis=1)
    return cos, sin


def _rope_swap_perm():
    f = ROPE_F
    return jnp.concatenate([jnp.arange(f, 2 * f), jnp.arange(0, f), jnp.arange(3 * f, 4 * f), jnp.arange(2 * f, 3 * f)])


def _router_params(w_group, b_group, w_expert, b_expert):
    pad = LANES - MOE_GROUPS - MOE_EXPERTS
    wr = jnp.concatenate([w_group, w_expert, jnp.zeros((D, pad), F32)], axis=1)
    br = jnp.concatenate([b_group, b_expert, jnp.zeros((pad,), F32)]).reshape(1, LANES)
    return wr, br


def kernel(x, c, ctx, c_ctx, ada_w, ada_b, ln_g, ln_b, ab_w_in, pool_w, pool_scale, gdn_conv_w, gdn_a_log, gdn_dt_bias, gdn_norm_w, ab_w_out, mla_w_in, mla_kv_norm, mla_w_ukv, mla_q_norm, mla_w_uq, mla_w_out, moe_w_group, moe_b_group, moe_w_expert, moe_b_expert, moe_w_gate, moe_w_up, moe_w_down):
    assert x.shape == (NB, SEQ, D) and ctx.shape == (NB, CTX, D)
    xs = jnp.concatenate([ctx, x], axis=1)
    cv = jnp.concatenate([c, c_ctx[None, :], jnp.zeros((16 - NB - 1, D), F32)], axis=0)
    mod = _ada(cv, ada_w, ada_b)

    w_in = ab_w_in[0]
    w_main = w_in[:, :W_MAIN].astype(BF16)
    o_ab = W_MAIN
    ab_cols = []
    for g in range(NG):
        idx = [o_ab + kind * 2 * GDN_HEADS + d * GDN_HEADS + g * HB + hh
               for kind in range(2) for d in range(2) for hh in range(HB)]
        ab_cols.append(jnp.concatenate([w_in[:, jnp.array(idx)], jnp.zeros((D, LANES - len(idx)), F32)], axis=1))
    w_ab = jnp.concatenate(ab_cols, axis=1)

    def gate_rows(p):
        rows = []
        for g in range(NG):
            vals = jnp.stack([p[d, g * HB + hh] for d in range(2) for hh in range(HB)])
            rows.append(jnp.concatenate([vals, jnp.zeros((LANES - 2 * HB,), F32)]))
        return jnp.stack(rows).reshape(NG, 1, LANES)

    alog_g = gate_rows(gdn_a_log[0])
    dtb_g = gate_rows(gdn_dt_bias[0])
    wbd = jax.scipy.linalg.block_diag(*[pool_w[0, g] for g in range(len(POOL_WINDOWS))]).astype(BF16)

    pool_u, p_main, ab = _inproj0(xs, mod[0], w_main, w_ab)
    pool_y = _pool(pool_u, wbd, pool_scale[0].reshape(1, POOL_WIDTH))
    gdn_y = _gdn(p_main, gdn_conv_w[0], ab, alog_g, dtb_g, gdn_norm_w[0].reshape(1, GDN_DIM))

    w_out0 = ab_w_out[0].astype(BF16)
    wr0, br0 = _router_params(moe_w_group[0], moe_b_group[0], moe_w_expert[0], moe_b_expert[0])
    x1, h2, route = _mix([pool_y, gdn_y], [w_out0[:POOL_WIDTH], w_out0[POOL_WIDTH:]], xs, mod[0],
                         ln_g[0, 0].reshape(1, D), ln_b[0, 0].reshape(1, D), wr0, br0, True)
    xs = _moe(h2, route, x1, mod[0], ln_g[0, 1].reshape(1, D), ln_b[0, 1].reshape(1, D),
              moe_w_gate[0], moe_w_up[0], moe_w_down[0], True)

    perm = _rope_swap_perm()
    w_in1 = mla_w_in[0]
    win = jnp.concatenate([w_in1, w_in1[:, MLA_KV_LORA:Q_OFF][:, perm]], axis=1).astype(BF16)
    ukv = mla_w_ukv[0].reshape(MLA_KV_LORA, MLA_HEADS, MLA_NOPE + MLA_V)
    wkn = ukv[:, :, :MLA_NOPE].reshape(MLA_KV_LORA, MLA_HEADS * MLA_NOPE).astype(BF16)
    wkv = ukv[:, :, MLA_NOPE:].reshape(MLA_KV_LORA, MLA_HEADS * MLA_V).astype(BF16)
    uq = mla_w_uq[0].reshape(MLA_Q_LORA, MLA_HEADS, MLA_QK)
    wqn = uq[:, :, :MLA_NOPE].reshape(MLA_Q_LORA, MLA_HEADS * MLA_NOPE).astype(BF16)
    wqr = uq[:, :, MLA_NOPE:].reshape(MLA_Q_LORA, MLA_HEADS * MLA_ROPE).astype(BF16)
    wqrs = uq[:, :, MLA_NOPE:][:, :, perm].reshape(MLA_Q_LORA, MLA_HEADS * MLA_ROPE).astype(BF16)
    cosk, sink = _rope_tables()
    cosq = jnp.tile(cosk, (1, MLA_HEADS))
    sinq = jnp.tile(sink, (1, MLA_HEADS))

    q, k, v = _mla_proj(xs, mod[1], win, mla_kv_norm[0].reshape(1, MLA_KV_LORA), mla_q_norm[0].reshape(1, MLA_Q_LORA),
                        wkn, wkv, wqn, wqr, wqrs, cosk, sink, cosq, sinq)
    att = _attn(q, k, v)
    wr1, br1 = _router_params(moe_w_group[1], moe_b_group[1], moe_w_expert[1], moe_b_expert[1])
    x1, h2, route = _mix([att], [mla_w_out[0].astype(BF16)], xs, mod[1],
                         ln_g[1, 0].reshape(1, D), ln_b[1, 0].reshape(1, D), wr1, br1, False)
    return _moe(h2, route, x1, mod[1], ln_g[1, 1].reshape(1, D), ln_b[1, 1].reshape(1, D),
                moe_w_gate[1], moe_w_up[1], moe_w_down[1], False)
```

```python
import functools
import math

import jax
import jax.numpy as jnp
from jax import lax
from jax.experimental import pallas as pl
from jax.experimental.pallas import tpu as pltpu

F32 = jnp.float32
BF16 = jnp.bfloat16
I32 = jnp.int32

D = 1024
NB = 8
SEQ = 2048
CTX = 256
T = SEQ + CTX
DEPTH = 2
N_MOD = 6
ALPHA = (2 * DEPTH) ** 0.25
LN_EPS = 1e-5
RMS_EPS = 1e-6

POOL_WINDOWS = (2, 4, 8, 16)
POOL_WIDTH = 256
POOL_GROUP_DIM = 64
GDN_HEADS = 6
GDN_DIM = 128
GDN_WIDTH = GDN_HEADS * GDN_DIM
GDN_CONV = 4
CHUNK = 64

MLA_HEADS = 8
MLA_NOPE = 128
MLA_ROPE = 64
MLA_V = 128
MLA_QK = MLA_NOPE + MLA_ROPE
MLA_Q_LORA = 384
MLA_KV_LORA = 256
GRID_W = 64
ROPE_BASE = 10000.0
ROPE_F = MLA_ROPE // 4

MOE_GROUPS = 4
MOE_PER_GROUP = 8
MOE_EXPERTS = 32
MOE_FF = 512

TT = CTX
NT = T // TT
NTL = SEQ // TT
TM = 256
PB = 512
HB = 2
NG = GDN_HEADS // HB
NC = T // CHUNK
NCC = CTX // CHUNK
TQ = 512
LANES = 128
VMEM_LIMIT = 56 * 1024 * 1024


def _dot(a, b):
    return jnp.dot(a, b, preferred_element_type=F32)


def _dot_nt(a, b):
    return lax.dot_general(a, b, (((1,), (1,)), ((), ())), preferred_element_type=F32)


def _dot_tn(a, b):
    return lax.dot_general(a, b, (((0,), (0,)), ((), ())), preferred_element_type=F32)


def _split2(x):
    hi = x.astype(BF16)
    lo = (x - hi.astype(F32)).astype(BF16)
    return hi, lo


def _dot3(a, b):
    ah, al = _split2(a)
    bh, bl = _split2(b)
    return _dot(ah, bh) + (_dot(ah, bl) + _dot(al, bh))


def _silu(x):
    return x * jax.nn.sigmoid(x)


def _softplus(x):
    return jnp.maximum(x, 0.0) + jnp.log1p(jnp.exp(-jnp.abs(x)))


def _layer_norm(v, g, b):
    mu = jnp.mean(v, axis=-1, keepdims=True)
    c = v - mu
    var = jnp.mean(c * c, axis=-1, keepdims=True)
    return c * lax.rsqrt(var + LN_EPS) * g + b


def _mod_slice(mod_ref, row, k):
    return mod_ref[pl.ds(row, 1), k * D:(k + 1) * D]


def _cparams(sem, vmem=VMEM_LIMIT):
    return pltpu.CompilerParams(dimension_semantics=sem, vmem_limit_bytes=vmem)


def _ada_kernel(cv_ref, w_ref, b_ref, o_ref):
    s = _silu(cv_ref[...])
    o_ref[0] = _dot3(s, w_ref[0]) + b_ref[0]


def _ada(cv, ada_w, ada_b):
    nblk = N_MOD
    return pl.pallas_call(
        _ada_kernel,
        out_shape=jax.ShapeDtypeStruct((DEPTH, 16, N_MOD * D), F32),
        grid=(DEPTH, nblk),
        in_specs=[
            pl.BlockSpec((16, D), lambda l, j: (0, 0)),
            pl.BlockSpec((1, D, D), lambda l, j: (l, 0, j)),
            pl.BlockSpec((1, 1, D), lambda l, j: (l, 0, j)),
        ],
        out_specs=pl.BlockSpec((1, 16, D), lambda l, j: (l, 0, j)),
        compiler_params=_cparams(("arbitrary", "arbitrary")),
        name="ada_mod",
    )(cv, ada_w, ada_b.reshape(DEPTH, 1, N_MOD * D))


W_MAIN = POOL_WIDTH + 4 * GDN_WIDTH
W_AB = NG * LANES


def _inproj0_kernel(x_ref, mod_ref, w_ref, wab_ref, pool_ref, main_ref, ab_ref):
    b = pl.program_id(0)
    t = pl.program_id(1)
    row = jnp.where(t == 0, NB, b)
    h = x_ref[0] * (1.0 + _mod_slice(mod_ref, row, 1)) + _mod_slice(mod_ref, row, 0)
    p = _dot(h.astype(BF16), w_ref[...])
    pool_ref[0] = p[:, :POOL_WIDTH]
    main_ref[0] = p[:, POOL_WIDTH:]
    ab_ref[0] = _dot3(h, wab_ref[...])


def _inproj0(xs, mod, w_main, w_ab):
    return pl.pallas_call(
        _inproj0_kernel,
        out_shape=(
            jax.ShapeDtypeStruct((NB, T, POOL_WIDTH), F32),
            jax.ShapeDtypeStruct((NB, T, 4 * GDN_WIDTH), F32),
            jax.ShapeDtypeStruct((NB, T, W_AB), F32),
        ),
        grid=(NB, NT),
        in_specs=[
            pl.BlockSpec((1, TT, D), lambda b, t: (b, t, 0)),
            pl.BlockSpec((16, N_MOD * D), lambda b, t: (0, 0)),
            pl.BlockSpec((D, W_MAIN), lambda b, t: (0, 0)),
            pl.BlockSpec((D, W_AB), lambda b, t: (0, 0)),
        ],
        out_specs=(
            pl.BlockSpec((1, TT, POOL_WIDTH), lambda b, t: (b, t, 0)),
            pl.BlockSpec((1, TT, 4 * GDN_WIDTH), lambda b, t: (b, t, 0)),
            pl.BlockSpec((1, TT, W_AB), lambda b, t: (b, t, 0)),
        ),
        compiler_params=_cparams(("arbitrary", "arbitrary")),
        name="inproj0",
    )(xs, mod, w_main, w_ab)


PAD_GAP = 16
PAD_CTX = PAD_GAP
PAD_LAT = PAD_CTX + CTX + 2 * PAD_GAP
PAD_ROWS = PAD_LAT + SEQ + PAD_GAP


def _fill_padded(pad_ref, src):
    w = pad_ref.shape[1]
    pad_ref[0:PAD_CTX, :] = jnp.zeros((PAD_CTX, w), F32)
    pad_ref[PAD_CTX + CTX:PAD_LAT, :] = jnp.zeros((2 * PAD_GAP, w), F32)
    pad_ref[PAD_LAT + SEQ:PAD_ROWS, :] = jnp.zeros((PAD_GAP, w), F32)
    pad_ref[PAD_CTX:PAD_CTX + CTX, :] = src(0, CTX)
    pad_ref[PAD_LAT:PAD_LAT + SEQ, :] = src(CTX, SEQ)


def _tile_pad_row(ti):
    return PAD_CTX if ti == 0 else PAD_LAT + (ti - 1) * TT


def _pool_kernel(u_ref, wbd_ref, scale_ref, o_ref, pad_ref):
    _fill_padded(pad_ref, lambda s, n: u_ref[0, s:s + n, :])
    lane = lax.broadcasted_iota(I32, (1, POOL_WIDTH), 1)
    grp = lane // POOL_GROUP_DIM
    win = jnp.zeros((1, POOL_WIDTH), I32)
    for g, w in enumerate(POOL_WINDOWS):
        win = jnp.where(grp == g, w, win)
    left = win // 2
    right = win - 1 - left
    for ti in range(NT):
        seg_len = CTX if ti == 0 else SEQ
        seg_t0 = 0 if ti == 0 else (ti - 1) * TT
        prow = _tile_pad_row(ti)
        tpos = seg_t0 + lax.broadcasted_iota(I32, (TT, 1), 0)
        acc = jnp.zeros((TT, POOL_WIDTH), F32)
        for j in range(-max(POOL_WINDOWS) // 2, max(POOL_WINDOWS) // 2):
            inwin = (j >= -left) & (j <= right)
            acc = acc + jnp.where(inwin, pad_ref[pl.ds(prow + j, TT), :], 0.0)
        cnt = jnp.minimum(tpos + right + 1, seg_len) - jnp.maximum(tpos - left, 0)
        dlt = acc / cnt.astype(F32) - pad_ref[pl.ds(prow, TT), :]
        y = _dot(dlt.astype(BF16), wbd_ref[...]) * scale_ref[...]
        o_ref[0, ti * TT:(ti + 1) * TT, :] = y.astype(BF16)


def _pool(pool_u, wbd, scale):
    return pl.pallas_call(
        _pool_kernel,
        out_shape=jax.ShapeDtypeStruct((NB, T, POOL_WIDTH), BF16),
        grid=(NB,),
        in_specs=[
            pl.BlockSpec((1, T, POOL_WIDTH), lambda b: (b, 0, 0)),
            pl.BlockSpec((POOL_WIDTH, POOL_WIDTH), lambda b: (0, 0)),
            pl.BlockSpec((1, POOL_WIDTH), lambda b: (0, 0)),
        ],
        out_specs=pl.BlockSpec((1, T, POOL_WIDTH), lambda b: (b, 0, 0)),
        scratch_shapes=[pltpu.VMEM((PAD_ROWS, POOL_WIDTH), F32)],
        compiler_params=_cparams(("arbitrary",)),
        name="pool",
    )(pool_u, wbd, scale)


HW = HB * GDN_DIM


NCH = 2 * HB
PW = NCH * CHUNK
GCH = 4


def _block_diag(xp, blk_masks):
    return jnp.concatenate([jnp.where(m, xp, 0.0) for m in blk_masks], axis=0).astype(BF16)


def _gdn_kernel(q_ref, k_ref, v_ref, z_ref, cwq_ref, cwk_ref, cwv_ref, ab_ref, alog_ref, dtb_ref, nw_ref,
                y_ref, pad_ref, qn_ref, kn_ref, vv_ref, m_ref, b_ref, qt_ref, o_ref, egl_ref, s_ref):
    def conv(x_ref, cw_ref, dst_ref, l2, scale):
        _fill_padded(pad_ref, lambda s, n: x_ref[0, s:s + n, :])
        cw = cw_ref[...]
        for ti in range(NT):
            prow = _tile_pad_row(ti)
            acc = jnp.zeros((TT, HW), F32)
            for j in range(GDN_CONV):
                acc = acc + pad_ref[pl.ds(prow - 2 + j, TT), :] * cw[j:j + 1, :]
            y = _silu(acc)
            if l2:
                for hh in range(HB):
                    yh = y[:, hh * GDN_DIM:(hh + 1) * GDN_DIM]
                    yh = yh * lax.rsqrt(jnp.sum(yh * yh, axis=-1, keepdims=True) + RMS_EPS)
                    dst_ref[ti * TT:(ti + 1) * TT, hh * GDN_DIM:(hh + 1) * GDN_DIM] = yh * scale
            else:
                dst_ref[ti * TT:(ti + 1) * TT, :] = y

    conv(q_ref, cwq_ref, qn_ref, True, GDN_DIM ** -0.5)
    conv(k_ref, cwk_ref, kn_ref, True, 1.0)
    conv(v_ref, cwv_ref, vv_ref, False, 1.0)

    ri = lax.broadcasted_iota(I32, (CHUNK, CHUNK), 0)
    ci = lax.broadcasted_iota(I32, (CHUNK, CHUNK), 1)
    tri2 = jnp.concatenate([(ci <= ri).astype(BF16), (ci >= ri).astype(BF16)], axis=0)
    rowp = lax.broadcasted_iota(I32, (CHUNK, PW), 0)
    lanep = lax.broadcasted_iota(I32, (CHUNK, PW), 1)
    blk = lanep // CHUNK
    colp = lanep - blk * CHUNK
    is_fwd = blk < HB
    ahead = jnp.where(is_fwd, rowp - colp, colp - rowp)
    incl_p = ahead >= 0
    strict_p = ahead > 0
    eye_p = (colp == rowp).astype(F32)
    eye_d = (lax.broadcasted_iota(I32, (GDN_DIM, GDN_DIM), 0)
             == lax.broadcasted_iota(I32, (GDN_DIM, GDN_DIM), 1)).astype(BF16)
    blk_masks = [blk == i for i in range(NCH)]
    alog = alog_ref[0]
    dtb = dtb_ref[0]

    def split3(v):
        h1 = v.astype(BF16)
        rem = v - h1.astype(F32)
        h2 = rem.astype(BF16)
        return h1, h2, (rem - h2.astype(F32)).astype(BF16)

    def phase1(it, carry):
        cs = [it * GCH + j for j in range(GCH)]
        r0 = [pl.multiple_of(c * CHUNK, CHUNK) for c in cs]
        m0 = [pl.multiple_of(c * GDN_DIM, GDN_DIM) for c in cs]
        e0 = [pl.multiple_of(c * 8, 8) for c in cs]
        G = range(GCH)
        abt = [ab_ref[0, pl.ds(r0[j], CHUNK), :] for j in G]
        g_all = [-jnp.exp(alog) * _softplus(abt[j] + dtb) for j in G]
        beta_all = [jax.nn.sigmoid(abt[j]) for j in G]
        g3 = [split3(g_all[j]) for j in G]
        gcs = [_dot(tri2, g3[j][0]) + (_dot(tri2, g3[j][1]) + _dot(tri2, g3[j][2])) for j in G]
        gb3 = [split3(jnp.where(strict_p, jnp.concatenate(
            [jnp.broadcast_to(g_all[j][:, i:i + 1], (CHUNK, CHUNK)) for i in range(NCH)], axis=1), 0.0)) for j in G]
        dif2 = [_dot(tri2, gb3[j][0]) + (_dot(tri2, gb3[j][1]) + _dot(tri2, gb3[j][2])) for j in G]
        kh = [[kn_ref[pl.ds(r0[j], CHUNK), hh * GDN_DIM:(hh + 1) * GDN_DIM] for hh in range(HB)] for j in G]
        qh = [[qn_ref[pl.ds(r0[j], CHUNK), hh * GDN_DIM:(hh + 1) * GDN_DIM] for hh in range(HB)] for j in G]
        vh = [[vv_ref[pl.ds(r0[j], CHUNK), hh * GDN_DIM:(hh + 1) * GDN_DIM] for hh in range(HB)] for j in G]
        gram = [[_dot_nt(jnp.concatenate([kh[j][hh].astype(BF16), qh[j][hh].astype(BF16), eye_d], axis=0),
                         kh[j][hh].astype(BF16)) for hh in range(HB)] for j in G]
        gcol1, bcol1, glast, egc1, kdsc = [], [], [], [], []
        for j in G:
            gcol1.append([]); bcol1.append([]); glast.append([]); egc1.append([]); kdsc.append([])
            for i in range(NCH):
                d = i // HB
                gc = gcs[j][d * CHUNK:(d + 1) * CHUNK]
                gcol1[j].append(gc[:, i:i + 1])
                bcol1[j].append(beta_all[j][:, NCH + i:NCH + i + 1])
                glast[j].append(gc[CHUNK - 1:CHUNK, i:i + 1] if d == 0 else gc[0:1, i:i + 1])
                egc1[j].append(jnp.exp(gcol1[j][i]))
                kdsc[j].append(jnp.exp(glast[j][i] - gcol1[j][i]))
                egl_ref[d, i % HB, pl.ds(e0[j], 8), :] = jnp.broadcast_to(jnp.exp(glast[j][i]), (8, LANES))
        lm, a_p = [], []
        for j in G:
            bcol_p = jnp.concatenate([jnp.broadcast_to(b, (CHUNK, CHUNK)) for b in bcol1[j]], axis=1)
            diff = jnp.where(is_fwd, dif2[j][:CHUNK], dif2[j][CHUNK:])
            dec = jnp.where(incl_p, jnp.exp(jnp.where(incl_p, diff, 0.0)), 0.0)
            kk_p = jnp.concatenate([gram[j][i % HB][:CHUNK] for i in range(NCH)], axis=1)
            qk_p = jnp.concatenate([gram[j][i % HB][CHUNK:2 * CHUNK] for i in range(NCH)], axis=1)
            lm.append(jnp.where(strict_p, bcol_p * kk_p * dec, 0.0))
            a_p.append(qk_p * dec)
        x = [eye_p - lm[j] for j in G]
        p = [_dot(lm[j].astype(BF16), _block_diag(lm[j], blk_masks)) for j in G]
        for _ in range(4):
            r = [_dot(jnp.concatenate([x[j], p[j]], axis=0).astype(BF16), _block_diag(p[j], blk_masks)) for j in G]
            x = [x[j] + r[j][:CHUNK] for j in G]
            p = [r[j][CHUNK:] for j in G]
        r = [_dot(x[j].astype(BF16), _block_diag(p[j], blk_masks)) for j in G]
        x = [x[j] + r[j] for j in G]
        rhs = [jnp.concatenate(
            [jnp.concatenate([vh[j][i % HB] * bcol1[j][i], kh[j][i % HB] * (bcol1[j][i] * egc1[j][i])], axis=1)
             for i in range(NCH)], axis=0).astype(BF16) for j in G]
        uw = [_dot(_block_diag(x[j], blk_masks), rhs[j]) for j in G]
        ao = [_dot(_block_diag(a_p[j], blk_masks), uw[j].astype(BF16)) for j in G]
        mb = [[_dot(gram[j][i % HB][2 * CHUNK:].astype(BF16),
                    (uw[j][i * CHUNK:(i + 1) * CHUNK] * kdsc[j][i]).astype(BF16)) for i in range(NCH)] for j in G]
        for j in G:
            for hh in range(HB):
                hs = slice(hh * GDN_DIM, (hh + 1) * GDN_DIM)
                o0 = jnp.zeros((CHUNK, GDN_DIM), F32)
                for d in range(2):
                    i = d * HB + hh
                    rs = slice(i * CHUNK, (i + 1) * CHUNK)
                    b_ref[d, hh, pl.ds(m0[j], GDN_DIM), :] = mb[j][i][:, :GDN_DIM].astype(BF16)
                    m_ref[d, hh, pl.ds(m0[j], GDN_DIM), :] = mb[j][i][:, GDN_DIM:].astype(BF16)
                    qt_ref[d, pl.ds(r0[j], CHUNK), hs] = (qh[j][hh] * egc1[j][i] - ao[j][rs, GDN_DIM:]).astype(BF16)
                    o0 = o0 + ao[j][rs, :GDN_DIM]
                o_ref[pl.ds(r0[j], CHUNK), hs] = o0
        return carry

    lax.fori_loop(0, NC // GCH, phase1, 0)

    s_ref[...] = jnp.zeros(s_ref.shape, F32)

    def phase2(step, carry):
        cb = jnp.where(step < NCC, NCC - 1 - step, NC + NCC - 1 - step)
        chains = [(d, hh, c) for d, c in ((0, step), (1, cb)) for hh in range(HB)]
        rows = [pl.ds(pl.multiple_of(c * CHUNK, CHUNK), CHUNK) for _, _, c in chains]
        mrows = [pl.ds(pl.multiple_of(c * GDN_DIM, GDN_DIM), GDN_DIM) for _, _, c in chains]
        hs = [slice(hh * GDN_DIM, (hh + 1) * GDN_DIM) for _, hh, _ in chains]
        n = range(len(chains))
        st = [s_ref[chains[i][0], chains[i][1]] for i in n]
        stb = [st[i].astype(BF16) for i in n]
        ms = [_dot(m_ref[chains[i][0], chains[i][1], mrows[i], :], stb[i]) for i in n]
        oq = [_dot(qt_ref[chains[i][0], rows[i], hs[i]], stb[i]) for i in n]
        for i in n:
            d, hh, c = chains[i]
            egl = egl_ref[d, hh, pl.ds(pl.multiple_of(c * 8, 8), 1), :]
            s_ref[d, hh] = st[i] * egl - ms[i] + b_ref[d, hh, mrows[i], :].astype(F32)
            o_ref[rows[i], hs[i]] += oq[i]
        return carry

    lax.fori_loop(0, NC, phase2, 0)

    nw = nw_ref[...]
    for ti in range(NT):
        rs = slice(ti * TT, (ti + 1) * TT)
        o = o_ref[rs, :]
        zz = _silu(z_ref[0, rs, :])
        for hh in range(HB):
            hs = slice(hh * GDN_DIM, (hh + 1) * GDN_DIM)
            oh = o[:, hs]
            oh = oh * lax.rsqrt(jnp.mean(oh * oh, axis=-1, keepdims=True) + RMS_EPS) * nw
            y_ref[0, rs, hs] = (oh * zz[:, hs]).astype(BF16)


def _gdn(p_main, conv_w, ab, alog_g, dtb_g, norm_w):
    nhb = GDN_WIDTH // HW
    blk = lambda off: pl.BlockSpec((1, T, HW), lambda b, g: (b, 0, off * nhb + g))
    cblk = lambda off: pl.BlockSpec((GDN_CONV, HW), lambda b, g: (0, off * nhb + g))
    return pl.pallas_call(
        _gdn_kernel,
        out_shape=jax.ShapeDtypeStruct((NB, T, GDN_WIDTH), BF16),
        grid=(NB, NG),
        in_specs=[
            blk(0), blk(1), blk(2), blk(3),
            cblk(0), cblk(1), cblk(2),
            pl.BlockSpec((1, T, LANES), lambda b, g: (b, 0, g)),
            pl.BlockSpec((1, 1, LANES), lambda b, g: (g, 0, 0)),
            pl.BlockSpec((1, 1, LANES), lambda b, g: (g, 0, 0)),
            pl.BlockSpec((1, GDN_DIM), lambda b, g: (0, 0)),
        ],
        out_specs=pl.BlockSpec((1, T, HW), lambda b, g: (b, 0, g)),
        scratch_shapes=[
            pltpu.VMEM((PAD_ROWS, HW), F32),
            pltpu.VMEM((T, HW), F32),
            pltpu.VMEM((T, HW), F32),
            pltpu.VMEM((T, HW), F32),
            pltpu.VMEM((2, HB, NC * GDN_DIM, GDN_DIM), BF16),
            pltpu.VMEM((2, HB, NC * GDN_DIM, GDN_DIM), BF16),
            pltpu.VMEM((2, T, HW), BF16),
            pltpu.VMEM((T, HW), F32),
            pltpu.VMEM((2, HB, NC * 8, LANES), F32),
            pltpu.VMEM((2, HB, GDN_DIM, GDN_DIM), F32),
        ],
        compiler_params=_cparams(("arbitrary", "arbitrary")),
        name="gdn",
    )(p_main, p_main, p_main, p_main, conv_w, conv_w, conv_w, ab, alog_g, dtb_g, norm_w)


NEG = -1e30


def _mix_kernel(n_act, ctx_tile0, *refs):
    act_refs = refs[:n_act]
    w_refs = refs[n_act:2 * n_act]
    x_ref, mod_ref, lng_ref, lnb_ref, wr_ref, br_ref, x1_ref, h2_ref, route_ref = refs[2 * n_act:]
    b = pl.program_id(0)
    t = pl.program_id(1)
    row = jnp.where(t == 0, NB, b) if ctx_tile0 else b
    y = _dot(act_refs[0][0], w_refs[0][...])
    for a_ref, w_ref in zip(act_refs[1:], w_refs[1:]):
        y = y + _dot(a_ref[0], w_ref[...])
    x1 = _layer_norm(ALPHA * x_ref[0] + _mod_slice(mod_ref, row, 2) * y, lng_ref[...], lnb_ref[...])
    x1_ref[0] = x1
    h2 = x1 * (1.0 + _mod_slice(mod_ref, row, 4)) + _mod_slice(mod_ref, row, 3)
    h2_ref[...] = h2
    logits = _dot3(h2, wr_ref[...]) + br_ref[...]
    lane = lax.broadcasted_iota(I32, (TT, LANES), 1)
    gl = jnp.where(lane < MOE_GROUPS, logits, NEG)
    gm = jnp.max(gl, axis=-1, keepdims=True)
    gsum = jnp.sum(jnp.exp(gl - gm), axis=-1, keepdims=True)
    g_val = 1.0 / gsum
    g_idx = jnp.min(jnp.where(gl == gm, lane, LANES), axis=-1, keepdims=True)
    sel = (lane >= MOE_GROUPS) & (lane < MOE_GROUPS + MOE_EXPERTS) & (((lane - MOE_GROUPS) // MOE_PER_GROUP) == g_idx)
    el = jnp.where(sel, logits, NEG)
    m1 = jnp.max(el, axis=-1, keepdims=True)
    i1 = jnp.min(jnp.where(el == m1, lane, LANES), axis=-1, keepdims=True)
    el2 = jnp.where(lane == i1, NEG, el)
    m2 = jnp.max(el2, axis=-1, keepdims=True)
    i2 = jnp.min(jnp.where(el2 == m2, lane, LANES), axis=-1, keepdims=True)
    ex2 = jnp.exp(m2 - m1)
    w1 = g_val / (1.0 + ex2)
    w2 = w1 * ex2
    e1 = (i1 - MOE_GROUPS).astype(F32)
    e2 = (i2 - MOE_GROUPS).astype(F32)
    route_ref[...] = jnp.where(lane == 0, e1, jnp.where(lane == 1, e2, jnp.where(lane == 2, w1, jnp.where(lane == 3, w2, 0.0))))


def _mix(acts, ws, xs, mod, ln_g, ln_b, wr, br, ctx_tile0):
    ntl = NT if ctx_tile0 else NTL
    toff = 0 if ctx_tile0 else 1
    n_tok = NB * ntl * TT
    act_specs = [pl.BlockSpec((1, TT, a.shape[2]), lambda b, t: (b, t, 0)) for a in acts]
    w_specs = [pl.BlockSpec(w.shape, lambda b, t: (0, 0)) for w in ws]
    return pl.pallas_call(
        functools.partial(_mix_kernel, len(acts), ctx_tile0),
        out_shape=(
            jax.ShapeDtypeStruct((NB, ntl * TT, D), F32),
            jax.ShapeDtypeStruct((n_tok, D), F32),
            jax.ShapeDtypeStruct((n_tok, LANES), F32),
        ),
        grid=(NB, ntl),
        in_specs=act_specs + w_specs + [
            pl.BlockSpec((1, TT, D), lambda b, t: (b, t + toff, 0)),
            pl.BlockSpec((16, N_MOD * D), lambda b, t: (0, 0)),
            pl.BlockSpec((1, D), lambda b, t: (0, 0)),
            pl.BlockSpec((1, D), lambda b, t: (0, 0)),
            pl.BlockSpec((D, LANES), lambda b, t: (0, 0)),
            pl.BlockSpec((1, LANES), lambda b, t: (0, 0)),
        ],
        out_specs=(
            pl.BlockSpec((1, TT, D), lambda b, t: (b, t, 0)),
            pl.BlockSpec((TT, D), lambda b, t: (b * ntl + t, 0)),
            pl.BlockSpec((TT, LANES), lambda b, t: (b * ntl + t, 0)),
        ),
        compiler_params=_cparams(("arbitrary", "arbitrary")),
        name="mix_ctx" if ctx_tile0 else "mix_lat",
    )(*acts, *ws, xs, mod, ln_g, ln_b, wr, br)


def _plan_kernel(route_ref, rank_ref, cnt_ref, carry_ref):
    i = pl.program_id(0)

    @pl.when(i == 0)
    def _():
        carry_ref[...] = jnp.zeros(carry_ref.shape, F32)

    r = route_ref[...]
    lane = lax.broadcasted_iota(I32, (PB, LANES), 1)
    lanef = lane.astype(F32)
    ri = lax.broadcasted_iota(I32, (PB, PB), 0)
    ci = lax.broadcasted_iota(I32, (PB, PB), 1)
    before = (ci < ri).astype(BF16)
    oh1 = jnp.where(lanef == r[:, 0:1], 1.0, 0.0)
    oh2 = jnp.where(lanef == r[:, 1:2], 1.0, 0.0)
    c0 = carry_ref[0:1, :]
    tot1 = jnp.sum(oh1, axis=0, keepdims=True)
    tot2 = jnp.sum(oh2, axis=0, keepdims=True)
    r1 = _dot(before, oh1.astype(BF16)) + c0
    r2 = _dot(before, oh2.astype(BF16)) + (c0 + tot1)
    rank1 = jnp.sum(r1 * oh1, axis=-1, keepdims=True)
    rank2 = jnp.sum(r2 * oh2, axis=-1, keepdims=True)
    rank_ref[...] = jnp.where(lane == 0, rank1, jnp.where(lane == 1, rank2, 0.0))
    cnew = jnp.broadcast_to(c0 + tot1 + tot2, carry_ref.shape)
    carry_ref[...] = cnew
    cnt_ref[...] = cnew


def _plan(route):
    n_tok = route.shape[0]
    return pl.pallas_call(
        _plan_kernel,
        out_shape=(jax.ShapeDtypeStruct((n_tok, LANES), F32), jax.ShapeDtypeStruct((8, LANES), F32)),
        grid=(n_tok // PB,),
        in_specs=[pl.BlockSpec((PB, LANES), lambda i: (i, 0))],
        out_specs=(pl.BlockSpec((PB, LANES), lambda i: (i, 0)), pl.BlockSpec((8, LANES), lambda i: (0, 0))),
        scratch_shapes=[pltpu.VMEM((8, LANES), F32)],
        compiler_params=_cparams(("arbitrary",)),
        name="moe_plan",
    )(route)


def _invmap_kernel(pos_ref, src_ref):
    n_rows = src_ref.shape[0]
    n_asg = pos_ref.shape[0]

    def clear(i, c):
        src_ref[i] = 0
        return c

    lax.fori_loop(0, n_rows, clear, 0, unroll=8)

    def put(a, c):
        src_ref[pos_ref[a]] = a // 2
        return c

    lax.fori_loop(0, n_asg, put, 0, unroll=8)


def _invmap(pos_flat, n_rows):
    return pl.pallas_call(
        _invmap_kernel,
        out_shape=jax.ShapeDtypeStruct((n_rows,), I32),
        in_specs=[pl.BlockSpec(memory_space=pltpu.SMEM)],
        out_specs=pl.BlockSpec(memory_space=pltpu.SMEM),
        name="moe_invmap",
    )(pos_flat)


def _ffn_kernel(te_ref, nu_ref, src_ref, h_hbm, wg_ref, wu_ref, wd_ref, y_ref, buf, wgb, wub, wdb, sem):
    i = pl.program_id(0)
    nu = nu_ref[0]

    def issue(tile, slot):
        base = tile * TM

        def body(r, c):
            tok = src_ref[base + r]
            pltpu.make_async_copy(h_hbm.at[pl.ds(tok, 1), :], buf.at[slot, pl.ds(r, 1), :], sem.at[slot]).start()
            return c

        lax.fori_loop(0, TM, body, 0, unroll=8)

    @pl.when(i == 0)
    def _():
        issue(0, 0)

    @pl.when(i + 1 < nu)
    def _():
        issue(i + 1, (i + 1) % 2)

    @pl.when(i < nu)
    def _():
        slot = i % 2
        pltpu.make_async_copy(h_hbm.at[pl.ds(0, TM), :], buf.at[slot], sem.at[slot]).wait()
        e = te_ref[i]
        ep = te_ref[jnp.maximum(i - 1, 0)]

        @pl.when((i == 0) | (e != ep))
        def _():
            wgb[...] = wg_ref[0].astype(BF16)
            wub[...] = wu_ref[0].astype(BF16)
            wdb[...] = wd_ref[0].astype(BF16)

        x = buf[slot].astype(BF16)
        a = _dot(x, wgb[...])
        bb = _dot(x, wub[...])
        act = (_silu(a) * bb).astype(BF16)
        y_ref[...] = _dot(act, wdb[...])

    @pl.when(i >= nu)
    def _():
        y_ref[...] = jnp.zeros(y_ref.shape, F32)


def _ffn(tile_expert, n_used, src, h2, wg, wu, wd):
    n_rows = src.shape[0]
    n_tiles = n_rows // TM
    return pl.pallas_call(
        _ffn_kernel,
        out_shape=jax.ShapeDtypeStruct((n_rows, D), F32),
        grid_spec=pltpu.PrefetchScalarGridSpec(
            num_scalar_prefetch=3,
            grid=(n_tiles,),
            in_specs=[
                pl.BlockSpec(memory_space=pl.ANY),
                pl.BlockSpec((1, D, MOE_FF), lambda i, te, nu, src: (te[i], 0, 0)),
                pl.BlockSpec((1, D, MOE_FF), lambda i, te, nu, src: (te[i], 0, 0)),
                pl.BlockSpec((1, MOE_FF, D), lambda i, te, nu, src: (te[i], 0, 0)),
            ],
            out_specs=pl.BlockSpec((TM, D), lambda i, te, nu, src: (i, 0)),
            scratch_shapes=[
                pltpu.VMEM((2, TM, D), F32),
                pltpu.VMEM((D, MOE_FF), BF16),
                pltpu.VMEM((D, MOE_FF), BF16),
                pltpu.VMEM((MOE_FF, D), BF16),
                pltpu.SemaphoreType.DMA((2,)),
            ],
        ),
        compiler_params=_cparams(("arbitrary",)),
        name="moe_ffn",
    )(tile_expert, n_used, src, h2, wg, wu, wd)


def _combine_kernel(ntl, ctx_tile0, pos_ref, y_hbm, x1_ref, route_ref, mod_ref, lng_ref, lnb_ref, o_ref, buf, sem):
    b = pl.program_id(0)
    t = pl.program_id(1)
    flat = b * ntl + t
    n_tiles = NB * ntl

    def issue(tile, slot):
        base = tile * (2 * TT)

        def body(r, c):
            for k in range(2):
                p = pos_ref[base + 2 * r + k]
                pltpu.make_async_copy(y_hbm.at[pl.ds(p, 1), :], buf.at[slot, pl.ds(k * TT + r, 1), :], sem.at[slot]).start()
            return c

        lax.fori_loop(0, TT, body, 0, unroll=4)

    @pl.when(flat == 0)
    def _():
        issue(0, 0)

    @pl.when(flat + 1 < n_tiles)
    def _():
        issue(flat + 1, (flat + 1) % 2)

    slot = flat % 2
    pltpu.make_async_copy(y_hbm.at[pl.ds(0, 2 * TT), :], buf.at[slot], sem.at[slot]).wait()
    row = jnp.where(t == 0, NB, b) if ctx_tile0 else b
    r = route_ref[...]
    f = r[:, 2:3] * buf[slot, 0:TT, :] + r[:, 3:4] * buf[slot, TT:2 * TT, :]
    o_ref[0] = _layer_norm(ALPHA * x1_ref[0] + _mod_slice(mod_ref, row, 5) * f, lng_ref[...], lnb_ref[...])


def _combine(pos_flat, ys, x1, route, mod, ln_g, ln_b, ctx_tile0):
    ntl = NT if ctx_tile0 else NTL
    return pl.pallas_call(
        functools.partial(_combine_kernel, ntl, ctx_tile0),
        out_shape=jax.ShapeDtypeStruct((NB, ntl * TT, D), F32),
        grid_spec=pltpu.PrefetchScalarGridSpec(
            num_scalar_prefetch=1,
            grid=(NB, ntl),
            in_specs=[
                pl.BlockSpec(memory_space=pl.ANY),
                pl.BlockSpec((1, TT, D), lambda b, t, pos: (b, t, 0)),
                pl.BlockSpec((TT, LANES), lambda b, t, pos: (b * ntl + t, 0)),
                pl.BlockSpec((16, N_MOD * D), lambda b, t, pos: (0, 0)),
                pl.BlockSpec((1, D), lambda b, t, pos: (0, 0)),
                pl.BlockSpec((1, D), lambda b, t, pos: (0, 0)),
            ],
            out_specs=pl.BlockSpec((1, TT, D), lambda b, t, pos: (b, t, 0)),
            scratch_shapes=[pltpu.VMEM((2, 2 * TT, D), F32), pltpu.SemaphoreType.DMA((2,))],
        ),
        compiler_params=_cparams(("arbitrary", "arbitrary")),
        name="moe_combine_ctx" if ctx_tile0 else "moe_combine_lat",
    )(pos_flat, ys, x1, route, mod, ln_g, ln_b)


def _moe(h2, route, x1, mod, ln_g, ln_b, wg, wu, wd, ctx_tile0):
    n_tok = h2.shape[0]
    n_tiles = (2 * n_tok) // TM + MOE_EXPERTS
    rank, cnt = _plan(route)
    counts = cnt[0, :MOE_EXPERTS].astype(I32)
    tiles_e = (counts + TM - 1) // TM
    tile_end = jnp.cumsum(tiles_e)
    tile_start = tile_end - tiles_e
    n_used = tile_end[-1]
    tj = jnp.arange(n_tiles, dtype=I32)
    te = jnp.minimum(jnp.sum((tile_end[None, :] <= tj[:, None]).astype(I32), axis=1), MOE_EXPERTS - 1)
    te_last = jnp.max(jnp.where(tj < n_used, te, 0))
    tile_expert = jnp.where(tj < n_used, te, te_last)
    eid = route[:, 0:2].astype(I32)
    pos = tile_start[eid] * TM + rank[:, 0:2].astype(I32)
    pos_flat = pos.reshape(-1)
    src = _invmap(pos_flat, n_tiles * TM)
    ys = _ffn(tile_expert, n_used.reshape(1), src, h2, wg, wu, wd)
    return _combine(pos_flat, ys, x1, route, mod, ln_g, ln_b, ctx_tile0)


W_IN1 = MLA_KV_LORA + MLA_ROPE + MLA_Q_LORA + MLA_ROPE
Q_OFF = MLA_KV_LORA + MLA_ROPE
KRS_OFF = Q_OFF + MLA_Q_LORA
SM_SCALE = MLA_QK ** -0.5


def _rms(v, w):
    return v * lax.rsqrt(jnp.mean(v * v, axis=-1, keepdims=True) + RMS_EPS) * w


def _mla_proj_kernel(x_ref, mod_ref, win_ref, kvn_ref, qnw_ref, wkn_ref, wkv_ref, wqn_ref, wqr_ref, wqrs_ref,
                     cosk_ref, sink_ref, cosq_ref, sinq_ref, q_out, k_out, v_out):
    b = pl.program_id(0)
    t = pl.program_id(1)
    row = jnp.where(t == 0, NB, b)
    h = x_ref[0] * (1.0 + _mod_slice(mod_ref, row, 1)) + _mod_slice(mod_ref, row, 0)
    p = _dot(h.astype(BF16), win_ref[...])
    ckv = _rms(p[:, :MLA_KV_LORA], kvn_ref[...]).astype(BF16)
    kn = _dot(ckv, wkn_ref[...])
    vv = _dot(ckv, wkv_ref[...])
    kr = p[:, MLA_KV_LORA:Q_OFF]
    krs = p[:, KRS_OFF:KRS_OFF + MLA_ROPE]
    kr = jnp.where(t > 0, kr * cosk_ref[...] + krs * sink_ref[...], kr).astype(BF16)
    ql = _rms(p[:, Q_OFF:KRS_OFF], qnw_ref[...]).astype(BF16)
    qn = _dot(ql, wqn_ref[...]) * SM_SCALE
    qr = (_dot(ql, wqr_ref[...]) * cosq_ref[...] + _dot(ql, wqrs_ref[...]) * sinq_ref[...]) * SM_SCALE
    for hd in range(MLA_HEADS):
        k_out[0, hd, :, 0:MLA_NOPE] = kn[:, hd * MLA_NOPE:(hd + 1) * MLA_NOPE].astype(BF16)
        k_out[0, hd, :, MLA_NOPE:MLA_QK] = kr
        v_out[0, hd] = vv[:, hd * MLA_V:(hd + 1) * MLA_V].astype(BF16)
        q_out[0, hd, :, 0:MLA_NOPE] = qn[:, hd * MLA_NOPE:(hd + 1) * MLA_NOPE].astype(BF16)
        q_out[0, hd, :, MLA_NOPE:MLA_QK] = qr[:, hd * MLA_ROPE:(hd + 1) * MLA_ROPE].astype(BF16)


def _mla_proj(xs, mod, win, kvn, qnw, wkn, wkv, wqn, wqr, wqrs, cosk, sink, cosq, sinq):
    full = lambda a: pl.BlockSpec(a.shape, lambda b, t: (0, 0))
    lat = lambda w: pl.BlockSpec((TT, w), lambda b, t: (jnp.maximum(t - 1, 0), 0))
    return pl.pallas_call(
        _mla_proj_kernel,
        out_shape=(
            jax.ShapeDtypeStruct((NB, MLA_HEADS, SEQ, MLA_QK), BF16),
            jax.ShapeDtypeStruct((NB, MLA_HEADS, T, MLA_QK), BF16),
            jax.ShapeDtypeStruct((NB, MLA_HEADS, T, MLA_V), BF16),
        ),
        grid=(NB, NT),
        in_specs=[
            pl.BlockSpec((1, TT, D), lambda b, t: (b, t, 0)),
            pl.BlockSpec((16, N_MOD * D), lambda b, t: (0, 0)),
            full(win), full(kvn), full(qnw), full(wkn), full(wkv), full(wqn), full(wqr), full(wqrs),
            lat(MLA_ROPE), lat(MLA_ROPE), lat(MLA_HEADS * MLA_ROPE), lat(MLA_HEADS * MLA_ROPE),
        ],
        out_specs=(
            pl.BlockSpec((1, MLA_HEADS, TT, MLA_QK), lambda b, t: (b, 0, jnp.maximum(t - 1, 0), 0)),
            pl.BlockSpec((1, MLA_HEADS, TT, MLA_QK), lambda b, t: (b, 0, t, 0)),
            pl.BlockSpec((1, MLA_HEADS, TT, MLA_V), lambda b, t: (b, 0, t, 0)),
        ),
        compiler_params=_cparams(("arbitrary", "arbitrary")),
        name="mla_proj",
    )(xs, mod, win, kvn, qnw, wkn, wkv, wqn, wqr, wqrs, cosk, sink, cosq, sinq)


def _attn_kernel(q_ref, k_ref, v_ref, o_ref):
    s = _dot_nt(q_ref[0, 0], k_ref[0, 0])
    m = jnp.max(s, axis=-1, keepdims=True)
    p = jnp.exp(s - m)
    l = jnp.sum(p, axis=-1, keepdims=True)
    o = _dot(p.astype(BF16), v_ref[0, 0])
    o_ref[0] = (o / l).astype(BF16)


def _attn(q, k, v):
    return pl.pallas_call(
        _attn_kernel,
        out_shape=jax.ShapeDtypeStruct((NB, SEQ, MLA_HEADS * MLA_V), BF16),
        grid=(NB, MLA_HEADS, SEQ // TQ),
        in_specs=[
            pl.BlockSpec((1, 1, TQ, MLA_QK), lambda b, h, i: (b, h, i, 0)),
            pl.BlockSpec((1, 1, T, MLA_QK), lambda b, h, i: (b, h, 0, 0)),
            pl.BlockSpec((1, 1, T, MLA_V), lambda b, h, i: (b, h, 0, 0)),
        ],
        out_specs=pl.BlockSpec((1, TQ, MLA_V), lambda b, h, i: (b, i, h)),
        compiler_params=_cparams(("arbitrary", "arbitrary", "arbitrary")),
        name="mla_attn",
    )(q, k, v)


def _rope_tables():
    n = SEQ
    rowp = (jnp.arange(n) // GRID_W).astype(F32)
    colp = (jnp.arange(n) % GRID_W).astype(F32)
    inv_freq = ROPE_BASE ** (-jnp.arange(ROPE_F, dtype=F32) / ROPE_F)
    ar = rowp[:, None] * inv_freq
    ac = colp[:, None] * inv_freq
    cos = jnp.concatenate([jnp.cos(ar), jnp.cos(ar), jnp.cos(ac), jnp.cos(ac)], axis=1)
    sin = jnp.concatenate([-jnp.sin(ar), jnp.sin(ar), -jnp.sin(ac), jnp.sin(ac)], axis=1)
    return cos, sin


def _rope_swap_perm():
    f = ROPE_F
    return jnp.concatenate([jnp.arange(f, 2 * f), jnp.arange(0, f), jnp.arange(3 * f, 4 * f), jnp.arange(2 * f, 3 * f)])


def _router_params(w_group, b_group, w_expert, b_expert):
    pad = LANES - MOE_GROUPS - MOE_EXPERTS
    wr = jnp.concatenate([w_group, w_expert, jnp.zeros((D, pad), F32)], axis=1)
    br = jnp.concatenate([b_group, b_expert, jnp.zeros((pad,), F32)]).reshape(1, LANES)
    return wr, br


def kernel(x, c, ctx, c_ctx, ada_w, ada_b, ln_g, ln_b, ab_w_in, pool_w, pool_scale, gdn_conv_w, gdn_a_log, gdn_dt_bias, gdn_norm_w, ab_w_out, mla_w_in, mla_kv_norm, mla_w_ukv, mla_q_norm, mla_w_uq, mla_w_out, moe_w_group, moe_b_group, moe_w_expert, moe_b_expert, moe_w_gate, moe_w_up, moe_w_down):
    assert x.shape == (NB, SEQ, D) and ctx.shape == (NB, CTX, D)
    xs = jnp.concatenate([ctx, x], axis=1)
    cv = jnp.concatenate([c, c_ctx[None, :], jnp.zeros((16 - NB - 1, D), F32)], axis=0)
    mod = _ada(cv, ada_w, ada_b)

    w_in = ab_w_in[0]
    w_main = w_in[:, :W_MAIN].astype(BF16)
    o_ab = W_MAIN
    ab_cols = []
    for g in range(NG):
        idx = [o_ab + kind * 2 * GDN_HEADS + d * GDN_HEADS + g * HB + hh
               for kind in range(2) for d in range(2) for hh in range(HB)]
        ab_cols.append(jnp.concatenate([w_in[:, jnp.array(idx)], jnp.zeros((D, LANES - len(idx)), F32)], axis=1))
    w_ab = jnp.concatenate(ab_cols, axis=1)

    def gate_rows(p):
        rows = []
        for g in range(NG):
            vals = jnp.stack([p[d, g * HB + hh] for d in range(2) for hh in range(HB)])
            rows.append(jnp.concatenate([vals, jnp.zeros((LANES - 2 * HB,), F32)]))
        return jnp.stack(rows).reshape(NG, 1, LANES)

    alog_g = gate_rows(gdn_a_log[0])
    dtb_g = gate_rows(gdn_dt_bias[0])
    wbd = jax.scipy.linalg.block_diag(*[pool_w[0, g] for g in range(len(POOL_WINDOWS))]).astype(BF16)

    pool_u, p_main, ab = _inproj0(xs, mod[0], w_main, w_ab)
    pool_y = _pool(pool_u, wbd, pool_scale[0].reshape(1, POOL_WIDTH))
    gdn_y = _gdn(p_main, gdn_conv_w[0], ab, alog_g, dtb_g, gdn_norm_w[0].reshape(1, GDN_DIM))

    w_out0 = ab_w_out[0].astype(BF16)
    wr0, br0 = _router_params(moe_w_group[0], moe_b_group[0], moe_w_expert[0], moe_b_expert[0])
    x1, h2, route = _mix([pool_y, gdn_y], [w_out0[:POOL_WIDTH], w_out0[POOL_WIDTH:]], xs, mod[0],
                         ln_g[0, 0].reshape(1, D), ln_b[0, 0].reshape(1, D), wr0, br0, True)
    xs = _moe(h2, route, x1, mod[0], ln_g[0, 1].reshape(1, D), ln_b[0, 1].reshape(1, D),
              moe_w_gate[0], moe_w_up[0], moe_w_down[0], True)

    perm = _rope_swap_perm()
    w_in1 = mla_w_in[0]
    win = jnp.concatenate([w_in1, w_in1[:, MLA_KV_LORA:Q_OFF][:, perm]], axis=1).astype(BF16)
    ukv = mla_w_ukv[0].reshape(MLA_KV_LORA, MLA_HEADS, MLA_NOPE + MLA_V)
    wkn = ukv[:, :, :MLA_NOPE].reshape(MLA_KV_LORA, MLA_HEADS * MLA_NOPE).astype(BF16)
    wkv = ukv[:, :, MLA_NOPE:].reshape(MLA_KV_LORA, MLA_HEADS * MLA_V).astype(BF16)
    uq = mla_w_uq[0].reshape(MLA_Q_LORA, MLA_HEADS, MLA_QK)
    wqn = uq[:, :, :MLA_NOPE].reshape(MLA_Q_LORA, MLA_HEADS * MLA_NOPE).astype(BF16)
    wqr = uq[:, :, MLA_NOPE:].reshape(MLA_Q_LORA, MLA_HEADS * MLA_ROPE).astype(BF16)
    wqrs = uq[:, :, MLA_NOPE:][:, :, perm].reshape(MLA_Q_LORA, MLA_HEADS * MLA_ROPE).astype(BF16)
    cosk, sink = _rope_tables()
    cosq = jnp.tile(cosk, (1, MLA_HEADS))
    sinq = jnp.tile(sink, (1, MLA_HEADS))

    q, k, v = _mla_proj(xs, mod[1], win, mla_kv_norm[0].reshape(1, MLA_KV_LORA), mla_q_norm[0].reshape(1, MLA_Q_LORA),
                        wkn, wkv, wqn, wqr, wqrs, cosk, sink, cosq, sinq)
    att = _attn(q, k, v)
    wr1, br1 = _router_params(moe_w_group[1], moe_b_group[1], moe_w_expert[1], moe_b_expert[1])
    x1, h2, route = _mix([att], [mla_w_out[0].astype(BF16)], xs, mod[1],
                         ln_g[1, 0].reshape(1, D), ln_b[1, 0].reshape(1, D), wr1, br1, False)
    return _moe(h2, route, x1, mod[1], ln_g[1, 1].reshape(1, D), ln_b[1, 1].reshape(1, D),
                moe_w_gate[1], moe_w_up[1], moe_w_down[1], False)
```

```python
import functools
import math

import jax
import jax.numpy as jnp
from jax import lax
from jax.experimental import pallas as pl
from jax.experimental.pallas import tpu as pltpu

F32 = jnp.float32
BF16 = jnp.bfloat16
I32 = jnp.int32

D = 1024
NB = 8
SEQ = 2048
CTX = 256
T = SEQ + CTX
DEPTH = 2
N_MOD = 6
ALPHA = (2 * DEPTH) ** 0.25
LN_EPS = 1e-5
RMS_EPS = 1e-6

POOL_WINDOWS = (2, 4, 8, 16)
POOL_WIDTH = 256
POOL_GROUP_DIM = 64
GDN_HEADS = 6
GDN_DIM = 128
GDN_WIDTH = GDN_HEADS * GDN_DIM
GDN_CONV = 4
CHUNK = 64

MLA_HEADS = 8
MLA_NOPE = 128
MLA_ROPE = 64
MLA_V = 128
MLA_QK = MLA_NOPE + MLA_ROPE
MLA_Q_LORA = 384
MLA_KV_LORA = 256
GRID_W = 64
ROPE_BASE = 10000.0
ROPE_F = MLA_ROPE // 4

MOE_GROUPS = 4
MOE_PER_GROUP = 8
MOE_EXPERTS = 32
MOE_FF = 512

TT = CTX
NT = T // TT
NTL = SEQ // TT
TM = 256
PB = 512
HB = 2
NG = GDN_HEADS // HB
NC = T // CHUNK
NCC = CTX // CHUNK
TQ = 256
LANES = 128
VMEM_LIMIT = 56 * 1024 * 1024


def _dot(a, b):
    return jnp.dot(a, b, preferred_element_type=F32)


def _dot_nt(a, b):
    return lax.dot_general(a, b, (((1,), (1,)), ((), ())), preferred_element_type=F32)


def _dot_tn(a, b):
    return lax.dot_general(a, b, (((0,), (0,)), ((), ())), preferred_element_type=F32)


def _split2(x):
    hi = x.astype(BF16)
    lo = (x - hi.astype(F32)).astype(BF16)
    return hi, lo


def _dot3(a, b):
    ah, al = _split2(a)
    bh, bl = _split2(b)
    return _dot(ah, bh) + (_dot(ah, bl) + _dot(al, bh))


def _silu(x):
    return x * jax.nn.sigmoid(x)


def _softplus(x):
    return jnp.maximum(x, 0.0) + jnp.log1p(jnp.exp(-jnp.abs(x)))


def _layer_norm(v, g, b):
    mu = jnp.mean(v, axis=-1, keepdims=True)
    c = v - mu
    var = jnp.mean(c * c, axis=-1, keepdims=True)
    return c * lax.rsqrt(var + LN_EPS) * g + b


def _mod_slice(mod_ref, row, k):
    return mod_ref[pl.ds(row, 1), k * D:(k + 1) * D]


NBLK = D // LANES


def _to_token_tiles(y):
    return jnp.transpose(jnp.stack([y[:, s * LANES:(s + 1) * LANES] for s in range(NBLK)], axis=0), (1, 0, 2))


def _from_token_tiles(x3):
    xt = jnp.transpose(x3, (1, 0, 2))
    return jnp.concatenate([xt[s] for s in range(NBLK)], axis=1)


def _cparams(sem, vmem=VMEM_LIMIT):
    return pltpu.CompilerParams(dimension_semantics=sem, vmem_limit_bytes=vmem)


def _ada_kernel(cv_ref, w_ref, b_ref, o_ref):
    s = _silu(cv_ref[...])
    o_ref[0] = _dot3(s, w_ref[0]) + b_ref[0]


def _ada(cv, ada_w, ada_b):
    nblk = N_MOD
    return pl.pallas_call(
        _ada_kernel,
        out_shape=jax.ShapeDtypeStruct((DEPTH, 16, N_MOD * D), F32),
        grid=(DEPTH, nblk),
        in_specs=[
            pl.BlockSpec((16, D), lambda l, j: (0, 0)),
            pl.BlockSpec((1, D, D), lambda l, j: (l, 0, j)),
            pl.BlockSpec((1, 1, D), lambda l, j: (l, 0, j)),
        ],
        out_specs=pl.BlockSpec((1, 16, D), lambda l, j: (l, 0, j)),
        compiler_params=_cparams(("arbitrary", "arbitrary")),
        name="ada_mod",
    )(cv, ada_w, ada_b.reshape(DEPTH, 1, N_MOD * D))


W_MAIN = POOL_WIDTH + 4 * GDN_WIDTH
W_AB = NG * LANES


def _inproj0_kernel(x_ref, mod_ref, w_ref, wab_ref, pool_ref, main_ref, ab_ref):
    b = pl.program_id(0)
    t = pl.program_id(1)
    row = jnp.where(t == 0, NB, b)
    h = x_ref[0] * (1.0 + _mod_slice(mod_ref, row, 1)) + _mod_slice(mod_ref, row, 0)
    p = _dot(h.astype(BF16), w_ref[...])
    pool_ref[0] = p[:, :POOL_WIDTH]
    main_ref[0] = p[:, POOL_WIDTH:]
    ab_ref[0] = _dot3(h, wab_ref[...])


def _inproj0(xs, mod, w_main, w_ab):
    return pl.pallas_call(
        _inproj0_kernel,
        out_shape=(
            jax.ShapeDtypeStruct((NB, T, POOL_WIDTH), F32),
            jax.ShapeDtypeStruct((NB, T, 4 * GDN_WIDTH), F32),
            jax.ShapeDtypeStruct((NB, T, W_AB), F32),
        ),
        grid=(NB, NT),
        in_specs=[
            pl.BlockSpec((1, TT, D), lambda b, t: (b, t, 0)),
            pl.BlockSpec((16, N_MOD * D), lambda b, t: (0, 0)),
            pl.BlockSpec((D, W_MAIN), lambda b, t: (0, 0)),
            pl.BlockSpec((D, W_AB), lambda b, t: (0, 0)),
        ],
        out_specs=(
            pl.BlockSpec((1, TT, POOL_WIDTH), lambda b, t: (b, t, 0)),
            pl.BlockSpec((1, TT, 4 * GDN_WIDTH), lambda b, t: (b, t, 0)),
            pl.BlockSpec((1, TT, W_AB), lambda b, t: (b, t, 0)),
        ),
        compiler_params=_cparams(("arbitrary", "arbitrary")),
        name="inproj0",
    )(xs, mod, w_main, w_ab)


PAD_GAP = 16
PAD_CTX = PAD_GAP
PAD_LAT = PAD_CTX + CTX + 2 * PAD_GAP
PAD_ROWS = PAD_LAT + SEQ + PAD_GAP


def _fill_padded(pad_ref, src):
    w = pad_ref.shape[1]
    pad_ref[0:PAD_CTX, :] = jnp.zeros((PAD_CTX, w), F32)
    pad_ref[PAD_CTX + CTX:PAD_LAT, :] = jnp.zeros((2 * PAD_GAP, w), F32)
    pad_ref[PAD_LAT + SEQ:PAD_ROWS, :] = jnp.zeros((PAD_GAP, w), F32)
    pad_ref[PAD_CTX:PAD_CTX + CTX, :] = src(0, CTX)
    pad_ref[PAD_LAT:PAD_LAT + SEQ, :] = src(CTX, SEQ)


def _tile_pad_row(ti):
    return PAD_CTX if ti == 0 else PAD_LAT + (ti - 1) * TT


def _pool_kernel(u_ref, wbd_ref, scale_ref, o_ref, pad_ref):
    _fill_padded(pad_ref, lambda s, n: u_ref[0, s:s + n, :])
    lane = lax.broadcasted_iota(I32, (1, POOL_WIDTH), 1)
    grp = lane // POOL_GROUP_DIM
    win = jnp.zeros((1, POOL_WIDTH), I32)
    for g, w in enumerate(POOL_WINDOWS):
        win = jnp.where(grp == g, w, win)
    left = win // 2
    right = win - 1 - left
    for ti in range(NT):
        seg_len = CTX if ti == 0 else SEQ
        seg_t0 = 0 if ti == 0 else (ti - 1) * TT
        prow = _tile_pad_row(ti)
        tpos = seg_t0 + lax.broadcasted_iota(I32, (TT, 1), 0)
        acc = jnp.zeros((TT, POOL_WIDTH), F32)
        for j in range(-max(POOL_WINDOWS) // 2, max(POOL_WINDOWS) // 2):
            inwin = (j >= -left) & (j <= right)
            acc = acc + jnp.where(inwin, pad_ref[pl.ds(prow + j, TT), :], 0.0)
        cnt = jnp.minimum(tpos + right + 1, seg_len) - jnp.maximum(tpos - left, 0)
        dlt = acc / cnt.astype(F32) - pad_ref[pl.ds(prow, TT), :]
        y = _dot(dlt.astype(BF16), wbd_ref[...]) * scale_ref[...]
        o_ref[0, ti * TT:(ti + 1) * TT, :] = y.astype(BF16)


def _pool(pool_u, wbd, scale):
    return pl.pallas_call(
        _pool_kernel,
        out_shape=jax.ShapeDtypeStruct((NB, T, POOL_WIDTH), BF16),
        grid=(NB,),
        in_specs=[
            pl.BlockSpec((1, T, POOL_WIDTH), lambda b: (b, 0, 0)),
            pl.BlockSpec((POOL_WIDTH, POOL_WIDTH), lambda b: (0, 0)),
            pl.BlockSpec((1, POOL_WIDTH), lambda b: (0, 0)),
        ],
        out_specs=pl.BlockSpec((1, T, POOL_WIDTH), lambda b: (b, 0, 0)),
        scratch_shapes=[pltpu.VMEM((PAD_ROWS, POOL_WIDTH), F32)],
        compiler_params=_cparams(("arbitrary",)),
        name="pool",
    )(pool_u, wbd, scale)


HW = HB * GDN_DIM


NCH = 2 * HB
PW = NCH * CHUNK
GCH = 4


def _block_diag(xp, blk_masks):
    return jnp.concatenate([jnp.where(m, xp, 0.0) for m in blk_masks], axis=0).astype(BF16)


def _gdn_kernel(q_ref, k_ref, v_ref, z_ref, cwq_ref, cwk_ref, cwv_ref, ab_ref, alog_ref, dtb_ref, nw_ref,
                y_ref, pad_ref, qn_ref, kn_ref, vv_ref, m_ref, b_ref, qt_ref, o_ref, egl_ref, s_ref):
    def conv(x_ref, cw_ref, dst_ref, l2, scale):
        _fill_padded(pad_ref, lambda s, n: x_ref[0, s:s + n, :])
        cw = cw_ref[...]
        for ti in range(NT):
            prow = _tile_pad_row(ti)
            acc = jnp.zeros((TT, HW), F32)
            for j in range(GDN_CONV):
                acc = acc + pad_ref[pl.ds(prow - 2 + j, TT), :] * cw[j:j + 1, :]
            y = _silu(acc)
            if l2:
                for hh in range(HB):
                    yh = y[:, hh * GDN_DIM:(hh + 1) * GDN_DIM]
                    yh = yh * lax.rsqrt(jnp.sum(yh * yh, axis=-1, keepdims=True) + RMS_EPS)
                    dst_ref[ti * TT:(ti + 1) * TT, hh * GDN_DIM:(hh + 1) * GDN_DIM] = yh * scale
            else:
                dst_ref[ti * TT:(ti + 1) * TT, :] = y

    conv(q_ref, cwq_ref, qn_ref, True, GDN_DIM ** -0.5)
    conv(k_ref, cwk_ref, kn_ref, True, 1.0)
    conv(v_ref, cwv_ref, vv_ref, False, 1.0)

    ri = lax.broadcasted_iota(I32, (CHUNK, CHUNK), 0)
    ci = lax.broadcasted_iota(I32, (CHUNK, CHUNK), 1)
    tri2 = jnp.concatenate([(ci <= ri).astype(BF16), (ci >= ri).astype(BF16)], axis=0)
    rowp = lax.broadcasted_iota(I32, (CHUNK, PW), 0)
    lanep = lax.broadcasted_iota(I32, (CHUNK, PW), 1)
    blk = lanep // CHUNK
    colp = lanep - blk * CHUNK
    is_fwd = blk < HB
    ahead = jnp.where(is_fwd, rowp - colp, colp - rowp)
    incl_p = ahead >= 0
    strict_p = ahead > 0
    eye_p = (colp == rowp).astype(F32)
    eye_d = (lax.broadcasted_iota(I32, (GDN_DIM, GDN_DIM), 0)
             == lax.broadcasted_iota(I32, (GDN_DIM, GDN_DIM), 1)).astype(BF16)
    blk_masks = [blk == i for i in range(NCH)]
    alog = alog_ref[0]
    dtb = dtb_ref[0]

    def split3(v):
        h1 = v.astype(BF16)
        rem = v - h1.astype(F32)
        h2 = rem.astype(BF16)
        return h1, h2, (rem - h2.astype(F32)).astype(BF16)

    def phase1(it, carry):
        cs = [it * GCH + j for j in range(GCH)]
        r0 = [pl.multiple_of(c * CHUNK, CHUNK) for c in cs]
        m0 = [pl.multiple_of(c * GDN_DIM, GDN_DIM) for c in cs]
        e0 = [pl.multiple_of(c * 8, 8) for c in cs]
        G = range(GCH)
        abt = [ab_ref[0, pl.ds(r0[j], CHUNK), :] for j in G]
        g_all = [-jnp.exp(alog) * _softplus(abt[j] + dtb) for j in G]
        beta_all = [jax.nn.sigmoid(abt[j]) for j in G]
        g3 = [split3(g_all[j]) for j in G]
        gcs = [_dot(tri2, g3[j][0]) + (_dot(tri2, g3[j][1]) + _dot(tri2, g3[j][2])) for j in G]
        gb3 = [split3(jnp.where(strict_p, jnp.concatenate(
            [jnp.broadcast_to(g_all[j][:, i:i + 1], (CHUNK, CHUNK)) for i in range(NCH)], axis=1), 0.0)) for j in G]
        dif2 = [_dot(tri2, gb3[j][0]) + (_dot(tri2, gb3[j][1]) + _dot(tri2, gb3[j][2])) for j in G]
        kh = [[kn_ref[pl.ds(r0[j], CHUNK), hh * GDN_DIM:(hh + 1) * GDN_DIM] for hh in range(HB)] for j in G]
        qh = [[qn_ref[pl.ds(r0[j], CHUNK), hh * GDN_DIM:(hh + 1) * GDN_DIM] for hh in range(HB)] for j in G]
        vh = [[vv_ref[pl.ds(r0[j], CHUNK), hh * GDN_DIM:(hh + 1) * GDN_DIM] for hh in range(HB)] for j in G]
        gram = [[_dot_nt(jnp.concatenate([kh[j][hh].astype(BF16), qh[j][hh].astype(BF16), eye_d], axis=0),
                         kh[j][hh].astype(BF16)) for hh in range(HB)] for j in G]
        gcol1, bcol1, glast, egc1, kdsc = [], [], [], [], []
        for j in G:
            gcol1.append([]); bcol1.append([]); glast.append([]); egc1.append([]); kdsc.append([])
            for i in range(NCH):
                d = i // HB
                gc = gcs[j][d * CHUNK:(d + 1) * CHUNK]
                gcol1[j].append(gc[:, i:i + 1])
                bcol1[j].append(beta_all[j][:, NCH + i:NCH + i + 1])
                glast[j].append(gc[CHUNK - 1:CHUNK, i:i + 1] if d == 0 else gc[0:1, i:i + 1])
                egc1[j].append(jnp.exp(gcol1[j][i]))
                kdsc[j].append(jnp.exp(glast[j][i] - gcol1[j][i]))
                egl_ref[d, i % HB, pl.ds(e0[j], 8), :] = jnp.broadcast_to(jnp.exp(glast[j][i]), (8, LANES))
        lm, a_p = [], []
        for j in G:
            bcol_p = jnp.concatenate([jnp.broadcast_to(b, (CHUNK, CHUNK)) for b in bcol1[j]], axis=1)
            diff = jnp.where(is_fwd, dif2[j][:CHUNK], dif2[j][CHUNK:])
            dec = jnp.where(incl_p, jnp.exp(jnp.where(incl_p, diff, 0.0)), 0.0)
            kk_p = jnp.concatenate([gram[j][i % HB][:CHUNK] for i in range(NCH)], axis=1)
            qk_p = jnp.concatenate([gram[j][i % HB][CHUNK:2 * CHUNK] for i in range(NCH)], axis=1)
            lm.append(jnp.where(strict_p, bcol_p * kk_p * dec, 0.0))
            a_p.append(qk_p * dec)
        x = [eye_p - lm[j] for j in G]
        p = [_dot(lm[j].astype(BF16), _block_diag(lm[j], blk_masks)) for j in G]
        for _ in range(4):
            r = [_dot(jnp.concatenate([x[j], p[j]], axis=0).astype(BF16), _block_diag(p[j], blk_masks)) for j in G]
            x = [x[j] + r[j][:CHUNK] for j in G]
            p = [r[j][CHUNK:] for j in G]
        r = [_dot(x[j].astype(BF16), _block_diag(p[j], blk_masks)) for j in G]
        x = [x[j] + r[j] for j in G]
        rhs = [jnp.concatenate(
            [jnp.concatenate([vh[j][i % HB] * bcol1[j][i], kh[j][i % HB] * (bcol1[j][i] * egc1[j][i])], axis=1)
             for i in range(NCH)], axis=0).astype(BF16) for j in G]
        uw = [_dot(_block_diag(x[j], blk_masks), rhs[j]) for j in G]
        ao = [_dot(_block_diag(a_p[j], blk_masks), uw[j].astype(BF16)) for j in G]
        mb = [[_dot(gram[j][i % HB][2 * CHUNK:].astype(BF16),
                    (uw[j][i * CHUNK:(i + 1) * CHUNK] * kdsc[j][i]).astype(BF16)) for i in range(NCH)] for j in G]
        for j in G:
            for hh in range(HB):
                hs = slice(hh * GDN_DIM, (hh + 1) * GDN_DIM)
                o0 = jnp.zeros((CHUNK, GDN_DIM), F32)
                for d in range(2):
                    i = d * HB + hh
                    rs = slice(i * CHUNK, (i + 1) * CHUNK)
                    b_ref[d, hh, pl.ds(m0[j], GDN_DIM), :] = mb[j][i][:, :GDN_DIM].astype(BF16)
                    m_ref[d, hh, pl.ds(m0[j], GDN_DIM), :] = mb[j][i][:, GDN_DIM:].astype(BF16)
                    qt_ref[d, pl.ds(r0[j], CHUNK), hs] = (qh[j][hh] * egc1[j][i] - ao[j][rs, GDN_DIM:]).astype(BF16)
                    o0 = o0 + ao[j][rs, :GDN_DIM]
                o_ref[pl.ds(r0[j], CHUNK), hs] = o0
        return carry

    lax.fori_loop(0, NC // GCH, phase1, 0)

    s_ref[...] = jnp.zeros(s_ref.shape, F32)

    def phase2(step, carry):
        cb = jnp.where(step < NCC, NCC - 1 - step, NC + NCC - 1 - step)
        chains = [(d, hh, c) for d, c in ((0, step), (1, cb)) for hh in range(HB)]
        rows = [pl.ds(pl.multiple_of(c * CHUNK, CHUNK), CHUNK) for _, _, c in chains]
        mrows = [pl.ds(pl.multiple_of(c * GDN_DIM, GDN_DIM), GDN_DIM) for _, _, c in chains]
        hs = [slice(hh * GDN_DIM, (hh + 1) * GDN_DIM) for _, hh, _ in chains]
        n = range(len(chains))
        st = [s_ref[chains[i][0], chains[i][1]] for i in n]
        stb = [st[i].astype(BF16) for i in n]
        ms = [_dot(m_ref[chains[i][0], chains[i][1], mrows[i], :], stb[i]) for i in n]
        oq = [_dot(qt_ref[chains[i][0], rows[i], hs[i]], stb[i]) for i in n]
        for i in n:
            d, hh, c = chains[i]
            egl = egl_ref[d, hh, pl.ds(pl.multiple_of(c * 8, 8), 1), :]
            s_ref[d, hh] = st[i] * egl - ms[i] + b_ref[d, hh, mrows[i], :].astype(F32)
            o_ref[rows[i], hs[i]] += oq[i]
        return carry

    lax.fori_loop(0, NC, phase2, 0)

    nw = nw_ref[...]
    for ti in range(NT):
        rs = slice(ti * TT, (ti + 1) * TT)
        o = o_ref[rs, :]
        zz = _silu(z_ref[0, rs, :])
        for hh in range(HB):
            hs = slice(hh * GDN_DIM, (hh + 1) * GDN_DIM)
            oh = o[:, hs]
            oh = oh * lax.rsqrt(jnp.mean(oh * oh, axis=-1, keepdims=True) + RMS_EPS) * nw
            y_ref[0, rs, hs] = (oh * zz[:, hs]).astype(BF16)


def _gdn(p_main, conv_w, ab, alog_g, dtb_g, norm_w):
    nhb = GDN_WIDTH // HW
    blk = lambda off: pl.BlockSpec((1, T, HW), lambda b, g: (b, 0, off * nhb + g))
    cblk = lambda off: pl.BlockSpec((GDN_CONV, HW), lambda b, g: (0, off * nhb + g))
    return pl.pallas_call(
        _gdn_kernel,
        out_shape=jax.ShapeDtypeStruct((NB, T, GDN_WIDTH), BF16),
        grid=(NB, NG),
        in_specs=[
            blk(0), blk(1), blk(2), blk(3),
            cblk(0), cblk(1), cblk(2),
            pl.BlockSpec((1, T, LANES), lambda b, g: (b, 0, g)),
            pl.BlockSpec((1, 1, LANES), lambda b, g: (g, 0, 0)),
            pl.BlockSpec((1, 1, LANES), lambda b, g: (g, 0, 0)),
            pl.BlockSpec((1, GDN_DIM), lambda b, g: (0, 0)),
        ],
        out_specs=pl.BlockSpec((1, T, HW), lambda b, g: (b, 0, g)),
        scratch_shapes=[
            pltpu.VMEM((PAD_ROWS, HW), F32),
            pltpu.VMEM((T, HW), F32),
            pltpu.VMEM((T, HW), F32),
            pltpu.VMEM((T, HW), F32),
            pltpu.VMEM((2, HB, NC * GDN_DIM, GDN_DIM), BF16),
            pltpu.VMEM((2, HB, NC * GDN_DIM, GDN_DIM), BF16),
            pltpu.VMEM((2, T, HW), BF16),
            pltpu.VMEM((T, HW), F32),
            pltpu.VMEM((2, HB, NC * 8, LANES), F32),
            pltpu.VMEM((2, HB, GDN_DIM, GDN_DIM), F32),
        ],
        compiler_params=_cparams(("arbitrary", "arbitrary")),
        name="gdn",
    )(p_main, p_main, p_main, p_main, conv_w, conv_w, conv_w, ab, alog_g, dtb_g, norm_w)


NEG = -1e30


def _mix_kernel(n_act, ctx_tile0, *refs):
    act_refs = refs[:n_act]
    w_refs = refs[n_act:2 * n_act]
    x_ref, mod_ref, lng_ref, lnb_ref, wr_ref, br_ref, x1_ref, h2_ref, route_ref = refs[2 * n_act:]
    b = pl.program_id(0)
    t = pl.program_id(1)
    row = jnp.where(t == 0, NB, b) if ctx_tile0 else b
    y = _dot(act_refs[0][0], w_refs[0][...])
    for a_ref, w_ref in zip(act_refs[1:], w_refs[1:]):
        y = y + _dot(a_ref[0], w_ref[...])
    x1 = _layer_norm(ALPHA * x_ref[0] + _mod_slice(mod_ref, row, 2) * y, lng_ref[...], lnb_ref[...])
    x1_ref[0] = x1
    h2 = x1 * (1.0 + _mod_slice(mod_ref, row, 4)) + _mod_slice(mod_ref, row, 3)
    h2_ref[...] = _to_token_tiles(h2)
    logits = _dot3(h2, wr_ref[...]) + br_ref[...]
    lane = lax.broadcasted_iota(I32, (TT, LANES), 1)
    gl = jnp.where(lane < MOE_GROUPS, logits, NEG)
    gm = jnp.max(gl, axis=-1, keepdims=True)
    gsum = jnp.sum(jnp.exp(gl - gm), axis=-1, keepdims=True)
    g_val = 1.0 / gsum
    g_idx = jnp.min(jnp.where(gl == gm, lane, LANES), axis=-1, keepdims=True)
    sel = (lane >= MOE_GROUPS) & (lane < MOE_GROUPS + MOE_EXPERTS) & (((lane - MOE_GROUPS) // MOE_PER_GROUP) == g_idx)
    el = jnp.where(sel, logits, NEG)
    m1 = jnp.max(el, axis=-1, keepdims=True)
    i1 = jnp.min(jnp.where(el == m1, lane, LANES), axis=-1, keepdims=True)
    el2 = jnp.where(lane == i1, NEG, el)
    m2 = jnp.max(el2, axis=-1, keepdims=True)
    i2 = jnp.min(jnp.where(el2 == m2, lane, LANES), axis=-1, keepdims=True)
    ex2 = jnp.exp(m2 - m1)
    w1 = g_val / (1.0 + ex2)
    w2 = w1 * ex2
    e1 = (i1 - MOE_GROUPS).astype(F32)
    e2 = (i2 - MOE_GROUPS).astype(F32)
    route_ref[...] = jnp.where(lane == 0, e1, jnp.where(lane == 1, e2, jnp.where(lane == 2, w1, jnp.where(lane == 3, w2, 0.0))))


def _mix(acts, ws, xs, mod, ln_g, ln_b, wr, br, ctx_tile0):
    ntl = NT if ctx_tile0 else NTL
    toff = 0 if ctx_tile0 else 1
    n_tok = NB * ntl * TT
    act_specs = [pl.BlockSpec((1, TT, a.shape[2]), lambda b, t: (b, t, 0)) for a in acts]
    w_specs = [pl.BlockSpec(w.shape, lambda b, t: (0, 0)) for w in ws]
    return pl.pallas_call(
        functools.partial(_mix_kernel, len(acts), ctx_tile0),
        out_shape=(
            jax.ShapeDtypeStruct((NB, ntl * TT, D), F32),
            jax.ShapeDtypeStruct((n_tok, NBLK, LANES), F32),
            jax.ShapeDtypeStruct((n_tok, LANES), F32),
        ),
        grid=(NB, ntl),
        in_specs=act_specs + w_specs + [
            pl.BlockSpec((1, TT, D), lambda b, t: (b, t + toff, 0)),
            pl.BlockSpec((16, N_MOD * D), lambda b, t: (0, 0)),
            pl.BlockSpec((1, D), lambda b, t: (0, 0)),
            pl.BlockSpec((1, D), lambda b, t: (0, 0)),
            pl.BlockSpec((D, LANES), lambda b, t: (0, 0)),
            pl.BlockSpec((1, LANES), lambda b, t: (0, 0)),
        ],
        out_specs=(
            pl.BlockSpec((1, TT, D), lambda b, t: (b, t, 0)),
            pl.BlockSpec((TT, NBLK, LANES), lambda b, t: (b * ntl + t, 0, 0)),
            pl.BlockSpec((TT, LANES), lambda b, t: (b * ntl + t, 0)),
        ),
        compiler_params=_cparams(("arbitrary", "arbitrary")),
        name="mix_ctx" if ctx_tile0 else "mix_lat",
    )(*acts, *ws, xs, mod, ln_g, ln_b, wr, br)


def _plan_kernel(route_ref, rank_ref, cnt_ref, carry_ref):
    i = pl.program_id(0)

    @pl.when(i == 0)
    def _():
        carry_ref[...] = jnp.zeros(carry_ref.shape, F32)

    r = route_ref[...]
    lane = lax.broadcasted_iota(I32, (PB, LANES), 1)
    lanef = lane.astype(F32)
    ri = lax.broadcasted_iota(I32, (PB, PB), 0)
    ci = lax.broadcasted_iota(I32, (PB, PB), 1)
    before = (ci < ri).astype(BF16)
    oh1 = jnp.where(lanef == r[:, 0:1], 1.0, 0.0)
    oh2 = jnp.where(lanef == r[:, 1:2], 1.0, 0.0)
    c0 = carry_ref[0:1, :]
    tot1 = jnp.sum(oh1, axis=0, keepdims=True)
    tot2 = jnp.sum(oh2, axis=0, keepdims=True)
    r1 = _dot(before, oh1.astype(BF16)) + c0
    r2 = _dot(before, oh2.astype(BF16)) + (c0 + tot1)
    rank1 = jnp.sum(r1 * oh1, axis=-1, keepdims=True)
    rank2 = jnp.sum(r2 * oh2, axis=-1, keepdims=True)
    rank_ref[...] = jnp.where(lane == 0, rank1, jnp.where(lane == 1, rank2, 0.0))
    cnew = jnp.broadcast_to(c0 + tot1 + tot2, carry_ref.shape)
    carry_ref[...] = cnew
    cnt_ref[...] = cnew


def _plan(route):
    n_tok = route.shape[0]
    return pl.pallas_call(
        _plan_kernel,
        out_shape=(jax.ShapeDtypeStruct((n_tok, LANES), F32), jax.ShapeDtypeStruct((8, LANES), F32)),
        grid=(n_tok // PB,),
        in_specs=[pl.BlockSpec((PB, LANES), lambda i: (i, 0))],
        out_specs=(pl.BlockSpec((PB, LANES), lambda i: (i, 0)), pl.BlockSpec((8, LANES), lambda i: (0, 0))),
        scratch_shapes=[pltpu.VMEM((8, LANES), F32)],
        compiler_params=_cparams(("arbitrary",)),
        name="moe_plan",
    )(route)


INV_UNROLL = 16


def _invmap_kernel(pos_ref, src_ref):
    n_rows = src_ref.shape[0]
    n_tok = pos_ref.shape[0] // 2

    def clear(g, c):
        for u in range(INV_UNROLL):
            src_ref[g * INV_UNROLL + u] = 0
        return c

    lax.fori_loop(0, n_rows // INV_UNROLL, clear, 0)

    def put(g, c):
        for u in range(INV_UNROLL):
            t = g * INV_UNROLL + u
            src_ref[pos_ref[2 * t]] = t
            src_ref[pos_ref[2 * t + 1]] = t
        return c

    lax.fori_loop(0, n_tok // INV_UNROLL, put, 0)


def _invmap(pos_flat, n_rows):
    return pl.pallas_call(
        _invmap_kernel,
        out_shape=jax.ShapeDtypeStruct((n_rows,), I32),
        in_specs=[pl.BlockSpec(memory_space=pltpu.SMEM)],
        out_specs=pl.BlockSpec(memory_space=pltpu.SMEM),
        name="moe_invmap",
    )(pos_flat)


def _ffn_kernel(te_ref, nu_ref, src_ref, h_hbm, wg_ref, wu_ref, wd_ref, y_ref, buf, wgb, wub, wdb, sem):
    i = pl.program_id(0)
    nu = nu_ref[0]

    def issue(tile, slot):
        base = tile * TM
        for r in range(TM):
            pltpu.make_async_copy(h_hbm.at[src_ref[base + r]], buf.at[slot, r], sem.at[slot]).start()

    @pl.when(i == 0)
    def _():
        issue(0, 0)

    @pl.when(i < nu)
    def _():
        slot = i % 2
        pltpu.make_async_copy(h_hbm.at[pl.ds(0, TM)], buf.at[slot], sem.at[slot]).wait()

        @pl.when(i + 1 < nu)
        def _():
            issue(i + 1, 1 - slot)

        e = te_ref[i]
        ep = te_ref[jnp.maximum(i - 1, 0)]

        @pl.when((i == 0) | (e != ep))
        def _():
            wgb[...] = wg_ref[0, 0].astype(BF16)
            wub[...] = wu_ref[0, 0].astype(BF16)
            wdb[...] = wd_ref[0, 0].astype(BF16)

        x = _from_token_tiles(buf[slot]).astype(BF16)
        a = _dot(x, wgb[...])
        bb = _dot(x, wub[...])
        act = (_silu(a) * bb).astype(BF16)
        y_ref[...] = _to_token_tiles(_dot(act, wdb[...]))

    @pl.when(i >= nu)
    def _():
        y_ref[...] = jnp.zeros(y_ref.shape, F32)


def _ffn(tile_expert, n_used, src, h2, wg, wu, wd, layer):
    n_rows = src.shape[0]
    n_tiles = n_rows // TM
    wspec = lambda shape: pl.BlockSpec((1, 1) + shape, lambda i, te, nu, src: (layer, te[i], 0, 0))
    return pl.pallas_call(
        _ffn_kernel,
        out_shape=jax.ShapeDtypeStruct((n_rows, NBLK, LANES), F32),
        grid_spec=pltpu.PrefetchScalarGridSpec(
            num_scalar_prefetch=3,
            grid=(n_tiles,),
            in_specs=[
                pl.BlockSpec(memory_space=pl.ANY),
                wspec((D, MOE_FF)), wspec((D, MOE_FF)), wspec((MOE_FF, D)),
            ],
            out_specs=pl.BlockSpec((TM, NBLK, LANES), lambda i, te, nu, src: (i, 0, 0)),
            scratch_shapes=[
                pltpu.VMEM((2, TM, NBLK, LANES), F32),
                pltpu.VMEM((D, MOE_FF), BF16),
                pltpu.VMEM((D, MOE_FF), BF16),
                pltpu.VMEM((MOE_FF, D), BF16),
                pltpu.SemaphoreType.DMA((2,)),
            ],
        ),
        compiler_params=_cparams(("arbitrary",)),
        name="moe_ffn",
    )(tile_expert, n_used, src, h2, wg, wu, wd)


def _combine_kernel(ntl, ctx_tile0, pos_ref, y_hbm, x1_ref, route_ref, mod_ref, lng_ref, lnb_ref, o_ref, buf, sem):
    b = pl.program_id(0)
    t = pl.program_id(1)
    flat = b * ntl + t
    n_tiles = NB * ntl

    def issue(tile, slot):
        base = tile * (2 * TT)
        for r in range(TT):
            for k in range(2):
                pltpu.make_async_copy(y_hbm.at[pos_ref[base + 2 * r + k]], buf.at[slot, k * TT + r], sem.at[slot]).start()

    @pl.when(flat == 0)
    def _():
        issue(0, 0)

    slot = flat % 2
    pltpu.make_async_copy(y_hbm.at[pl.ds(0, 2 * TT)], buf.at[slot], sem.at[slot]).wait()

    @pl.when(flat + 1 < n_tiles)
    def _():
        issue(flat + 1, 1 - slot)

    row = jnp.where(t == 0, NB, b) if ctx_tile0 else b
    r = route_ref[...]
    f = (r[:, 2:3] * _from_token_tiles(buf[slot, 0:TT]) + r[:, 3:4] * _from_token_tiles(buf[slot, TT:2 * TT]))
    o_ref[0] = _layer_norm(ALPHA * x1_ref[0] + _mod_slice(mod_ref, row, 5) * f, lng_ref[...], lnb_ref[...])


def _combine(pos_flat, ys, x1, route, mod, ln_g, ln_b, ctx_tile0):
    ntl = NT if ctx_tile0 else NTL
    return pl.pallas_call(
        functools.partial(_combine_kernel, ntl, ctx_tile0),
        out_shape=jax.ShapeDtypeStruct((NB, ntl * TT, D), F32),
        grid_spec=pltpu.PrefetchScalarGridSpec(
            num_scalar_prefetch=1,
            grid=(NB, ntl),
            in_specs=[
                pl.BlockSpec(memory_space=pl.ANY),
                pl.BlockSpec((1, TT, D), lambda b, t, pos: (b, t, 0)),
                pl.BlockSpec((TT, LANES), lambda b, t, pos: (b * ntl + t, 0)),
                pl.BlockSpec((16, N_MOD * D), lambda b, t, pos: (0, 0)),
                pl.BlockSpec((1, D), lambda b, t, pos: (0, 0)),
                pl.BlockSpec((1, D), lambda b, t, pos: (0, 0)),
            ],
            out_specs=pl.BlockSpec((1, TT, D), lambda b, t, pos: (b, t, 0)),
            scratch_shapes=[pltpu.VMEM((2, 2 * TT, NBLK, LANES), F32), pltpu.SemaphoreType.DMA((2,))],
        ),
        compiler_params=_cparams(("arbitrary", "arbitrary")),
        name="moe_combine_ctx" if ctx_tile0 else "moe_combine_lat",
    )(pos_flat, ys, x1, route, mod, ln_g, ln_b)


def _moe(h2, route, x1, mod, ln_g, ln_b, wg, wu, wd, layer, ctx_tile0):
    n_tok = h2.shape[0]
    n_tiles = (2 * n_tok) // TM + MOE_EXPERTS
    rank, cnt = _plan(route)
    counts = cnt[0, :MOE_EXPERTS].astype(I32)
    tiles_e = (counts + TM - 1) // TM
    tile_end = jnp.cumsum(tiles_e)
    tile_start = tile_end - tiles_e
    n_used = tile_end[-1]
    tj = jnp.arange(n_tiles, dtype=I32)
    te = jnp.minimum(jnp.sum((tile_end[None, :] <= tj[:, None]).astype(I32), axis=1), MOE_EXPERTS - 1)
    te_last = jnp.max(jnp.where(tj < n_used, te, 0))
    tile_expert = jnp.where(tj < n_used, te, te_last)
    eid = route[:, 0:2].astype(I32)
    pos = tile_start[eid] * TM + rank[:, 0:2].astype(I32)
    pos_flat = pos.reshape(-1)
    src = _invmap(pos_flat, n_tiles * TM)
    ys = _ffn(tile_expert, n_used.reshape(1), src, h2, wg, wu, wd, layer)
    return _combine(pos_flat, ys, x1, route, mod, ln_g, ln_b, ctx_tile0)


W_IN1 = MLA_KV_LORA + MLA_ROPE + MLA_Q_LORA + MLA_ROPE
Q_OFF = MLA_KV_LORA + MLA_ROPE
KRS_OFF = Q_OFF + MLA_Q_LORA
SM_SCALE = MLA_QK ** -0.5


def _rms(v, w):
    return v * lax.rsqrt(jnp.mean(v * v, axis=-1, keepdims=True) + RMS_EPS) * w


def _mla_proj_kernel(x_ref, mod_ref, win_ref, kvn_ref, qnw_ref, wkn_ref, wkv_ref, wqn_ref, wqr_ref, wqrs_ref,
                     cosk_ref, sink_ref, cosq_ref, sinq_ref, q_out, k_out, v_out):
    b = pl.program_id(0)
    t = pl.program_id(1)
    row = jnp.where(t == 0, NB, b)
    h = x_ref[0] * (1.0 + _mod_slice(mod_ref, row, 1)) + _mod_slice(mod_ref, row, 0)
    p = _dot(h.astype(BF16), win_ref[...])
    ckv = _rms(p[:, :MLA_KV_LORA], kvn_ref[...]).astype(BF16)
    kn = _dot(ckv, wkn_ref[...])
    vv = _dot(ckv, wkv_ref[...])
    kr = p[:, MLA_KV_LORA:Q_OFF]
    krs = p[:, KRS_OFF:KRS_OFF + MLA_ROPE]
    kr = jnp.where(t > 0, kr * cosk_ref[...] + krs * sink_ref[...], kr).astype(BF16)
    ql = _rms(p[:, Q_OFF:KRS_OFF], qnw_ref[...]).astype(BF16)
    qn = _dot(ql, wqn_ref[...]) * SM_SCALE
    qr = (_dot(ql, wqr_ref[...]) * cosq_ref[...] + _dot(ql, wqrs_ref[...]) * sinq_ref[...]) * SM_SCALE
    for hd in range(MLA_HEADS):
        k_out[0, hd, :, 0:MLA_NOPE] = kn[:, hd * MLA_NOPE:(hd + 1) * MLA_NOPE].astype(BF16)
        k_out[0, hd, :, MLA_NOPE:MLA_QK] = kr
        v_out[0, hd] = vv[:, hd * MLA_V:(hd + 1) * MLA_V].T.astype(BF16)
        q_out[0, hd, :, 0:MLA_NOPE] = qn[:, hd * MLA_NOPE:(hd + 1) * MLA_NOPE].astype(BF16)
        q_out[0, hd, :, MLA_NOPE:MLA_QK] = qr[:, hd * MLA_ROPE:(hd + 1) * MLA_ROPE].astype(BF16)


def _mla_proj(xs, mod, win, kvn, qnw, wkn, wkv, wqn, wqr, wqrs, cosk, sink, cosq, sinq):
    full = lambda a: pl.BlockSpec(a.shape, lambda b, t: (0, 0))
    lat = lambda w: pl.BlockSpec((TT, w), lambda b, t: (jnp.maximum(t - 1, 0), 0))
    return pl.pallas_call(
        _mla_proj_kernel,
        out_shape=(
            jax.ShapeDtypeStruct((NB, MLA_HEADS, SEQ, MLA_QK), BF16),
            jax.ShapeDtypeStruct((NB, MLA_HEADS, T, MLA_QK), BF16),
            jax.ShapeDtypeStruct((NB, MLA_HEADS, MLA_V, T), BF16),
        ),
        grid=(NB, NT),
        in_specs=[
            pl.BlockSpec((1, TT, D), lambda b, t: (b, t, 0)),
            pl.BlockSpec((16, N_MOD * D), lambda b, t: (0, 0)),
            full(win), full(kvn), full(qnw), full(wkn), full(wkv), full(wqn), full(wqr), full(wqrs),
            lat(MLA_ROPE), lat(MLA_ROPE), lat(MLA_HEADS * MLA_ROPE), lat(MLA_HEADS * MLA_ROPE),
        ],
        out_specs=(
            pl.BlockSpec((1, MLA_HEADS, TT, MLA_QK), lambda b, t: (b, 0, jnp.maximum(t - 1, 0), 0)),
            pl.BlockSpec((1, MLA_HEADS, TT, MLA_QK), lambda b, t: (b, 0, t, 0)),
            pl.BlockSpec((1, MLA_HEADS, MLA_V, TT), lambda b, t: (b, 0, 0, t)),
        ),
        compiler_params=_cparams(("arbitrary", "arbitrary")),
        name="mla_proj",
    )(xs, mod, win, kvn, qnw, wkn, wkv, wqn, wqr, wqrs, cosk, sink, cosq, sinq)


def _attn_kernel(q_ref, k_ref, vt_ref, o_ref, s_ref):
    k = k_ref[0, 0]
    vt = vt_ref[0, 0]
    nq = SEQ // TQ

    def scores(j):
        s_ref[j % 2] = _dot_nt(k, q_ref[0, 0, j * TQ:(j + 1) * TQ, :])

    scores(0)
    for j in range(nq):
        if j + 1 < nq:
            scores(j + 1)
        s = s_ref[j % 2]
        m = jnp.max(s, axis=0, keepdims=True)
        p = jnp.exp(s - m)
        l = jnp.sum(p, axis=0, keepdims=True)
        ot = _dot(vt, p.astype(BF16)) / l
        o_ref[0, j * TQ:(j + 1) * TQ, :] = ot.T.astype(BF16)


def _attn(q, k, vt):
    return pl.pallas_call(
        _attn_kernel,
        out_shape=jax.ShapeDtypeStruct((NB, SEQ, MLA_HEADS * MLA_V), BF16),
        grid=(NB, MLA_HEADS),
        in_specs=[
            pl.BlockSpec((1, 1, SEQ, MLA_QK), lambda b, h: (b, h, 0, 0)),
            pl.BlockSpec((1, 1, T, MLA_QK), lambda b, h: (b, h, 0, 0)),
            pl.BlockSpec((1, 1, MLA_V, T), lambda b, h: (b, h, 0, 0)),
        ],
        out_specs=pl.BlockSpec((1, SEQ, MLA_V), lambda b, h: (b, 0, h)),
        scratch_shapes=[pltpu.VMEM((2, T, TQ), F32)],
        compiler_params=_cparams(("arbitrary", "arbitrary")),
        name="mla_attn",
    )(q, k, vt)


def _rope_tables():
    n = SEQ
    rowp = (jnp.arange(n) // GRID_W).astype(F32)
    colp = (jnp.arange(n) % GRID_W).astype(F32)
    inv_freq = ROPE_BASE ** (-jnp.arange(ROPE_F, dtype=F32) / ROPE_F)
    ar = rowp[:, None] * inv_freq
    ac = colp[:, None] * inv_freq
    cos = jnp.concatenate([jnp.cos(ar), jnp.cos(ar), jnp.cos(ac), jnp.cos(ac)], axis=1)
    sin = jnp.concatenate([-jnp.sin(ar), jnp.sin(ar), -jnp.sin(ac), jnp.sin(ac)], axis=1)
    return cos, sin


def _rope_swap_perm():
    f = ROPE_F
    return jnp.concatenate([jnp.arange(f, 2 * f), jnp.arange(0, f), jnp.arange(3 * f, 4 * f), jnp.arange(2 * f, 3 * f)])


def _router_params(w_group, b_group, w_expert, b_expert):
    pad = LANES - MOE_GROUPS - MOE_EXPERTS
    wr = jnp.concatenate([w_group, w_expert, jnp.zeros((D, pad), F32)], axis=1)
    br = jnp.concatenate([b_group, b_expert, jnp.zeros((pad,), F32)]).reshape(1, LANES)
    return wr, br


def kernel(x, c, ctx, c_ctx, ada_w, ada_b, ln_g, ln_b, ab_w_in, pool_w, pool_scale, gdn_conv_w, gdn_a_log, gdn_dt_bias, gdn_norm_w, ab_w_out, mla_w_in, mla_kv_norm, mla_w_ukv, mla_q_norm, mla_w_uq, mla_w_out, moe_w_group, moe_b_group, moe_w_expert, moe_b_expert, moe_w_gate, moe_w_up, moe_w_down):
    assert x.shape == (NB, SEQ, D) and ctx.shape == (NB, CTX, D)
    xs = jnp.concatenate([ctx, x], axis=1)
    cv = jnp.concatenate([c, c_ctx[None, :], jnp.zeros((16 - NB - 1, D), F32)], axis=0)
    mod = _ada(cv, ada_w, ada_b)

    w_in = ab_w_in[0]
    w_main = w_in[:, :W_MAIN].astype(BF16)
    o_ab = W_MAIN
    ab_cols = []
    for g in range(NG):
        idx = [o_ab + kind * 2 * GDN_HEADS + d * GDN_HEADS + g * HB + hh
               for kind in range(2) for d in range(2) for hh in range(HB)]
        ab_cols.append(jnp.concatenate([w_in[:, jnp.array(idx)], jnp.zeros((D, LANES - len(idx)), F32)], axis=1))
    w_ab = jnp.concatenate(ab_cols, axis=1)

    def gate_rows(p):
        rows = []
        for g in range(NG):
            vals = jnp.stack([p[d, g * HB + hh] for d in range(2) for hh in range(HB)])
            rows.append(jnp.concatenate([vals, jnp.zeros((LANES - 2 * HB,), F32)]))
        return jnp.stack(rows).reshape(NG, 1, LANES)

    alog_g = gate_rows(gdn_a_log[0])
    dtb_g = gate_rows(gdn_dt_bias[0])
    wbd = jax.scipy.linalg.block_diag(*[pool_w[0, g] for g in range(len(POOL_WINDOWS))]).astype(BF16)

    pool_u, p_main, ab = _inproj0(xs, mod[0], w_main, w_ab)
    pool_y = _pool(pool_u, wbd, pool_scale[0].reshape(1, POOL_WIDTH))
    gdn_y = _gdn(p_main, gdn_conv_w[0], ab, alog_g, dtb_g, gdn_norm_w[0].reshape(1, GDN_DIM))

    w_out0 = ab_w_out[0].astype(BF16)
    wr0, br0 = _router_params(moe_w_group[0], moe_b_group[0], moe_w_expert[0], moe_b_expert[0])
    x1, h2, route = _mix([pool_y, gdn_y], [w_out0[:POOL_WIDTH], w_out0[POOL_WIDTH:]], xs, mod[0],
                         ln_g[0, 0].reshape(1, D), ln_b[0, 0].reshape(1, D), wr0, br0, True)
    xs = _moe(h2, route, x1, mod[0], ln_g[0, 1].reshape(1, D), ln_b[0, 1].reshape(1, D),
              moe_w_gate, moe_w_up, moe_w_down, 0, True)

    perm = _rope_swap_perm()
    w_in1 = mla_w_in[0]
    win = jnp.concatenate([w_in1, w_in1[:, MLA_KV_LORA:Q_OFF][:, perm]], axis=1).astype(BF16)
    ukv = mla_w_ukv[0].reshape(MLA_KV_LORA, MLA_HEADS, MLA_NOPE + MLA_V)
    wkn = ukv[:, :, :MLA_NOPE].reshape(MLA_KV_LORA, MLA_HEADS * MLA_NOPE).astype(BF16)
    wkv = ukv[:, :, MLA_NOPE:].reshape(MLA_KV_LORA, MLA_HEADS * MLA_V).astype(BF16)
    uq = mla_w_uq[0].reshape(MLA_Q_LORA, MLA_HEADS, MLA_QK)
    wqn = uq[:, :, :MLA_NOPE].reshape(MLA_Q_LORA, MLA_HEADS * MLA_NOPE).astype(BF16)
    wqr = uq[:, :, MLA_NOPE:].reshape(MLA_Q_LORA, MLA_HEADS * MLA_ROPE).astype(BF16)
    wqrs = uq[:, :, MLA_NOPE:][:, :, perm].reshape(MLA_Q_LORA, MLA_HEADS * MLA_ROPE).astype(BF16)
    cosk, sink = _rope_tables()
    cosq = jnp.tile(cosk, (1, MLA_HEADS))
    sinq = jnp.tile(sink, (1, MLA_HEADS))

    q, k, v = _mla_proj(xs, mod[1], win, mla_kv_norm[0].reshape(1, MLA_KV_LORA), mla_q_norm[0].reshape(1, MLA_Q_LORA),
                        wkn, wkv, wqn, wqr, wqrs, cosk, sink, cosq, sinq)
    att = _attn(q, k, v)
    wr1, br1 = _router_params(moe_w_group[1], moe_b_group[1], moe_w_expert[1], moe_b_expert[1])
    x1, h2, route = _mix([att], [mla_w_out[0].astype(BF16)], xs, mod[1],
                         ln_g[1, 0].reshape(1, D), ln_b[1, 0].reshape(1, D), wr1, br1, False)
    return _moe(h2, route, x1, mod[1], ln_g[1, 1].reshape(1, D), ln_b[1, 1].reshape(1, D),
                moe_w_gate, moe_w_up, moe_w_down, 1, False)
```

```python
import functools
import math

import jax
import jax.numpy as jnp
from jax import lax
from jax.experimental import pallas as pl
from jax.experimental.pallas import tpu as pltpu

F32 = jnp.float32
BF16 = jnp.bfloat16
I32 = jnp.int32

D = 1024
NB = 8
SEQ = 2048
CTX = 256
T = SEQ + CTX
DEPTH = 2
N_MOD = 6
ALPHA = (2 * DEPTH) ** 0.25
LN_EPS = 1e-5
RMS_EPS = 1e-6

POOL_WINDOWS = (2, 4, 8, 16)
POOL_WIDTH = 256
POOL_GROUP_DIM = 64
GDN_HEADS = 6
GDN_DIM = 128
GDN_WIDTH = GDN_HEADS * GDN_DIM
GDN_CONV = 4
CHUNK = 64

MLA_HEADS = 8
MLA_NOPE = 128
MLA_ROPE = 64
MLA_V = 128
MLA_QK = MLA_NOPE + MLA_ROPE
MLA_Q_LORA = 384
MLA_KV_LORA = 256
GRID_W = 64
ROPE_BASE = 10000.0
ROPE_F = MLA_ROPE // 4

MOE_GROUPS = 4
MOE_PER_GROUP = 8
MOE_EXPERTS = 32
MOE_FF = 512

TT = CTX
NT = T // TT
NTL = SEQ // TT
TM = 256
FFN_DMA_GROUPS = 8
PB = 512
HB = 2
NG = GDN_HEADS // HB
NC = T // CHUNK
NCC = CTX // CHUNK
TQ = 256
LANES = 128
VMEM_LIMIT = 56 * 1024 * 1024


def _dot(a, b):
    return jnp.dot(a, b, preferred_element_type=F32)


def _dot_nt(a, b):
    return lax.dot_general(a, b, (((1,), (1,)), ((), ())), preferred_element_type=F32)


def _dot_tn(a, b):
    return lax.dot_general(a, b, (((0,), (0,)), ((), ())), preferred_element_type=F32)


def _split2(x):
    hi = x.astype(BF16)
    lo = (x - hi.astype(F32)).astype(BF16)
    return hi, lo


def _dot3(a, b):
    ah, al = _split2(a)
    bh, bl = _split2(b)
    return _dot(ah, bh) + (_dot(ah, bl) + _dot(al, bh))


def _silu(x):
    return x * jax.nn.sigmoid(x)


def _softplus(x):
    return jnp.maximum(x, 0.0) + jnp.log1p(jnp.exp(-jnp.abs(x)))


def _layer_norm(v, g, b):
    mu = jnp.mean(v, axis=-1, keepdims=True)
    c = v - mu
    var = jnp.mean(c * c, axis=-1, keepdims=True)
    return c * lax.rsqrt(var + LN_EPS) * g + b


def _mod_slice(mod_ref, row, k):
    return mod_ref[pl.ds(row, 1), k * D:(k + 1) * D]


NBLK = D // LANES


def _to_token_tiles(y):
    return jnp.transpose(jnp.stack([y[:, s * LANES:(s + 1) * LANES] for s in range(NBLK)], axis=0), (1, 0, 2))


def _from_token_tiles(x3):
    xt = jnp.transpose(x3, (1, 0, 2))
    return jnp.concatenate([xt[s] for s in range(NBLK)], axis=1)


def _cparams(sem, vmem=VMEM_LIMIT):
    return pltpu.CompilerParams(dimension_semantics=sem, vmem_limit_bytes=vmem)


def _ada_kernel(cv_ref, w_ref, b_ref, o_ref):
    s = _silu(cv_ref[...])
    o_ref[0] = _dot3(s, w_ref[0]) + b_ref[0]


def _ada(cv, ada_w, ada_b):
    nblk = N_MOD
    return pl.pallas_call(
        _ada_kernel,
        out_shape=jax.ShapeDtypeStruct((DEPTH, 16, N_MOD * D), F32),
        grid=(DEPTH, nblk),
        in_specs=[
            pl.BlockSpec((16, D), lambda l, j: (0, 0)),
            pl.BlockSpec((1, D, D), lambda l, j: (l, 0, j)),
            pl.BlockSpec((1, 1, D), lambda l, j: (l, 0, j)),
        ],
        out_specs=pl.BlockSpec((1, 16, D), lambda l, j: (l, 0, j)),
        compiler_params=_cparams(("arbitrary", "arbitrary")),
        name="ada_mod",
    )(cv, ada_w, ada_b.reshape(DEPTH, 1, N_MOD * D))


W_MAIN = POOL_WIDTH + 4 * GDN_WIDTH
W_AB = NG * LANES


def _inproj0_kernel(x_ref, mod_ref, w_ref, wab_ref, pool_ref, main_ref, ab_ref):
    b = pl.program_id(0)
    t = pl.program_id(1)
    row = jnp.where(t == 0, NB, b)
    h = x_ref[0] * (1.0 + _mod_slice(mod_ref, row, 1)) + _mod_slice(mod_ref, row, 0)
    p = _dot(h.astype(BF16), w_ref[...])
    pool_ref[0] = p[:, :POOL_WIDTH]
    main_ref[0] = p[:, POOL_WIDTH:]
    ab_ref[0] = _dot3(h, wab_ref[...])


def _inproj0(xs, mod, w_main, w_ab):
    return pl.pallas_call(
        _inproj0_kernel,
        out_shape=(
            jax.ShapeDtypeStruct((NB, T, POOL_WIDTH), F32),
            jax.ShapeDtypeStruct((NB, T, 4 * GDN_WIDTH), F32),
            jax.ShapeDtypeStruct((NB, T, W_AB), F32),
        ),
        grid=(NB, NT),
        in_specs=[
            pl.BlockSpec((1, TT, D), lambda b, t: (b, t, 0)),
            pl.BlockSpec((16, N_MOD * D), lambda b, t: (0, 0)),
            pl.BlockSpec((D, W_MAIN), lambda b, t: (0, 0)),
            pl.BlockSpec((D, W_AB), lambda b, t: (0, 0)),
        ],
        out_specs=(
            pl.BlockSpec((1, TT, POOL_WIDTH), lambda b, t: (b, t, 0)),
            pl.BlockSpec((1, TT, 4 * GDN_WIDTH), lambda b, t: (b, t, 0)),
            pl.BlockSpec((1, TT, W_AB), lambda b, t: (b, t, 0)),
        ),
        compiler_params=_cparams(("arbitrary", "arbitrary")),
        name="inproj0",
    )(xs, mod, w_main, w_ab)


PAD_GAP = 16
PAD_CTX = PAD_GAP
PAD_LAT = PAD_CTX + CTX + 2 * PAD_GAP
PAD_ROWS = PAD_LAT + SEQ + PAD_GAP


def _fill_padded(pad_ref, src):
    w = pad_ref.shape[1]
    pad_ref[0:PAD_CTX, :] = jnp.zeros((PAD_CTX, w), F32)
    pad_ref[PAD_CTX + CTX:PAD_LAT, :] = jnp.zeros((2 * PAD_GAP, w), F32)
    pad_ref[PAD_LAT + SEQ:PAD_ROWS, :] = jnp.zeros((PAD_GAP, w), F32)
    pad_ref[PAD_CTX:PAD_CTX + CTX, :] = src(0, CTX)
    pad_ref[PAD_LAT:PAD_LAT + SEQ, :] = src(CTX, SEQ)


def _tile_pad_row(ti):
    return PAD_CTX if ti == 0 else PAD_LAT + (ti - 1) * TT


def _pool_kernel(u_ref, wbd_ref, scale_ref, o_ref, pad_ref):
    _fill_padded(pad_ref, lambda s, n: u_ref[0, s:s + n, :])
    lane = lax.broadcasted_iota(I32, (1, POOL_WIDTH), 1)
    grp = lane // POOL_GROUP_DIM
    win = jnp.zeros((1, POOL_WIDTH), I32)
    for g, w in enumerate(POOL_WINDOWS):
        win = jnp.where(grp == g, w, win)
    left = win // 2
    right = win - 1 - left
    for ti in range(NT):
        seg_len = CTX if ti == 0 else SEQ
        seg_t0 = 0 if ti == 0 else (ti - 1) * TT
        prow = _tile_pad_row(ti)
        tpos = seg_t0 + lax.broadcasted_iota(I32, (TT, 1), 0)
        acc = jnp.zeros((TT, POOL_WIDTH), F32)
        for j in range(-max(POOL_WINDOWS) // 2, max(POOL_WINDOWS) // 2):
            inwin = (j >= -left) & (j <= right)
            acc = acc + jnp.where(inwin, pad_ref[pl.ds(prow + j, TT), :], 0.0)
        cnt = jnp.minimum(tpos + right + 1, seg_len) - jnp.maximum(tpos - left, 0)
        dlt = acc / cnt.astype(F32) - pad_ref[pl.ds(prow, TT), :]
        y = _dot(dlt.astype(BF16), wbd_ref[...]) * scale_ref[...]
        o_ref[0, ti * TT:(ti + 1) * TT, :] = y.astype(BF16)


def _pool(pool_u, wbd, scale):
    return pl.pallas_call(
        _pool_kernel,
        out_shape=jax.ShapeDtypeStruct((NB, T, POOL_WIDTH), BF16),
        grid=(NB,),
        in_specs=[
            pl.BlockSpec((1, T, POOL_WIDTH), lambda b: (b, 0, 0)),
            pl.BlockSpec((POOL_WIDTH, POOL_WIDTH), lambda b: (0, 0)),
            pl.BlockSpec((1, POOL_WIDTH), lambda b: (0, 0)),
        ],
        out_specs=pl.BlockSpec((1, T, POOL_WIDTH), lambda b: (b, 0, 0)),
        scratch_shapes=[pltpu.VMEM((PAD_ROWS, POOL_WIDTH), F32)],
        compiler_params=_cparams(("arbitrary",)),
        name="pool",
    )(pool_u, wbd, scale)


HW = HB * GDN_DIM


NCH = 2 * HB
PW = NCH * CHUNK
GCH = 4


def _block_diag(xp, blk_masks):
    return jnp.concatenate([jnp.where(m, xp, 0.0) for m in blk_masks], axis=0).astype(BF16)


def _gdn_kernel(q_ref, k_ref, v_ref, z_ref, cwq_ref, cwk_ref, cwv_ref, ab_ref, alog_ref, dtb_ref, nw_ref,
                y_ref, pad_ref, qn_ref, kn_ref, vv_ref, m_ref, b_ref, qt_ref, o_ref, egl_ref, s_ref):
    def conv(x_ref, cw_ref, dst_ref, l2, scale):
        _fill_padded(pad_ref, lambda s, n: x_ref[0, s:s + n, :])
        cw = cw_ref[...]
        for ti in range(NT):
            prow = _tile_pad_row(ti)
            acc = jnp.zeros((TT, HW), F32)
            for j in range(GDN_CONV):
                acc = acc + pad_ref[pl.ds(prow - 2 + j, TT), :] * cw[j:j + 1, :]
            y = _silu(acc)
            if l2:
                for hh in range(HB):
                    yh = y[:, hh * GDN_DIM:(hh + 1) * GDN_DIM]
                    yh = yh * lax.rsqrt(jnp.sum(yh * yh, axis=-1, keepdims=True) + RMS_EPS)
                    dst_ref[ti * TT:(ti + 1) * TT, hh * GDN_DIM:(hh + 1) * GDN_DIM] = yh * scale
            else:
                dst_ref[ti * TT:(ti + 1) * TT, :] = y

    conv(q_ref, cwq_ref, qn_ref, True, GDN_DIM ** -0.5)
    conv(k_ref, cwk_ref, kn_ref, True, 1.0)
    conv(v_ref, cwv_ref, vv_ref, False, 1.0)

    ri = lax.broadcasted_iota(I32, (CHUNK, CHUNK), 0)
    ci = lax.broadcasted_iota(I32, (CHUNK, CHUNK), 1)
    tri2 = jnp.concatenate([(ci <= ri).astype(BF16), (ci >= ri).astype(BF16)], axis=0)
    rowp = lax.broadcasted_iota(I32, (CHUNK, PW), 0)
    lanep = lax.broadcasted_iota(I32, (CHUNK, PW), 1)
    blk = lanep // CHUNK
    colp = lanep - blk * CHUNK
    is_fwd = blk < HB
    ahead = jnp.where(is_fwd, rowp - colp, colp - rowp)
    incl_p = ahead >= 0
    strict_p = ahead > 0
    eye_p = (colp == rowp).astype(F32)
    eye_d = (lax.broadcasted_iota(I32, (GDN_DIM, GDN_DIM), 0)
             == lax.broadcasted_iota(I32, (GDN_DIM, GDN_DIM), 1)).astype(BF16)
    blk_masks = [blk == i for i in range(NCH)]
    alog = alog_ref[0]
    dtb = dtb_ref[0]

    def split3(v):
        h1 = v.astype(BF16)
        rem = v - h1.astype(F32)
        h2 = rem.astype(BF16)
        return h1, h2, (rem - h2.astype(F32)).astype(BF16)

    def phase1(it, carry):
        cs = [it * GCH + j for j in range(GCH)]
        r0 = [pl.multiple_of(c * CHUNK, CHUNK) for c in cs]
        m0 = [pl.multiple_of(c * GDN_DIM, GDN_DIM) for c in cs]
        e0 = [pl.multiple_of(c * 8, 8) for c in cs]
        G = range(GCH)
        abt = [ab_ref[0, pl.ds(r0[j], CHUNK), :] for j in G]
        g_all = [-jnp.exp(alog) * _softplus(abt[j] + dtb) for j in G]
        beta_all = [jax.nn.sigmoid(abt[j]) for j in G]
        g3 = [split3(g_all[j]) for j in G]
        gcs = [_dot(tri2, g3[j][0]) + (_dot(tri2, g3[j][1]) + _dot(tri2, g3[j][2])) for j in G]
        gb3 = [split3(jnp.where(strict_p, jnp.concatenate(
            [jnp.broadcast_to(g_all[j][:, i:i + 1], (CHUNK, CHUNK)) for i in range(NCH)], axis=1), 0.0)) for j in G]
        dif2 = [_dot(tri2, gb3[j][0]) + (_dot(tri2, gb3[j][1]) + _dot(tri2, gb3[j][2])) for j in G]
        kh = [[kn_ref[pl.ds(r0[j], CHUNK), hh * GDN_DIM:(hh + 1) * GDN_DIM] for hh in range(HB)] for j in G]
        qh = [[qn_ref[pl.ds(r0[j], CHUNK), hh * GDN_DIM:(hh + 1) * GDN_DIM] for hh in range(HB)] for j in G]
        vh = [[vv_ref[pl.ds(r0[j], CHUNK), hh * GDN_DIM:(hh + 1) * GDN_DIM] for hh in range(HB)] for j in G]
        gram = [[_dot_nt(jnp.concatenate([kh[j][hh].astype(BF16), qh[j][hh].astype(BF16), eye_d], axis=0),
                         kh[j][hh].astype(BF16)) for hh in range(HB)] for j in G]
        gcol1, bcol1, glast, egc1, kdsc = [], [], [], [], []
        for j in G:
            gcol1.append([]); bcol1.append([]); glast.append([]); egc1.append([]); kdsc.append([])
            for i in range(NCH):
                d = i // HB
                gc = gcs[j][d * CHUNK:(d + 1) * CHUNK]
                gcol1[j].append(gc[:, i:i + 1])
                bcol1[j].append(beta_all[j][:, NCH + i:NCH + i + 1])
                glast[j].append(gc[CHUNK - 1:CHUNK, i:i + 1] if d == 0 else gc[0:1, i:i + 1])
                egc1[j].append(jnp.exp(gcol1[j][i]))
                kdsc[j].append(jnp.exp(glast[j][i] - gcol1[j][i]))
                egl_ref[d, i % HB, pl.ds(e0[j], 8), :] = jnp.broadcast_to(jnp.exp(glast[j][i]), (8, LANES))
        lm, a_p = [], []
        for j in G:
            bcol_p = jnp.concatenate([jnp.broadcast_to(b, (CHUNK, CHUNK)) for b in bcol1[j]], axis=1)
            diff = jnp.where(is_fwd, dif2[j][:CHUNK], dif2[j][CHUNK:])
            dec = jnp.where(incl_p, jnp.exp(jnp.where(incl_p, diff, 0.0)), 0.0)
            kk_p = jnp.concatenate([gram[j][i % HB][:CHUNK] for i in range(NCH)], axis=1)
            qk_p = jnp.concatenate([gram[j][i % HB][CHUNK:2 * CHUNK] for i in range(NCH)], axis=1)
            lm.append(jnp.where(strict_p, bcol_p * kk_p * dec, 0.0))
            a_p.append(qk_p * dec)
        x = [eye_p - lm[j] for j in G]
        p = [_dot(lm[j].astype(BF16), _block_diag(lm[j], blk_masks)) for j in G]
        for _ in range(4):
            r = [_dot(jnp.concatenate([x[j], p[j]], axis=0).astype(BF16), _block_diag(p[j], blk_masks)) for j in G]
            x = [x[j] + r[j][:CHUNK] for j in G]
            p = [r[j][CHUNK:] for j in G]
        r = [_dot(x[j].astype(BF16), _block_diag(p[j], blk_masks)) for j in G]
        x = [x[j] + r[j] for j in G]
        rhs = [jnp.concatenate(
            [jnp.concatenate([vh[j][i % HB] * bcol1[j][i], kh[j][i % HB] * (bcol1[j][i] * egc1[j][i])], axis=1)
             for i in range(NCH)], axis=0).astype(BF16) for j in G]
        uw = [_dot(_block_diag(x[j], blk_masks), rhs[j]) for j in G]
        ao = [_dot(_block_diag(a_p[j], blk_masks), uw[j].astype(BF16)) for j in G]
        mb = [[_dot(gram[j][i % HB][2 * CHUNK:].astype(BF16),
                    (uw[j][i * CHUNK:(i + 1) * CHUNK] * kdsc[j][i]).astype(BF16)) for i in range(NCH)] for j in G]
        for j in G:
            for hh in range(HB):
                hs = slice(hh * GDN_DIM, (hh + 1) * GDN_DIM)
                o0 = jnp.zeros((CHUNK, GDN_DIM), F32)
                for d in range(2):
                    i = d * HB + hh
                    rs = slice(i * CHUNK, (i + 1) * CHUNK)
                    b_ref[d, hh, pl.ds(m0[j], GDN_DIM), :] = mb[j][i][:, :GDN_DIM].astype(BF16)
                    m_ref[d, hh, pl.ds(m0[j], GDN_DIM), :] = mb[j][i][:, GDN_DIM:].astype(BF16)
                    qt_ref[d, pl.ds(r0[j], CHUNK), hs] = (qh[j][hh] * egc1[j][i] - ao[j][rs, GDN_DIM:]).astype(BF16)
                    o0 = o0 + ao[j][rs, :GDN_DIM]
                o_ref[pl.ds(r0[j], CHUNK), hs] = o0
        return carry

    lax.fori_loop(0, NC // GCH, phase1, 0)

    s_ref[...] = jnp.zeros(s_ref.shape, F32)

    def phase2(step, carry):
        cb = jnp.where(step < NCC, NCC - 1 - step, NC + NCC - 1 - step)
        chains = [(d, hh, c) for d, c in ((0, step), (1, cb)) for hh in range(HB)]
        rows = [pl.ds(pl.multiple_of(c * CHUNK, CHUNK), CHUNK) for _, _, c in chains]
        mrows = [pl.ds(pl.multiple_of(c * GDN_DIM, GDN_DIM), GDN_DIM) for _, _, c in chains]
        hs = [slice(hh * GDN_DIM, (hh + 1) * GDN_DIM) for _, hh, _ in chains]
        n = range(len(chains))
        st = [s_ref[chains[i][0], chains[i][1]] for i in n]
        stb = [st[i].astype(BF16) for i in n]
        ms = [_dot(m_ref[chains[i][0], chains[i][1], mrows[i], :], stb[i]) for i in n]
        oq = [_dot(qt_ref[chains[i][0], rows[i], hs[i]], stb[i]) for i in n]
        for i in n:
            d, hh, c = chains[i]
            egl = egl_ref[d, hh, pl.ds(pl.multiple_of(c * 8, 8), 1), :]
            s_ref[d, hh] = st[i] * egl - ms[i] + b_ref[d, hh, mrows[i], :].astype(F32)
            o_ref[rows[i], hs[i]] += oq[i]
        return carry

    lax.fori_loop(0, NC, phase2, 0)

    nw = nw_ref[...]
    for ti in range(NT):
        rs = slice(ti * TT, (ti + 1) * TT)
        o = o_ref[rs, :]
        zz = _silu(z_ref[0, rs, :])
        for hh in range(HB):
            hs = slice(hh * GDN_DIM, (hh + 1) * GDN_DIM)
            oh = o[:, hs]
            oh = oh * lax.rsqrt(jnp.mean(oh * oh, axis=-1, keepdims=True) + RMS_EPS) * nw
            y_ref[0, rs, hs] = (oh * zz[:, hs]).astype(BF16)


def _gdn(p_main, conv_w, ab, alog_g, dtb_g, norm_w):
    nhb = GDN_WIDTH // HW
    blk = lambda off: pl.BlockSpec((1, T, HW), lambda b, g: (b, 0, off * nhb + g))
    cblk = lambda off: pl.BlockSpec((GDN_CONV, HW), lambda b, g: (0, off * nhb + g))
    return pl.pallas_call(
        _gdn_kernel,
        out_shape=jax.ShapeDtypeStruct((NB, T, GDN_WIDTH), BF16),
        grid=(NB, NG),
        in_specs=[
            blk(0), blk(1), blk(2), blk(3),
            cblk(0), cblk(1), cblk(2),
            pl.BlockSpec((1, T, LANES), lambda b, g: (b, 0, g)),
            pl.BlockSpec((1, 1, LANES), lambda b, g: (g, 0, 0)),
            pl.BlockSpec((1, 1, LANES), lambda b, g: (g, 0, 0)),
            pl.BlockSpec((1, GDN_DIM), lambda b, g: (0, 0)),
        ],
        out_specs=pl.BlockSpec((1, T, HW), lambda b, g: (b, 0, g)),
        scratch_shapes=[
            pltpu.VMEM((PAD_ROWS, HW), F32),
            pltpu.VMEM((T, HW), F32),
            pltpu.VMEM((T, HW), F32),
            pltpu.VMEM((T, HW), F32),
            pltpu.VMEM((2, HB, NC * GDN_DIM, GDN_DIM), BF16),
            pltpu.VMEM((2, HB, NC * GDN_DIM, GDN_DIM), BF16),
            pltpu.VMEM((2, T, HW), BF16),
            pltpu.VMEM((T, HW), F32),
            pltpu.VMEM((2, HB, NC * 8, LANES), F32),
            pltpu.VMEM((2, HB, GDN_DIM, GDN_DIM), F32),
        ],
        compiler_params=_cparams(("arbitrary", "arbitrary")),
        name="gdn",
    )(p_main, p_main, p_main, p_main, conv_w, conv_w, conv_w, ab, alog_g, dtb_g, norm_w)


NEG = -1e30


def _mix_kernel(n_act, ctx_tile0, *refs):
    act_refs = refs[:n_act]
    w_refs = refs[n_act:2 * n_act]
    x_ref, mod_ref, lng_ref, lnb_ref, wr_ref, br_ref, x1_ref, h2_ref, route_ref = refs[2 * n_act:]
    b = pl.program_id(0)
    t = pl.program_id(1)
    row = jnp.where(t == 0, NB, b) if ctx_tile0 else b
    y = _dot(act_refs[0][0], w_refs[0][...])
    for a_ref, w_ref in zip(act_refs[1:], w_refs[1:]):
        y = y + _dot(a_ref[0], w_ref[...])
    x1 = _layer_norm(ALPHA * x_ref[0] + _mod_slice(mod_ref, row, 2) * y, lng_ref[...], lnb_ref[...])
    x1_ref[0] = x1
    h2 = x1 * (1.0 + _mod_slice(mod_ref, row, 4)) + _mod_slice(mod_ref, row, 3)
    h2_ref[...] = _to_token_tiles(h2)
    logits = _dot3(h2, wr_ref[...]) + br_ref[...]
    lane = lax.broadcasted_iota(I32, (TT, LANES), 1)
    gl = jnp.where(lane < MOE_GROUPS, logits, NEG)
    gm = jnp.max(gl, axis=-1, keepdims=True)
    gsum = jnp.sum(jnp.exp(gl - gm), axis=-1, keepdims=True)
    g_val = 1.0 / gsum
    g_idx = jnp.min(jnp.where(gl == gm, lane, LANES), axis=-1, keepdims=True)
    sel = (lane >= MOE_GROUPS) & (lane < MOE_GROUPS + MOE_EXPERTS) & (((lane - MOE_GROUPS) // MOE_PER_GROUP) == g_idx)
    el = jnp.where(sel, logits, NEG)
    m1 = jnp.max(el, axis=-1, keepdims=True)
    i1 = jnp.min(jnp.where(el == m1, lane, LANES), axis=-1, keepdims=True)
    el2 = jnp.where(lane == i1, NEG, el)
    m2 = jnp.max(el2, axis=-1, keepdims=True)
    i2 = jnp.min(jnp.where(el2 == m2, lane, LANES), axis=-1, keepdims=True)
    ex2 = jnp.exp(m2 - m1)
    w1 = g_val / (1.0 + ex2)
    w2 = w1 * ex2
    e1 = (i1 - MOE_GROUPS).astype(F32)
    e2 = (i2 - MOE_GROUPS).astype(F32)
    route_ref[...] = jnp.where(lane == 0, e1, jnp.where(lane == 1, e2, jnp.where(lane == 2, w1, jnp.where(lane == 3, w2, 0.0))))


def _mix(acts, ws, xs, mod, ln_g, ln_b, wr, br, ctx_tile0):
    ntl = NT if ctx_tile0 else NTL
    toff = 0 if ctx_tile0 else 1
    n_tok = NB * ntl * TT
    act_specs = [pl.BlockSpec((1, TT, a.shape[2]), lambda b, t: (b, t, 0)) for a in acts]
    w_specs = [pl.BlockSpec(w.shape, lambda b, t: (0, 0)) for w in ws]
    return pl.pallas_call(
        functools.partial(_mix_kernel, len(acts), ctx_tile0),
        out_shape=(
            jax.ShapeDtypeStruct((NB, ntl * TT, D), F32),
            jax.ShapeDtypeStruct((n_tok, NBLK, LANES), F32),
            jax.ShapeDtypeStruct((n_tok, LANES), F32),
        ),
        grid=(NB, ntl),
        in_specs=act_specs + w_specs + [
            pl.BlockSpec((1, TT, D), lambda b, t: (b, t + toff, 0)),
            pl.BlockSpec((16, N_MOD * D), lambda b, t: (0, 0)),
            pl.BlockSpec((1, D), lambda b, t: (0, 0)),
            pl.BlockSpec((1, D), lambda b, t: (0, 0)),
            pl.BlockSpec((D, LANES), lambda b, t: (0, 0)),
            pl.BlockSpec((1, LANES), lambda b, t: (0, 0)),
        ],
        out_specs=(
            pl.BlockSpec((1, TT, D), lambda b, t: (b, t, 0)),
            pl.BlockSpec((TT, NBLK, LANES), lambda b, t: (b * ntl + t, 0, 0)),
            pl.BlockSpec((TT, LANES), lambda b, t: (b * ntl + t, 0)),
        ),
        compiler_params=_cparams(("arbitrary", "arbitrary")),
        name="mix_ctx" if ctx_tile0 else "mix_lat",
    )(*acts, *ws, xs, mod, ln_g, ln_b, wr, br)


def _plan_kernel(route_ref, rank_ref, cnt_ref, carry_ref):
    i = pl.program_id(0)

    @pl.when(i == 0)
    def _():
        carry_ref[...] = jnp.zeros(carry_ref.shape, F32)

    r = route_ref[...]
    lane = lax.broadcasted_iota(I32, (PB, LANES), 1)
    lanef = lane.astype(F32)
    ri = lax.broadcasted_iota(I32, (PB, PB), 0)
    ci = lax.broadcasted_iota(I32, (PB, PB), 1)
    before = (ci < ri).astype(BF16)
    oh1 = jnp.where(lanef == r[:, 0:1], 1.0, 0.0)
    oh2 = jnp.where(lanef == r[:, 1:2], 1.0, 0.0)
    c0 = carry_ref[0:1, :]
    tot1 = jnp.sum(oh1, axis=0, keepdims=True)
    tot2 = jnp.sum(oh2, axis=0, keepdims=True)
    r1 = _dot(before, oh1.astype(BF16)) + c0
    r2 = _dot(before, oh2.astype(BF16)) + (c0 + tot1)
    rank1 = jnp.sum(r1 * oh1, axis=-1, keepdims=True)
    rank2 = jnp.sum(r2 * oh2, axis=-1, keepdims=True)
    rank_ref[...] = jnp.where(lane == 0, rank1, jnp.where(lane == 1, rank2, 0.0))
    cnew = jnp.broadcast_to(c0 + tot1 + tot2, carry_ref.shape)
    carry_ref[...] = cnew
    cnt_ref[...] = cnew


def _plan(route):
    n_tok = route.shape[0]
    return pl.pallas_call(
        _plan_kernel,
        out_shape=(jax.ShapeDtypeStruct((n_tok, LANES), F32), jax.ShapeDtypeStruct((8, LANES), F32)),
        grid=(n_tok // PB,),
        in_specs=[pl.BlockSpec((PB, LANES), lambda i: (i, 0))],
        out_specs=(pl.BlockSpec((PB, LANES), lambda i: (i, 0)), pl.BlockSpec((8, LANES), lambda i: (0, 0))),
        scratch_shapes=[pltpu.VMEM((8, LANES), F32)],
        compiler_params=_cparams(("arbitrary",)),
        name="moe_plan",
    )(route)


INV_UNROLL = 16


def _invmap_kernel(pos_ref, src_ref):
    n_rows = src_ref.shape[0]
    n_tok = pos_ref.shape[0] // 2

    def clear(g, c):
        for u in range(INV_UNROLL):
            src_ref[g * INV_UNROLL + u] = 0
        return c

    lax.fori_loop(0, n_rows // INV_UNROLL, clear, 0)

    def put(g, c):
        for u in range(INV_UNROLL):
            t = g * INV_UNROLL + u
            src_ref[pos_ref[2 * t]] = t
            src_ref[pos_ref[2 * t + 1]] = t
        return c

    lax.fori_loop(0, n_tok // INV_UNROLL, put, 0)


def _invmap(pos_flat, n_rows):
    return pl.pallas_call(
        _invmap_kernel,
        out_shape=jax.ShapeDtypeStruct((n_rows,), I32),
        in_specs=[pl.BlockSpec(memory_space=pltpu.SMEM)],
        out_specs=pl.BlockSpec(memory_space=pltpu.SMEM),
        name="moe_invmap",
    )(pos_flat)


def _ffn_kernel(te_ref, nu_ref, src_ref, h_hbm, wg_ref, wu_ref, wd_ref, y_ref, buf, wgb, wub, wdb, sem):
    i = pl.program_id(0)
    nu = nu_ref[0]

    def issue(tile, slot, lo, hi):
        base = tile * TM
        for r in range(lo, hi):
            pltpu.make_async_copy(h_hbm.at[src_ref[base + r]], buf.at[slot, r], sem.at[slot]).start(priority=r % 2)

    def ffn_tile(slot, next_tile):
        per = TM // FFN_DMA_GROUPS
        sent = [0]

        def gather_some():
            if next_tile is not None and sent[0] < FFN_DMA_GROUPS:
                issue(next_tile, 1 - slot, sent[0] * per, (sent[0] + 1) * per)
                sent[0] += 1

        x = _from_token_tiles(buf[slot]).astype(BF16)
        hcol = MOE_FF // 2
        gather_some()
        parts = []
        for w in (wgb, wub):
            for c in range(2):
                parts.append(_dot(x, w[:, c * hcol:(c + 1) * hcol]))
                gather_some()
        a = jnp.concatenate(parts[0:2], axis=1)
        bb = jnp.concatenate(parts[2:4], axis=1)
        act = (_silu(a) * bb).astype(BF16)
        ys = []
        for c in range(D // hcol):
            ys.append(_dot(act, wdb[:, c * hcol:(c + 1) * hcol]))
            gather_some()
        assert next_tile is None or sent[0] == FFN_DMA_GROUPS
        y_ref[...] = _to_token_tiles(jnp.concatenate(ys, axis=1))

    @pl.when(i == 0)
    def _():
        issue(0, 0, 0, TM)

    @pl.when(i < nu)
    def _():
        slot = i % 2
        pltpu.make_async_copy(h_hbm.at[pl.ds(0, TM)], buf.at[slot], sem.at[slot]).wait()
        e = te_ref[i]
        ep = te_ref[jnp.maximum(i - 1, 0)]

        @pl.when((i == 0) | (e != ep))
        def _():
            wgb[...] = wg_ref[0, 0].astype(BF16)
            wub[...] = wu_ref[0, 0].astype(BF16)
            wdb[...] = wd_ref[0, 0].astype(BF16)

        @pl.when(i + 1 < nu)
        def _():
            ffn_tile(slot, i + 1)

        @pl.when(i + 1 >= nu)
        def _():
            ffn_tile(slot, None)

    @pl.when(i >= nu)
    def _():
        y_ref[...] = jnp.zeros(y_ref.shape, F32)


def _ffn(tile_expert, n_used, src, h2, wg, wu, wd, layer):
    n_rows = src.shape[0]
    n_tiles = n_rows // TM
    wspec = lambda shape: pl.BlockSpec((1, 1) + shape, lambda i, te, nu, src: (layer, te[i], 0, 0))
    return pl.pallas_call(
        _ffn_kernel,
        out_shape=jax.ShapeDtypeStruct((n_rows, NBLK, LANES), F32),
        grid_spec=pltpu.PrefetchScalarGridSpec(
            num_scalar_prefetch=3,
            grid=(n_tiles,),
            in_specs=[
                pl.BlockSpec(memory_space=pl.ANY),
                wspec((D, MOE_FF)), wspec((D, MOE_FF)), wspec((MOE_FF, D)),
            ],
            out_specs=pl.BlockSpec((TM, NBLK, LANES), lambda i, te, nu, src: (i, 0, 0)),
            scratch_shapes=[
                pltpu.VMEM((2, TM, NBLK, LANES), F32),
                pltpu.VMEM((D, MOE_FF), BF16),
                pltpu.VMEM((D, MOE_FF), BF16),
                pltpu.VMEM((MOE_FF, D), BF16),
                pltpu.SemaphoreType.DMA((2,)),
            ],
        ),
        compiler_params=_cparams(("arbitrary",)),
        name="moe_ffn",
    )(tile_expert, n_used, src, h2, wg, wu, wd)


def _combine_kernel(ntl, ctx_tile0, pos_ref, y_hbm, x1_ref, route_ref, mod_ref, lng_ref, lnb_ref, o_ref, buf, sem):
    b = pl.program_id(0)
    t = pl.program_id(1)
    flat = b * ntl + t
    n_tiles = NB * ntl

    def issue(tile, slot):
        base = tile * (2 * TT)
        for r in range(TT):
            for k in range(2):
                pltpu.make_async_copy(y_hbm.at[pos_ref[base + 2 * r + k]], buf.at[slot, k * TT + r], sem.at[slot]).start(priority=k)

    @pl.when(flat == 0)
    def _():
        issue(0, 0)

    slot = flat % 2
    pltpu.make_async_copy(y_hbm.at[pl.ds(0, 2 * TT)], buf.at[slot], sem.at[slot]).wait()

    @pl.when(flat + 1 < n_tiles)
    def _():
        issue(flat + 1, 1 - slot)

    row = jnp.where(t == 0, NB, b) if ctx_tile0 else b
    r = route_ref[...]
    f = (r[:, 2:3] * _from_token_tiles(buf[slot, 0:TT]) + r[:, 3:4] * _from_token_tiles(buf[slot, TT:2 * TT]))
    o_ref[0] = _layer_norm(ALPHA * x1_ref[0] + _mod_slice(mod_ref, row, 5) * f, lng_ref[...], lnb_ref[...])


def _combine(pos_flat, ys, x1, route, mod, ln_g, ln_b, ctx_tile0):
    ntl = NT if ctx_tile0 else NTL
    return pl.pallas_call(
        functools.partial(_combine_kernel, ntl, ctx_tile0),
        out_shape=jax.ShapeDtypeStruct((NB, ntl * TT, D), F32),
        grid_spec=pltpu.PrefetchScalarGridSpec(
            num_scalar_prefetch=1,
            grid=(NB, ntl),
            in_specs=[
                pl.BlockSpec(memory_space=pl.ANY),
                pl.BlockSpec((1, TT, D), lambda b, t, pos: (b, t, 0)),
                pl.BlockSpec((TT, LANES), lambda b, t, pos: (b * ntl + t, 0)),
                pl.BlockSpec((16, N_MOD * D), lambda b, t, pos: (0, 0)),
                pl.BlockSpec((1, D), lambda b, t, pos: (0, 0)),
                pl.BlockSpec((1, D), lambda b, t, pos: (0, 0)),
            ],
            out_specs=pl.BlockSpec((1, TT, D), lambda b, t, pos: (b, t, 0)),
            scratch_shapes=[pltpu.VMEM((2, 2 * TT, NBLK, LANES), F32), pltpu.SemaphoreType.DMA((2,))],
        ),
        compiler_params=_cparams(("arbitrary", "arbitrary")),
        name="moe_combine_ctx" if ctx_tile0 else "moe_combine_lat",
    )(pos_flat, ys, x1, route, mod, ln_g, ln_b)


def _moe(h2, route, x1, mod, ln_g, ln_b, wg, wu, wd, layer, ctx_tile0):
    n_tok = h2.shape[0]
    n_tiles = (2 * n_tok) // TM + MOE_EXPERTS
    rank, cnt = _plan(route)
    counts = cnt[0, :MOE_EXPERTS].astype(I32)
    tiles_e = (counts + TM - 1) // TM
    tile_end = jnp.cumsum(tiles_e)
    tile_start = tile_end - tiles_e
    n_used = tile_end[-1]
    tj = jnp.arange(n_tiles, dtype=I32)
    te = jnp.minimum(jnp.sum((tile_end[None, :] <= tj[:, None]).astype(I32), axis=1), MOE_EXPERTS - 1)
    te_last = jnp.max(jnp.where(tj < n_used, te, 0))
    tile_expert = jnp.where(tj < n_used, te, te_last)
    eid = route[:, 0:2].astype(I32)
    pos = tile_start[eid] * TM + rank[:, 0:2].astype(I32)
    pos_flat = pos.reshape(-1)
    src = _invmap(pos_flat, n_tiles * TM)
    ys = _ffn(tile_expert, n_used.reshape(1), src, h2, wg, wu, wd, layer)
    return _combine(pos_flat, ys, x1, route, mod, ln_g, ln_b, ctx_tile0)


W_IN1 = MLA_KV_LORA + MLA_ROPE + MLA_Q_LORA + MLA_ROPE
Q_OFF = MLA_KV_LORA + MLA_ROPE
KRS_OFF = Q_OFF + MLA_Q_LORA
SM_SCALE = MLA_QK ** -0.5


def _rms(v, w):
    return v * lax.rsqrt(jnp.mean(v * v, axis=-1, keepdims=True) + RMS_EPS) * w


def _mla_proj_kernel(x_ref, mod_ref, win_ref, kvn_ref, qnw_ref, wkn_ref, wkv_ref, wqn_ref, wqr_ref, wqrs_ref,
                     cosk_ref, sink_ref, cosq_ref, sinq_ref, q_out, k_out, v_out):
    b = pl.program_id(0)
    t = pl.program_id(1)
    row = jnp.where(t == 0, NB, b)
    h = x_ref[0] * (1.0 + _mod_slice(mod_ref, row, 1)) + _mod_slice(mod_ref, row, 0)
    p = _dot(h.astype(BF16), win_ref[...])
    ckv = _rms(p[:, :MLA_KV_LORA], kvn_ref[...]).astype(BF16)
    kn = _dot(ckv, wkn_ref[...])
    vv = _dot(ckv, wkv_ref[...])
    kr = p[:, MLA_KV_LORA:Q_OFF]
    krs = p[:, KRS_OFF:KRS_OFF + MLA_ROPE]
    kr = jnp.where(t > 0, kr * cosk_ref[...] + krs * sink_ref[...], kr).astype(BF16)
    ql = _rms(p[:, Q_OFF:KRS_OFF], qnw_ref[...]).astype(BF16)
    qn = _dot(ql, wqn_ref[...]) * SM_SCALE
    qr = (_dot(ql, wqr_ref[...]) * cosq_ref[...] + _dot(ql, wqrs_ref[...]) * sinq_ref[...]) * SM_SCALE
    for hd in range(MLA_HEADS):
        k_out[0, hd, :, 0:MLA_NOPE] = kn[:, hd * MLA_NOPE:(hd + 1) * MLA_NOPE].astype(BF16)
        k_out[0, hd, :, MLA_NOPE:MLA_QK] = kr
        v_out[0, hd] = vv[:, hd * MLA_V:(hd + 1) * MLA_V].T.astype(BF16)
        q_out[0, hd, :, 0:MLA_NOPE] = qn[:, hd * MLA_NOPE:(hd + 1) * MLA_NOPE].astype(BF16)
        q_out[0, hd, :, MLA_NOPE:MLA_QK] = qr[:, hd * MLA_ROPE:(hd + 1) * MLA_ROPE].astype(BF16)


def _mla_proj(xs, mod, win, kvn, qnw, wkn, wkv, wqn, wqr, wqrs, cosk, sink, cosq, sinq):
    full = lambda a: pl.BlockSpec(a.shape, lambda b, t: (0, 0))
    lat = lambda w: pl.BlockSpec((TT, w), lambda b, t: (jnp.maximum(t - 1, 0), 0))
    return pl.pallas_call(
        _mla_proj_kernel,
        out_shape=(
            jax.ShapeDtypeStruct((NB, MLA_HEADS, SEQ, MLA_QK), BF16),
            jax.ShapeDtypeStruct((NB, MLA_HEADS, T, MLA_QK), BF16),
            jax.ShapeDtypeStruct((NB, MLA_HEADS, MLA_V, T), BF16),
        ),
        grid=(NB, NT),
        in_specs=[
            pl.BlockSpec((1, TT, D), lambda b, t: (b, t, 0)),
            pl.BlockSpec((16, N_MOD * D), lambda b, t: (0, 0)),
            full(win), full(kvn), full(qnw), full(wkn), full(wkv), full(wqn), full(wqr), full(wqrs),
            lat(MLA_ROPE), lat(MLA_ROPE), lat(MLA_HEADS * MLA_ROPE), lat(MLA_HEADS * MLA_ROPE),
        ],
        out_specs=(
            pl.BlockSpec((1, MLA_HEADS, TT, MLA_QK), lambda b, t: (b, 0, jnp.maximum(t - 1, 0), 0)),
            pl.BlockSpec((1, MLA_HEADS, TT, MLA_QK), lambda b, t: (b, 0, t, 0)),
            pl.BlockSpec((1, MLA_HEADS, MLA_V, TT), lambda b, t: (b, 0, 0, t)),
        ),
        compiler_params=_cparams(("arbitrary", "arbitrary")),
        name="mla_proj",
    )(xs, mod, win, kvn, qnw, wkn, wkv, wqn, wqr, wqrs, cosk, sink, cosq, sinq)


def _attn_kernel(q_ref, k_ref, vt_ref, o_ref, s_ref):
    k = k_ref[0, 0]
    vt = vt_ref[0, 0]
    nq = SEQ // TQ

    def scores(j):
        s_ref[j % 2] = _dot_nt(k, q_ref[0, 0, j * TQ:(j + 1) * TQ, :])

    scores(0)
    for j in range(nq):
        if j + 1 < nq:
            scores(j + 1)
        s = s_ref[j % 2]
        m = jnp.max(s, axis=0, keepdims=True)
        p = jnp.exp(s - m)
        l = jnp.sum(p, axis=0, keepdims=True)
        ot = _dot(vt, p.astype(BF16)) / l
        o_ref[0, j * TQ:(j + 1) * TQ, :] = ot.T.astype(BF16)


def _attn(q, k, vt):
    return pl.pallas_call(
        _attn_kernel,
        out_shape=jax.ShapeDtypeStruct((NB, SEQ, MLA_HEADS * MLA_V), BF16),
        grid=(NB, MLA_HEADS),
        in_specs=[
            pl.BlockSpec((1, 1, SEQ, MLA_QK), lambda b, h: (b, h, 0, 0)),
            pl.BlockSpec((1, 1, T, MLA_QK), lambda b, h: (b, h, 0, 0)),
            pl.BlockSpec((1, 1, MLA_V, T), lambda b, h: (b, h, 0, 0)),
        ],
        out_specs=pl.BlockSpec((1, SEQ, MLA_V), lambda b, h: (b, 0, h)),
        scratch_shapes=[pltpu.VMEM((2, T, TQ), F32)],
        compiler_params=_cparams(("arbitrary", "arbitrary")),
        name="mla_attn",
    )(q, k, vt)


def _rope_tables():
    n = SEQ
    rowp = (jnp.arange(n) // GRID_W).astype(F32)
    colp = (jnp.arange(n) % GRID_W).astype(F32)
    inv_freq = ROPE_BASE ** (-jnp.arange(ROPE_F, dtype=F32) / ROPE_F)
    ar = rowp[:, None] * inv_freq
    ac = colp[:, None] * inv_freq
    cos = jnp.concatenate([jnp.cos(ar), jnp.cos(ar), jnp.cos(ac), jnp.cos(ac)], axis=1)
    sin = jnp.concatenate([-jnp.sin(ar), jnp.sin(ar), -jnp.sin(ac), jnp.sin(ac)], axis=1)
    return cos, sin


def _rope_swap_perm():
    f = ROPE_F
    return jnp.concatenate([jnp.arange(f, 2 * f), jnp.arange(0, f), jnp.arange(3 * f, 4 * f), jnp.arange(2 * f, 3 * f)])


def _router_params(w_group, b_group, w_expert, b_expert):
    pad = LANES - MOE_GROUPS - MOE_EXPERTS
    wr = jnp.concatenate([w_group, w_expert, jnp.zeros((D, pad), F32)], axis=1)
    br = jnp.concatenate([b_group, b_expert, jnp.zeros((pad,), F32)]).reshape(1, LANES)
    return wr, br


def kernel(x, c, ctx, c_ctx, ada_w, ada_b, ln_g, ln_b, ab_w_in, pool_w, pool_scale, gdn_conv_w, gdn_a_log, gdn_dt_bias, gdn_norm_w, ab_w_out, mla_w_in, mla_kv_norm, mla_w_ukv, mla_q_norm, mla_w_uq, mla_w_out, moe_w_group, moe_b_group, moe_w_expert, moe_b_expert, moe_w_gate, moe_w_up, moe_w_down):
    assert x.shape == (NB, SEQ, D) and ctx.shape == (NB, CTX, D)
    xs = jnp.concatenate([ctx, x], axis=1)
    cv = jnp.concatenate([c, c_ctx[None, :], jnp.zeros((16 - NB - 1, D), F32)], axis=0)
    mod = _ada(cv, ada_w, ada_b)

    w_in = ab_w_in[0]
    w_main = w_in[:, :W_MAIN].astype(BF16)
    o_ab = W_MAIN
    ab_cols = []
    for g in range(NG):
        idx = [o_ab + kind * 2 * GDN_HEADS + d * GDN_HEADS + g * HB + hh
               for kind in range(2) for d in range(2) for hh in range(HB)]
        ab_cols.append(jnp.concatenate([w_in[:, jnp.array(idx)], jnp.zeros((D, LANES - len(idx)), F32)], axis=1))
    w_ab = jnp.concatenate(ab_cols, axis=1)

    def gate_rows(p):
        rows = []
        for g in range(NG):
            vals = jnp.stack([p[d, g * HB + hh] for d in range(2) for hh in range(HB)])
            rows.append(jnp.concatenate([vals, jnp.zeros((LANES - 2 * HB,), F32)]))
        return jnp.stack(rows).reshape(NG, 1, LANES)

    alog_g = gate_rows(gdn_a_log[0])
    dtb_g = gate_rows(gdn_dt_bias[0])
    wbd = jax.scipy.linalg.block_diag(*[pool_w[0, g] for g in range(len(POOL_WINDOWS))]).astype(BF16)

    pool_u, p_main, ab = _inproj0(xs, mod[0], w_main, w_ab)
    pool_y = _pool(pool_u, wbd, pool_scale[0].reshape(1, POOL_WIDTH))
    gdn_y = _gdn(p_main, gdn_conv_w[0], ab, alog_g, dtb_g, gdn_norm_w[0].reshape(1, GDN_DIM))

    w_out0 = ab_w_out[0].astype(BF16)
    wr0, br0 = _router_params(moe_w_group[0], moe_b_group[0], moe_w_expert[0], moe_b_expert[0])
    x1, h2, route = _mix([pool_y, gdn_y], [w_out0[:POOL_WIDTH], w_out0[POOL_WIDTH:]], xs, mod[0],
                         ln_g[0, 0].reshape(1, D), ln_b[0, 0].reshape(1, D), wr0, br0, True)
    xs = _moe(h2, route, x1, mod[0], ln_g[0, 1].reshape(1, D), ln_b[0, 1].reshape(1, D),
              moe_w_gate, moe_w_up, moe_w_down, 0, True)

    perm = _rope_swap_perm()
    w_in1 = mla_w_in[0]
    win = jnp.concatenate([w_in1, w_in1[:, MLA_KV_LORA:Q_OFF][:, perm]], axis=1).astype(BF16)
    ukv = mla_w_ukv[0].reshape(MLA_KV_LORA, MLA_HEADS, MLA_NOPE + MLA_V)
    wkn = ukv[:, :, :MLA_NOPE].reshape(MLA_KV_LORA, MLA_HEADS * MLA_NOPE).astype(BF16)
    wkv = ukv[:, :, MLA_NOPE:].reshape(MLA_KV_LORA, MLA_HEADS * MLA_V).astype(BF16)
    uq = mla_w_uq[0].reshape(MLA_Q_LORA, MLA_HEADS, MLA_QK)
    wqn = uq[:, :, :MLA_NOPE].reshape(MLA_Q_LORA, MLA_HEADS * MLA_NOPE).astype(BF16)
    wqr = uq[:, :, MLA_NOPE:].reshape(MLA_Q_LORA, MLA_HEADS * MLA_ROPE).astype(BF16)
    wqrs = uq[:, :, MLA_NOPE:][:, :, perm].reshape(MLA_Q_LORA, MLA_HEADS * MLA_ROPE).astype(BF16)
    cosk, sink = _rope_tables()
    cosq = jnp.tile(cosk, (1, MLA_HEADS))
    sinq = jnp.tile(sink, (1, MLA_HEADS))

    q, k, v = _mla_proj(xs, mod[1], win, mla_kv_norm[0].reshape(1, MLA_KV_LORA), mla_q_norm[0].reshape(1, MLA_Q_LORA),
                        wkn, wkv, wqn, wqr, wqrs, cosk, sink, cosq, sinq)
    att = _attn(q, k, v)
    wr1, br1 = _router_params(moe_w_group[1], moe_b_group[1], moe_w_expert[1], moe_b_expert[1])
    x1, h2, route = _mix([att], [mla_w_out[0].astype(BF16)], xs, mod[1],
                         ln_g[1, 0].reshape(1, D), ln_b[1, 0].reshape(1, D), wr1, br1, False)
    return _moe(h2, route, x1, mod[1], ln_g[1, 1].reshape(1, D), ln_b[1, 1].reshape(1, D),
                moe_w_gate, moe_w_up, moe_w_down, 1, False)
```

```python
import functools
import math

import jax
import jax.numpy as jnp
from jax import lax
from jax.experimental import pallas as pl
from jax.experimental.pallas import tpu as pltpu

F32 = jnp.float32
BF16 = jnp.bfloat16
I32 = jnp.int32

D = 1024
NB = 8
SEQ = 2048
CTX = 256
T = SEQ + CTX
DEPTH = 2
N_MOD = 6
ALPHA = (2 * DEPTH) ** 0.25
LN_EPS = 1e-5
RMS_EPS = 1e-6

POOL_WINDOWS = (2, 4, 8, 16)
POOL_WIDTH = 256
POOL_GROUP_DIM = 64
GDN_HEADS = 6
GDN_DIM = 128
GDN_WIDTH = GDN_HEADS * GDN_DIM
GDN_CONV = 4
CHUNK = 64

MLA_HEADS = 8
MLA_NOPE = 128
MLA_ROPE = 64
MLA_V = 128
MLA_QK = MLA_NOPE + MLA_ROPE
MLA_Q_LORA = 384
MLA_KV_LORA = 256
GRID_W = 64
ROPE_BASE = 10000.0
ROPE_F = MLA_ROPE // 4

MOE_GROUPS = 4
MOE_PER_GROUP = 8
MOE_EXPERTS = 32
MOE_FF = 512

TT = CTX
NT = T // TT
NTL = SEQ // TT
TM = 256
FFN_DMA_GROUPS = 8
FFN_BUFS = 4
PB = 512
HB = 2
NG = GDN_HEADS // HB
NC = T // CHUNK
NCC = CTX // CHUNK
TQ = 256
LANES = 128
VMEM_LIMIT = 56 * 1024 * 1024


def _dot(a, b):
    return jnp.dot(a, b, preferred_element_type=F32)


def _dot_nt(a, b):
    return lax.dot_general(a, b, (((1,), (1,)), ((), ())), preferred_element_type=F32)


def _dot_tn(a, b):
    return lax.dot_general(a, b, (((0,), (0,)), ((), ())), preferred_element_type=F32)


def _split2(x):
    hi = x.astype(BF16)
    lo = (x - hi.astype(F32)).astype(BF16)
    return hi, lo


def _dot3(a, b):
    ah, al = _split2(a)
    bh, bl = _split2(b)
    return _dot(ah, bh) + (_dot(ah, bl) + _dot(al, bh))


def _silu(x):
    return x * jax.nn.sigmoid(x)


def _softplus(x):
    return jnp.maximum(x, 0.0) + jnp.log1p(jnp.exp(-jnp.abs(x)))


def _layer_norm(v, g, b):
    mu = jnp.mean(v, axis=-1, keepdims=True)
    c = v - mu
    var = jnp.mean(c * c, axis=-1, keepdims=True)
    return c * lax.rsqrt(var + LN_EPS) * g + b


def _mod_slice(mod_ref, row, k):
    return mod_ref[pl.ds(row, 1), k * D:(k + 1) * D]


NBLK = D // LANES


def _to_token_tiles(y):
    return jnp.transpose(jnp.stack([y[:, s * LANES:(s + 1) * LANES] for s in range(NBLK)], axis=0), (1, 0, 2))


def _from_token_tiles(x3):
    xt = jnp.transpose(x3, (1, 0, 2))
    return jnp.concatenate([xt[s] for s in range(NBLK)], axis=1)


def _cparams(sem, vmem=VMEM_LIMIT):
    return pltpu.CompilerParams(dimension_semantics=sem, vmem_limit_bytes=vmem)


def _ada_kernel(cv_ref, w_ref, b_ref, o_ref):
    s = _silu(cv_ref[...])
    o_ref[0] = _dot3(s, w_ref[0]) + b_ref[0]


def _ada(cv, ada_w, ada_b):
    nblk = N_MOD
    return pl.pallas_call(
        _ada_kernel,
        out_shape=jax.ShapeDtypeStruct((DEPTH, 16, N_MOD * D), F32),
        grid=(DEPTH, nblk),
        in_specs=[
            pl.BlockSpec((16, D), lambda l, j: (0, 0)),
            pl.BlockSpec((1, D, D), lambda l, j: (l, 0, j)),
            pl.BlockSpec((1, 1, D), lambda l, j: (l, 0, j)),
        ],
        out_specs=pl.BlockSpec((1, 16, D), lambda l, j: (l, 0, j)),
        compiler_params=_cparams(("arbitrary", "arbitrary")),
        name="ada_mod",
    )(cv, ada_w, ada_b.reshape(DEPTH, 1, N_MOD * D))


W_MAIN = POOL_WIDTH + 4 * GDN_WIDTH
W_AB = NG * LANES


def _inproj0_kernel(c_ref, x_ref, mod_ref, w_ref, wab_ref, pool_ref, main_ref, ab_ref):
    b = pl.program_id(0)
    t = pl.program_id(1)
    row = jnp.where(t == 0, NB, b)
    xt = jnp.where(t == 0, c_ref[0], x_ref[0])
    h = xt * (1.0 + _mod_slice(mod_ref, row, 1)) + _mod_slice(mod_ref, row, 0)
    p = _dot(h.astype(BF16), w_ref[...])
    pool_ref[0] = p[:, :POOL_WIDTH]
    main_ref[0] = p[:, POOL_WIDTH:]
    ab_ref[0] = _dot3(h, wab_ref[...])


def _inproj0(ctx, x, mod, w_main, w_ab):
    return pl.pallas_call(
        _inproj0_kernel,
        out_shape=(
            jax.ShapeDtypeStruct((NB, T, POOL_WIDTH), F32),
            jax.ShapeDtypeStruct((NB, T, 4 * GDN_WIDTH), F32),
            jax.ShapeDtypeStruct((NB, T, W_AB), F32),
        ),
        grid=(NB, NT),
        in_specs=[
            pl.BlockSpec((1, CTX, D), lambda b, t: (b, 0, 0)),
            pl.BlockSpec((1, TT, D), lambda b, t: (b, jnp.maximum(t - 1, 0), 0)),
            pl.BlockSpec((16, N_MOD * D), lambda b, t: (0, 0)),
            pl.BlockSpec((D, W_MAIN), lambda b, t: (0, 0)),
            pl.BlockSpec((D, W_AB), lambda b, t: (0, 0)),
        ],
        out_specs=(
            pl.BlockSpec((1, TT, POOL_WIDTH), lambda b, t: (b, t, 0)),
            pl.BlockSpec((1, TT, 4 * GDN_WIDTH), lambda b, t: (b, t, 0)),
            pl.BlockSpec((1, TT, W_AB), lambda b, t: (b, t, 0)),
        ),
        compiler_params=_cparams(("arbitrary", "arbitrary")),
        name="inproj0",
    )(ctx, x, mod, w_main, w_ab)


PAD_GAP = 16
PAD_CTX = PAD_GAP
PAD_LAT = PAD_CTX + CTX + 2 * PAD_GAP
PAD_ROWS = PAD_LAT + SEQ + PAD_GAP


def _fill_padded(pad_ref, src):
    w = pad_ref.shape[1]
    pad_ref[0:PAD_CTX, :] = jnp.zeros((PAD_CTX, w), F32)
    pad_ref[PAD_CTX + CTX:PAD_LAT, :] = jnp.zeros((2 * PAD_GAP, w), F32)
    pad_ref[PAD_LAT + SEQ:PAD_ROWS, :] = jnp.zeros((PAD_GAP, w), F32)
    pad_ref[PAD_CTX:PAD_CTX + CTX, :] = src(0, CTX)
    pad_ref[PAD_LAT:PAD_LAT + SEQ, :] = src(CTX, SEQ)


def _tile_pad_row(ti):
    return PAD_CTX if ti == 0 else PAD_LAT + (ti - 1) * TT


def _pool_kernel(u_ref, wbd_ref, scale_ref, o_ref, pad_ref):
    _fill_padded(pad_ref, lambda s, n: u_ref[0, s:s + n, :])
    lane = lax.broadcasted_iota(I32, (1, POOL_WIDTH), 1)
    grp = lane // POOL_GROUP_DIM
    win = jnp.zeros((1, POOL_WIDTH), I32)
    for g, w in enumerate(POOL_WINDOWS):
        win = jnp.where(grp == g, w, win)
    left = win // 2
    right = win - 1 - left
    for ti in range(NT):
        seg_len = CTX if ti == 0 else SEQ
        seg_t0 = 0 if ti == 0 else (ti - 1) * TT
        prow = _tile_pad_row(ti)
        tpos = seg_t0 + lax.broadcasted_iota(I32, (TT, 1), 0)
        acc = jnp.zeros((TT, POOL_WIDTH), F32)
        for j in range(-max(POOL_WINDOWS) // 2, max(POOL_WINDOWS) // 2):
            inwin = (j >= -left) & (j <= right)
            acc = acc + jnp.where(inwin, pad_ref[pl.ds(prow + j, TT), :], 0.0)
        cnt = jnp.minimum(tpos + right + 1, seg_len) - jnp.maximum(tpos - left, 0)
        dlt = acc / cnt.astype(F32) - pad_ref[pl.ds(prow, TT), :]
        y = _dot(dlt.astype(BF16), wbd_ref[...]) * scale_ref[...]
        o_ref[0, ti * TT:(ti + 1) * TT, :] = y.astype(BF16)


def _pool(pool_u, wbd, scale):
    return pl.pallas_call(
        _pool_kernel,
        out_shape=jax.ShapeDtypeStruct((NB, T, POOL_WIDTH), BF16),
        grid=(NB,),
        in_specs=[
            pl.BlockSpec((1, T, POOL_WIDTH), lambda b: (b, 0, 0)),
            pl.BlockSpec((POOL_WIDTH, POOL_WIDTH), lambda b: (0, 0)),
            pl.BlockSpec((1, POOL_WIDTH), lambda b: (0, 0)),
        ],
        out_specs=pl.BlockSpec((1, T, POOL_WIDTH), lambda b: (b, 0, 0)),
        scratch_shapes=[pltpu.VMEM((PAD_ROWS, POOL_WIDTH), F32)],
        compiler_params=_cparams(("arbitrary",)),
        name="pool",
    )(pool_u, wbd, scale)


HW = HB * GDN_DIM


NCH = 2 * HB
PW = NCH * CHUNK
GCH = 4


def _block_diag(xp, blk_masks):
    return jnp.concatenate([jnp.where(m, xp, 0.0) for m in blk_masks], axis=0).astype(BF16)


def _gdn_kernel(q_ref, k_ref, v_ref, z_ref, cwq_ref, cwk_ref, cwv_ref, ab_ref, alog_ref, dtb_ref, nw_ref,
                y_ref, pad_ref, qn_ref, kn_ref, vv_ref, m_ref, b_ref, qt_ref, o_ref, egl_ref, s_ref):
    def conv(x_ref, cw_ref, dst_ref, l2, scale):
        _fill_padded(pad_ref, lambda s, n: x_ref[0, s:s + n, :])
        cw = cw_ref[...]
        for ti in range(NT):
            prow = _tile_pad_row(ti)
            acc = jnp.zeros((TT, HW), F32)
            for j in range(GDN_CONV):
                acc = acc + pad_ref[pl.ds(prow - 2 + j, TT), :] * cw[j:j + 1, :]
            y = _silu(acc)
            if l2:
                for hh in range(HB):
                    yh = y[:, hh * GDN_DIM:(hh + 1) * GDN_DIM]
                    yh = yh * lax.rsqrt(jnp.sum(yh * yh, axis=-1, keepdims=True) + RMS_EPS)
                    dst_ref[ti * TT:(ti + 1) * TT, hh * GDN_DIM:(hh + 1) * GDN_DIM] = yh * scale
            else:
                dst_ref[ti * TT:(ti + 1) * TT, :] = y

    conv(q_ref, cwq_ref, qn_ref, True, GDN_DIM ** -0.5)
    conv(k_ref, cwk_ref, kn_ref, True, 1.0)
    conv(v_ref, cwv_ref, vv_ref, False, 1.0)

    ri = lax.broadcasted_iota(I32, (CHUNK, CHUNK), 0)
    ci = lax.broadcasted_iota(I32, (CHUNK, CHUNK), 1)
    tri2 = jnp.concatenate([(ci <= ri).astype(BF16), (ci >= ri).astype(BF16)], axis=0)
    rowp = lax.broadcasted_iota(I32, (CHUNK, PW), 0)
    lanep = lax.broadcasted_iota(I32, (CHUNK, PW), 1)
    blk = lanep // CHUNK
    colp = lanep - blk * CHUNK
    is_fwd = blk < HB
    ahead = jnp.where(is_fwd, rowp - colp, colp - rowp)
    incl_p = ahead >= 0
    strict_p = ahead > 0
    eye_p = (colp == rowp).astype(F32)
    eye_d = (lax.broadcasted_iota(I32, (GDN_DIM, GDN_DIM), 0)
             == lax.broadcasted_iota(I32, (GDN_DIM, GDN_DIM), 1)).astype(BF16)
    blk_masks = [blk == i for i in range(NCH)]
    alog = alog_ref[0]
    dtb = dtb_ref[0]

    def split3(v):
        h1 = v.astype(BF16)
        rem = v - h1.astype(F32)
        h2 = rem.astype(BF16)
        return h1, h2, (rem - h2.astype(F32)).astype(BF16)

    def phase1(it, carry):
        cs = [it * GCH + j for j in range(GCH)]
        r0 = [pl.multiple_of(c * CHUNK, CHUNK) for c in cs]
        m0 = [pl.multiple_of(c * GDN_DIM, GDN_DIM) for c in cs]
        e0 = [pl.multiple_of(c * 8, 8) for c in cs]
        G = range(GCH)
        abt = [ab_ref[0, pl.ds(r0[j], CHUNK), :] for j in G]
        g_all = [-jnp.exp(alog) * _softplus(abt[j] + dtb) for j in G]
        beta_all = [jax.nn.sigmoid(abt[j]) for j in G]
        g3 = [split3(g_all[j]) for j in G]
        gcs = [_dot(tri2, g3[j][0]) + (_dot(tri2, g3[j][1]) + _dot(tri2, g3[j][2])) for j in G]
        gb3 = [split3(jnp.where(strict_p, jnp.concatenate(
            [jnp.broadcast_to(g_all[j][:, i:i + 1], (CHUNK, CHUNK)) for i in range(NCH)], axis=1), 0.0)) for j in G]
        dif2 = [_dot(tri2, gb3[j][0]) + (_dot(tri2, gb3[j][1]) + _dot(tri2, gb3[j][2])) for j in G]
        kh = [[kn_ref[pl.ds(r0[j], CHUNK), hh * GDN_DIM:(hh + 1) * GDN_DIM] for hh in range(HB)] for j in G]
        qh = [[qn_ref[pl.ds(r0[j], CHUNK), hh * GDN_DIM:(hh + 1) * GDN_DIM] for hh in range(HB)] for j in G]
        vh = [[vv_ref[pl.ds(r0[j], CHUNK), hh * GDN_DIM:(hh + 1) * GDN_DIM] for hh in range(HB)] for j in G]
        gram = [[_dot_nt(jnp.concatenate([kh[j][hh].astype(BF16), qh[j][hh].astype(BF16), eye_d], axis=0),
                         kh[j][hh].astype(BF16)) for hh in range(HB)] for j in G]
        gcol1, bcol1, glast, egc1, kdsc = [], [], [], [], []
        for j in G:
            gcol1.append([]); bcol1.append([]); glast.append([]); egc1.append([]); kdsc.append([])
            for i in range(NCH):
                d = i // HB
                gc = gcs[j][d * CHUNK:(d + 1) * CHUNK]
                gcol1[j].append(gc[:, i:i + 1])
                bcol1[j].append(beta_all[j][:, NCH + i:NCH + i + 1])
                glast[j].append(gc[CHUNK - 1:CHUNK, i:i + 1] if d == 0 else gc[0:1, i:i + 1])
                egc1[j].append(jnp.exp(gcol1[j][i]))
                kdsc[j].append(jnp.exp(glast[j][i] - gcol1[j][i]))
                egl_ref[d, i % HB, pl.ds(e0[j], 8), :] = jnp.broadcast_to(jnp.exp(glast[j][i]), (8, LANES))
        lm, a_p = [], []
        for j in G:
            bcol_p = jnp.concatenate([jnp.broadcast_to(b, (CHUNK, CHUNK)) for b in bcol1[j]], axis=1)
            diff = jnp.where(is_fwd, dif2[j][:CHUNK], dif2[j][CHUNK:])
            dec = jnp.where(incl_p, jnp.exp(jnp.where(incl_p, diff, 0.0)), 0.0)
            kk_p = jnp.concatenate([gram[j][i % HB][:CHUNK] for i in range(NCH)], axis=1)
            qk_p = jnp.concatenate([gram[j][i % HB][CHUNK:2 * CHUNK] for i in range(NCH)], axis=1)
            lm.append(jnp.where(strict_p, bcol_p * kk_p * dec, 0.0))
            a_p.append(qk_p * dec)
        x = [eye_p - lm[j] for j in G]
        p = [_dot(lm[j].astype(BF16), _block_diag(lm[j], blk_masks)) for j in G]
        for _ in range(4):
            r = [_dot(jnp.concatenate([x[j], p[j]], axis=0).astype(BF16), _block_diag(p[j], blk_masks)) for j in G]
            x = [x[j] + r[j][:CHUNK] for j in G]
            p = [r[j][CHUNK:] for j in G]
        r = [_dot(x[j].astype(BF16), _block_diag(p[j], blk_masks)) for j in G]
        x = [x[j] + r[j] for j in G]
        rhs = [jnp.concatenate(
            [jnp.concatenate([vh[j][i % HB] * bcol1[j][i], kh[j][i % HB] * (bcol1[j][i] * egc1[j][i])], axis=1)
             for i in range(NCH)], axis=0).astype(BF16) for j in G]
        uw = [_dot(_block_diag(x[j], blk_masks), rhs[j]) for j in G]
        ao = [_dot(_block_diag(a_p[j], blk_masks), uw[j].astype(BF16)) for j in G]
        mb = [[_dot(gram[j][i % HB][2 * CHUNK:].astype(BF16),
                    (uw[j][i * CHUNK:(i + 1) * CHUNK] * kdsc[j][i]).astype(BF16)) for i in range(NCH)] for j in G]
        for j in G:
            for hh in range(HB):
                hs = slice(hh * GDN_DIM, (hh + 1) * GDN_DIM)
                o0 = jnp.zeros((CHUNK, GDN_DIM), F32)
                for d in range(2):
                    i = d * HB + hh
                    rs = slice(i * CHUNK, (i + 1) * CHUNK)
                    b_ref[d, hh, pl.ds(m0[j], GDN_DIM), :] = mb[j][i][:, :GDN_DIM].astype(BF16)
                    m_ref[d, hh, pl.ds(m0[j], GDN_DIM), :] = mb[j][i][:, GDN_DIM:].astype(BF16)
                    qt_ref[d, pl.ds(r0[j], CHUNK), hs] = (qh[j][hh] * egc1[j][i] - ao[j][rs, GDN_DIM:]).astype(BF16)
                    o0 = o0 + ao[j][rs, :GDN_DIM]
                o_ref[pl.ds(r0[j], CHUNK), hs] = o0
        return carry

    lax.fori_loop(0, NC // GCH, phase1, 0)

    s_ref[...] = jnp.zeros(s_ref.shape, F32)

    def phase2(step, carry):
        cb = jnp.where(step < NCC, NCC - 1 - step, NC + NCC - 1 - step)
        chains = [(d, hh, c) for d, c in ((0, step), (1, cb)) for hh in range(HB)]
        rows = [pl.ds(pl.multiple_of(c * CHUNK, CHUNK), CHUNK) for _, _, c in chains]
        mrows = [pl.ds(pl.multiple_of(c * GDN_DIM, GDN_DIM), GDN_DIM) for _, _, c in chains]
        hs = [slice(hh * GDN_DIM, (hh + 1) * GDN_DIM) for _, hh, _ in chains]
        n = range(len(chains))
        st = [s_ref[chains[i][0], chains[i][1]] for i in n]
        stb = [st[i].astype(BF16) for i in n]
        ms = [_dot(m_ref[chains[i][0], chains[i][1], mrows[i], :], stb[i]) for i in n]
        oq = [_dot(qt_ref[chains[i][0], rows[i], hs[i]], stb[i]) for i in n]
        for i in n:
            d, hh, c = chains[i]
            egl = egl_ref[d, hh, pl.ds(pl.multiple_of(c * 8, 8), 1), :]
            s_ref[d, hh] = st[i] * egl - ms[i] + b_ref[d, hh, mrows[i], :].astype(F32)
            o_ref[rows[i], hs[i]] += oq[i]
        return carry

    lax.fori_loop(0, NC, phase2, 0)

    nw = nw_ref[...]
    for ti in range(NT):
        rs = slice(ti * TT, (ti + 1) * TT)
        o = o_ref[rs, :]
        zz = _silu(z_ref[0, rs, :])
        for hh in range(HB):
            hs = slice(hh * GDN_DIM, (hh + 1) * GDN_DIM)
            oh = o[:, hs]
            oh = oh * lax.rsqrt(jnp.mean(oh * oh, axis=-1, keepdims=True) + RMS_EPS) * nw
            y_ref[0, rs, hs] = (oh * zz[:, hs]).astype(BF16)


def _gdn(p_main, conv_w, ab, alog_g, dtb_g, norm_w):
    nhb = GDN_WIDTH // HW
    blk = lambda off: pl.BlockSpec((1, T, HW), lambda b, g: (b, 0, off * nhb + g))
    cblk = lambda off: pl.BlockSpec((GDN_CONV, HW), lambda b, g: (0, off * nhb + g))
    return pl.pallas_call(
        _gdn_kernel,
        out_shape=jax.ShapeDtypeStruct((NB, T, GDN_WIDTH), BF16),
        grid=(NB, NG),
        in_specs=[
            blk(0), blk(1), blk(2), blk(3),
            cblk(0), cblk(1), cblk(2),
            pl.BlockSpec((1, T, LANES), lambda b, g: (b, 0, g)),
            pl.BlockSpec((1, 1, LANES), lambda b, g: (g, 0, 0)),
            pl.BlockSpec((1, 1, LANES), lambda b, g: (g, 0, 0)),
            pl.BlockSpec((1, GDN_DIM), lambda b, g: (0, 0)),
        ],
        out_specs=pl.BlockSpec((1, T, HW), lambda b, g: (b, 0, g)),
        scratch_shapes=[
            pltpu.VMEM((PAD_ROWS, HW), F32),
            pltpu.VMEM((T, HW), F32),
            pltpu.VMEM((T, HW), F32),
            pltpu.VMEM((T, HW), F32),
            pltpu.VMEM((2, HB, NC * GDN_DIM, GDN_DIM), BF16),
            pltpu.VMEM((2, HB, NC * GDN_DIM, GDN_DIM), BF16),
            pltpu.VMEM((2, T, HW), BF16),
            pltpu.VMEM((T, HW), F32),
            pltpu.VMEM((2, HB, NC * 8, LANES), F32),
            pltpu.VMEM((2, HB, GDN_DIM, GDN_DIM), F32),
        ],
        compiler_params=_cparams(("arbitrary", "arbitrary")),
        name="gdn",
    )(p_main, p_main, p_main, p_main, conv_w, conv_w, conv_w, ab, alog_g, dtb_g, norm_w)


NEG = -1e30


def _mix_kernel(n_act, ctx_tile0, *refs):
    act_refs = refs[:n_act]
    w_refs = refs[n_act:2 * n_act]
    rest = refs[2 * n_act:]
    b = pl.program_id(0)
    t = pl.program_id(1)
    if ctx_tile0:
        c_ref, x_ref, mod_ref, lng_ref, lnb_ref, wr_ref, br_ref, x1_ref, h2_ref, route_ref = rest
        row = jnp.where(t == 0, NB, b)
        xt = jnp.where(t == 0, c_ref[0], x_ref[0])
    else:
        x_ref, mod_ref, lng_ref, lnb_ref, wr_ref, br_ref, x1_ref, h2_ref, route_ref = rest
        row = b
        xt = x_ref[0]
    y = _dot(act_refs[0][0], w_refs[0][...])
    for a_ref, w_ref in zip(act_refs[1:], w_refs[1:]):
        y = y + _dot(a_ref[0], w_ref[...])
    x1 = _layer_norm(ALPHA * xt + _mod_slice(mod_ref, row, 2) * y, lng_ref[...], lnb_ref[...])
    x1_ref[0] = x1
    h2 = x1 * (1.0 + _mod_slice(mod_ref, row, 4)) + _mod_slice(mod_ref, row, 3)
    h2_ref[...] = _to_token_tiles(h2)
    logits = _dot3(h2, wr_ref[...]) + br_ref[...]
    lane = lax.broadcasted_iota(I32, (TT, LANES), 1)
    gl = jnp.where(lane < MOE_GROUPS, logits, NEG)
    gm = jnp.max(gl, axis=-1, keepdims=True)
    gsum = jnp.sum(jnp.exp(gl - gm), axis=-1, keepdims=True)
    g_val = 1.0 / gsum
    g_idx = jnp.min(jnp.where(gl == gm, lane, LANES), axis=-1, keepdims=True)
    sel = (lane >= MOE_GROUPS) & (lane < MOE_GROUPS + MOE_EXPERTS) & (((lane - MOE_GROUPS) // MOE_PER_GROUP) == g_idx)
    el = jnp.where(sel, logits, NEG)
    m1 = jnp.max(el, axis=-1, keepdims=True)
    i1 = jnp.min(jnp.where(el == m1, lane, LANES), axis=-1, keepdims=True)
    el2 = jnp.where(lane == i1, NEG, el)
    m2 = jnp.max(el2, axis=-1, keepdims=True)
    i2 = jnp.min(jnp.where(el2 == m2, lane, LANES), axis=-1, keepdims=True)
    ex2 = jnp.exp(m2 - m1)
    w1 = g_val / (1.0 + ex2)
    w2 = w1 * ex2
    e1 = (i1 - MOE_GROUPS).astype(F32)
    e2 = (i2 - MOE_GROUPS).astype(F32)
    route_ref[...] = jnp.where(lane == 0, e1, jnp.where(lane == 1, e2, jnp.where(lane == 2, w1, jnp.where(lane == 3, w2, 0.0))))


def _mix(acts, ws, streams, mod, ln_g, ln_b, wr, br, ctx_tile0):
    ntl = NT if ctx_tile0 else NTL
    if ctx_tile0:
        stream_specs = [pl.BlockSpec((1, CTX, D), lambda b, t: (b, 0, 0)),
                        pl.BlockSpec((1, TT, D), lambda b, t: (b, jnp.maximum(t - 1, 0), 0))]
    else:
        stream_specs = [pl.BlockSpec((1, TT, D), lambda b, t: (b, t + 1, 0))]
    n_tok = NB * ntl * TT
    act_specs = [pl.BlockSpec((1, TT, a.shape[2]), lambda b, t: (b, t, 0)) for a in acts]
    w_specs = [pl.BlockSpec(w.shape, lambda b, t: (0, 0)) for w in ws]
    return pl.pallas_call(
        functools.partial(_mix_kernel, len(acts), ctx_tile0),
        out_shape=(
            jax.ShapeDtypeStruct((NB, ntl * TT, D), F32),
            jax.ShapeDtypeStruct((n_tok, NBLK, LANES), F32),
            jax.ShapeDtypeStruct((n_tok, LANES), F32),
        ),
        grid=(NB, ntl),
        in_specs=act_specs + w_specs + stream_specs + [
            pl.BlockSpec((16, N_MOD * D), lambda b, t: (0, 0)),
            pl.BlockSpec((1, D), lambda b, t: (0, 0)),
            pl.BlockSpec((1, D), lambda b, t: (0, 0)),
            pl.BlockSpec((D, LANES), lambda b, t: (0, 0)),
            pl.BlockSpec((1, LANES), lambda b, t: (0, 0)),
        ],
        out_specs=(
            pl.BlockSpec((1, TT, D), lambda b, t: (b, t, 0)),
            pl.BlockSpec((TT, NBLK, LANES), lambda b, t: (b * ntl + t, 0, 0)),
            pl.BlockSpec((TT, LANES), lambda b, t: (b * ntl + t, 0)),
        ),
        compiler_params=_cparams(("arbitrary", "arbitrary")),
        name="mix_ctx" if ctx_tile0 else "mix_lat",
    )(*acts, *ws, *streams, mod, ln_g, ln_b, wr, br)


def _plan_kernel(route_ref, er_ref, cnt_ref, carry_ref):
    i = pl.program_id(0)

    @pl.when(i == 0)
    def _():
        carry_ref[...] = jnp.zeros(carry_ref.shape, F32)

    r = route_ref[...]
    lane = lax.broadcasted_iota(I32, (PB, LANES), 1)
    lanef = lane.astype(F32)
    ri = lax.broadcasted_iota(I32, (PB, PB), 0)
    ci = lax.broadcasted_iota(I32, (PB, PB), 1)
    before = (ci < ri).astype(BF16)
    oh1 = jnp.where(lanef == r[:, 0:1], 1.0, 0.0)
    oh2 = jnp.where(lanef == r[:, 1:2], 1.0, 0.0)
    c0 = carry_ref[0:1, :]
    tot1 = jnp.sum(oh1, axis=0, keepdims=True)
    tot2 = jnp.sum(oh2, axis=0, keepdims=True)
    r1 = _dot(before, oh1.astype(BF16)) + c0
    r2 = _dot(before, oh2.astype(BF16)) + (c0 + tot1)
    rank1 = jnp.sum(r1 * oh1, axis=-1, keepdims=True)
    rank2 = jnp.sum(r2 * oh2, axis=-1, keepdims=True)
    er = jnp.where(lane < 2, r, jnp.where(lane == 2, rank1, jnp.where(lane == 3, rank2, 0.0)))
    er_ref[...] = er.T[0:8, :]
    cnew = jnp.broadcast_to(c0 + tot1 + tot2, carry_ref.shape)
    carry_ref[...] = cnew
    cnt_ref[...] = cnew


def _plan(route):
    n_tok = route.shape[0]
    return pl.pallas_call(
        _plan_kernel,
        out_shape=(jax.ShapeDtypeStruct((8, n_tok), F32), jax.ShapeDtypeStruct((8, LANES), F32)),
        grid=(n_tok // PB,),
        in_specs=[pl.BlockSpec((PB, LANES), lambda i: (i, 0))],
        out_specs=(pl.BlockSpec((8, PB), lambda i: (0, i)), pl.BlockSpec((8, LANES), lambda i: (0, 0))),
        scratch_shapes=[pltpu.VMEM((8, LANES), F32)],
        compiler_params=_cparams(("arbitrary",)),
        name="moe_plan",
    )(route)


INV_UNROLL = 16


def _invmap_kernel(pos_ref, src_ref):
    n_rows = src_ref.shape[0]
    n_tok = pos_ref.shape[0] // 2

    def clear(g, c):
        for u in range(INV_UNROLL):
            src_ref[g * INV_UNROLL + u] = 0
        return c

    lax.fori_loop(0, n_rows // INV_UNROLL, clear, 0)

    def put(g, c):
        for u in range(INV_UNROLL):
            t = g * INV_UNROLL + u
            src_ref[pos_ref[t]] = t
            src_ref[pos_ref[n_tok + t]] = t
        return c

    lax.fori_loop(0, n_tok // INV_UNROLL, put, 0)


def _invmap(pos_flat, n_rows):
    return pl.pallas_call(
        _invmap_kernel,
        out_shape=jax.ShapeDtypeStruct((n_rows,), I32),
        in_specs=[pl.BlockSpec(memory_space=pltpu.SMEM)],
        out_specs=pl.BlockSpec(memory_space=pltpu.SMEM),
        name="moe_invmap",
    )(pos_flat)


def _ffn_kernel(te_ref, nu_ref, src_ref, h_hbm, wg_ref, wu_ref, wd_ref, y_ref, buf, wgb, wub, wdb, sem):
    i = pl.program_id(0)
    nu = nu_ref[0]

    def issue(tile, slot, lo, hi):
        base = tile * TM
        for r in range(lo, hi):
            pltpu.make_async_copy(h_hbm.at[src_ref[base + r]], buf.at[slot, r], sem.at[slot]).start(priority=r % 2)

    def ffn_tile(slot, next_tile, next_slot):
        per = TM // FFN_DMA_GROUPS
        sent = [0]

        def gather_some():
            if next_tile is not None and sent[0] < FFN_DMA_GROUPS:
                issue(next_tile, next_slot, sent[0] * per, (sent[0] + 1) * per)
                sent[0] += 1

        x = _from_token_tiles(buf[slot]).astype(BF16)
        hcol = MOE_FF // 2
        gather_some()
        parts = []
        for w in (wgb, wub):
            for c in range(2):
                parts.append(_dot(x, w[:, c * hcol:(c + 1) * hcol]))
                gather_some()
        a = jnp.concatenate(parts[0:2], axis=1)
        bb = jnp.concatenate(parts[2:4], axis=1)
        act = (_silu(a) * bb).astype(BF16)
        ys = []
        for c in range(D // hcol):
            ys.append(_dot(act, wdb[:, c * hcol:(c + 1) * hcol]))
            gather_some()
        assert next_tile is None or sent[0] == FFN_DMA_GROUPS
        y_ref[...] = _to_token_tiles(jnp.concatenate(ys, axis=1))

    @pl.when(i == 0)
    def _():
        issue(0, 0, 0, TM)

        for j in range(1, FFN_BUFS - 1):
            @pl.when(nu > j)
            def _():
                issue(j, j, 0, TM)

    @pl.when(i < nu)
    def _():
        slot = i % FFN_BUFS
        pltpu.make_async_copy(h_hbm.at[pl.ds(0, TM)], buf.at[slot], sem.at[slot]).wait()
        e = te_ref[i]
        ep = te_ref[jnp.maximum(i - 1, 0)]

        @pl.when((i == 0) | (e != ep))
        def _():
            wgb[...] = wg_ref[0, 0].astype(BF16)
            wub[...] = wu_ref[0, 0].astype(BF16)
            wdb[...] = wd_ref[0, 0].astype(BF16)

        ahead = i + (FFN_BUFS - 1)

        @pl.when(ahead < nu)
        def _():
            ffn_tile(slot, ahead, ahead % FFN_BUFS)

        @pl.when(ahead >= nu)
        def _():
            ffn_tile(slot, None, None)

    @pl.when(i >= nu)
    def _():
        y_ref[...] = jnp.zeros(y_ref.shape, F32)


def _ffn(tile_expert, n_used, src, h2, wg, wu, wd, layer):
    n_rows = src.shape[0]
    n_tiles = n_rows // TM
    wspec = lambda shape: pl.BlockSpec((1, 1) + shape, lambda i, te, nu, src: (layer, te[i], 0, 0))
    return pl.pallas_call(
        _ffn_kernel,
        out_shape=jax.ShapeDtypeStruct((n_rows, NBLK, LANES), F32),
        grid_spec=pltpu.PrefetchScalarGridSpec(
            num_scalar_prefetch=3,
            grid=(n_tiles,),
            in_specs=[
                pl.BlockSpec(memory_space=pl.ANY),
                wspec((D, MOE_FF)), wspec((D, MOE_FF)), wspec((MOE_FF, D)),
            ],
            out_specs=pl.BlockSpec((TM, NBLK, LANES), lambda i, te, nu, src: (i, 0, 0)),
            scratch_shapes=[
                pltpu.VMEM((FFN_BUFS, TM, NBLK, LANES), F32),
                pltpu.VMEM((D, MOE_FF), BF16),
                pltpu.VMEM((D, MOE_FF), BF16),
                pltpu.VMEM((MOE_FF, D), BF16),
                pltpu.SemaphoreType.DMA((FFN_BUFS,)),
            ],
        ),
        compiler_params=_cparams(("arbitrary",)),
        name="moe_ffn",
    )(tile_expert, n_used, src, h2, wg, wu, wd)


def _combine_kernel(ntl, ctx_tile0, pos_ref, y_hbm, x1_ref, route_ref, mod_ref, lng_ref, lnb_ref, o_ref, buf, sem):
    b = pl.program_id(0)
    t = pl.program_id(1)
    flat = b * ntl + t
    n_tiles = NB * ntl

    def issue(tile, slot):
        base = tile * TT
        for r in range(TT):
            for k in range(2):
                pltpu.make_async_copy(y_hbm.at[pos_ref[k * (n_tiles * TT) + base + r]], buf.at[slot, k * TT + r],
                                      sem.at[slot]).start(priority=k)

    @pl.when(flat == 0)
    def _():
        issue(0, 0)

    slot = flat % 2
    pltpu.make_async_copy(y_hbm.at[pl.ds(0, 2 * TT)], buf.at[slot], sem.at[slot]).wait()

    @pl.when(flat + 1 < n_tiles)
    def _():
        issue(flat + 1, 1 - slot)

    row = jnp.where(t == 0, NB, b) if ctx_tile0 else b
    r = route_ref[...]
    f = (r[:, 2:3] * _from_token_tiles(buf[slot, 0:TT]) + r[:, 3:4] * _from_token_tiles(buf[slot, TT:2 * TT]))
    o_ref[0] = _layer_norm(ALPHA * x1_ref[0] + _mod_slice(mod_ref, row, 5) * f, lng_ref[...], lnb_ref[...])


def _combine(pos_flat, ys, x1, route, mod, ln_g, ln_b, ctx_tile0):
    ntl = NT if ctx_tile0 else NTL
    return pl.pallas_call(
        functools.partial(_combine_kernel, ntl, ctx_tile0),
        out_shape=jax.ShapeDtypeStruct((NB, ntl * TT, D), F32),
        grid_spec=pltpu.PrefetchScalarGridSpec(
            num_scalar_prefetch=1,
            grid=(NB, ntl),
            in_specs=[
                pl.BlockSpec(memory_space=pl.ANY),
                pl.BlockSpec((1, TT, D), lambda b, t, pos: (b, t, 0)),
                pl.BlockSpec((TT, LANES), lambda b, t, pos: (b * ntl + t, 0)),
                pl.BlockSpec((16, N_MOD * D), lambda b, t, pos: (0, 0)),
                pl.BlockSpec((1, D), lambda b, t, pos: (0, 0)),
                pl.BlockSpec((1, D), lambda b, t, pos: (0, 0)),
            ],
            out_specs=pl.BlockSpec((1, TT, D), lambda b, t, pos: (b, t, 0)),
            scratch_shapes=[pltpu.VMEM((2, 2 * TT, NBLK, LANES), F32), pltpu.SemaphoreType.DMA((2,))],
        ),
        compiler_params=_cparams(("arbitrary", "arbitrary")),
        name="moe_combine_ctx" if ctx_tile0 else "moe_combine_lat",
    )(pos_flat, ys, x1, route, mod, ln_g, ln_b)


def _moe(h2, route, x1, mod, ln_g, ln_b, wg, wu, wd, layer, ctx_tile0):
    n_tok = h2.shape[0]
    n_tiles = (2 * n_tok) // TM + MOE_EXPERTS
    er, cnt = _plan(route)
    counts = cnt[0, :MOE_EXPERTS].astype(I32)
    tiles_e = (counts + TM - 1) // TM
    tile_end = jnp.cumsum(tiles_e)
    tile_start = tile_end - tiles_e
    n_used = tile_end[-1]
    tj = jnp.arange(n_tiles, dtype=I32)
    te = jnp.minimum(jnp.sum((tile_end[None, :] <= tj[:, None]).astype(I32), axis=1), MOE_EXPERTS - 1)
    te_last = jnp.max(jnp.where(tj < n_used, te, 0))
    tile_expert = jnp.where(tj < n_used, te, te_last)
    pos_flat = ((tile_start * TM)[er[0:2].astype(I32)] + er[2:4].astype(I32)).reshape(-1)
    src = _invmap(pos_flat, n_tiles * TM)
    ys = _ffn(tile_expert, n_used.reshape(1), src, h2, wg, wu, wd, layer)
    return _combine(pos_flat, ys, x1, route, mod, ln_g, ln_b, ctx_tile0)


W_IN1 = MLA_KV_LORA + MLA_ROPE + MLA_Q_LORA + MLA_ROPE
Q_OFF = MLA_KV_LORA + MLA_ROPE
KRS_OFF = Q_OFF + MLA_Q_LORA
SM_SCALE = MLA_QK ** -0.5


def _rms(v, w):
    return v * lax.rsqrt(jnp.mean(v * v, axis=-1, keepdims=True) + RMS_EPS) * w


def _mla_proj_kernel(x_ref, mod_ref, win_ref, kvn_ref, qnw_ref, wkn_ref, wkv_ref, wqn_ref, wqr_ref, wqrs_ref,
                     cosk_ref, sink_ref, cosq_ref, sinq_ref, q_out, k_out, v_out):
    b = pl.program_id(0)
    t = pl.program_id(1)
    row = jnp.where(t == 0, NB, b)
    h = x_ref[0] * (1.0 + _mod_slice(mod_ref, row, 1)) + _mod_slice(mod_ref, row, 0)
    p = _dot(h.astype(BF16), win_ref[...])
    ckv = _rms(p[:, :MLA_KV_LORA], kvn_ref[...]).astype(BF16)
    kn = _dot(ckv, wkn_ref[...])
    vv = _dot(ckv, wkv_ref[...])
    kr = p[:, MLA_KV_LORA:Q_OFF]
    krs = p[:, KRS_OFF:KRS_OFF + MLA_ROPE]
    kr = jnp.where(t > 0, kr * cosk_ref[...] + krs * sink_ref[...], kr).astype(BF16)
    ql = _rms(p[:, Q_OFF:KRS_OFF], qnw_ref[...]).astype(BF16)
    qn = _dot(ql, wqn_ref[...]) * SM_SCALE
    qr = (_dot(ql, wqr_ref[...]) * cosq_ref[...] + _dot(ql, wqrs_ref[...]) * sinq_ref[...]) * SM_SCALE
    for hd in range(MLA_HEADS):
        k_out[0, hd, :, 0:MLA_NOPE] = kn[:, hd * MLA_NOPE:(hd + 1) * MLA_NOPE].astype(BF16)
        k_out[0, hd, :, MLA_NOPE:MLA_QK] = kr
        v_out[0, hd] = vv[:, hd * MLA_V:(hd + 1) * MLA_V].T.astype(BF16)
        q_out[0, hd, :, 0:MLA_NOPE] = qn[:, hd * MLA_NOPE:(hd + 1) * MLA_NOPE].astype(BF16)
        q_out[0, hd, :, MLA_NOPE:MLA_QK] = qr[:, hd * MLA_ROPE:(hd + 1) * MLA_ROPE].astype(BF16)


def _mla_proj(xs, mod, win, kvn, qnw, wkn, wkv, wqn, wqr, wqrs, cosk, sink, cosq, sinq):
    full = lambda a: pl.BlockSpec(a.shape, lambda b, t: (0, 0))
    lat = lambda w: pl.BlockSpec((TT, w), lambda b, t: (jnp.maximum(t - 1, 0), 0))
    return pl.pallas_call(
        _mla_proj_kernel,
        out_shape=(
            jax.ShapeDtypeStruct((NB, MLA_HEADS, SEQ, MLA_QK), BF16),
            jax.ShapeDtypeStruct((NB, MLA_HEADS, T, MLA_QK), BF16),
            jax.ShapeDtypeStruct((NB, MLA_HEADS, MLA_V, T), BF16),
        ),
        grid=(NB, NT),
        in_specs=[
            pl.BlockSpec((1, TT, D), lambda b, t: (b, t, 0)),
            pl.BlockSpec((16, N_MOD * D), lambda b, t: (0, 0)),
            full(win), full(kvn), full(qnw), full(wkn), full(wkv), full(wqn), full(wqr), full(wqrs),
            lat(MLA_ROPE), lat(MLA_ROPE), lat(MLA_HEADS * MLA_ROPE), lat(MLA_HEADS * MLA_ROPE),
        ],
        out_specs=(
            pl.BlockSpec((1, MLA_HEADS, TT, MLA_QK), lambda b, t: (b, 0, jnp.maximum(t - 1, 0), 0)),
            pl.BlockSpec((1, MLA_HEADS, TT, MLA_QK), lambda b, t: (b, 0, t, 0)),
            pl.BlockSpec((1, MLA_HEADS, MLA_V, TT), lambda b, t: (b, 0, 0, t)),
        ),
        compiler_params=_cparams(("arbitrary", "arbitrary")),
        name="mla_proj",
    )(xs, mod, win, kvn, qnw, wkn, wkv, wqn, wqr, wqrs, cosk, sink, cosq, sinq)


def _attn_kernel(q_ref, k_ref, vt_ref, o_ref, s_ref):
    k = k_ref[0, 0]
    vt = vt_ref[0, 0]
    nq = SEQ // TQ

    def scores(j):
        s_ref[j % 2] = _dot_nt(k, q_ref[0, 0, j * TQ:(j + 1) * TQ, :])

    scores(0)
    for j in range(nq):
        if j + 1 < nq:
            scores(j + 1)
        s = s_ref[j % 2]
        m = jnp.max(s, axis=0, keepdims=True)
        p = jnp.exp(s - m)
        l = jnp.sum(p, axis=0, keepdims=True)
        ot = _dot(vt, p.astype(BF16)) / l
        o_ref[0, j * TQ:(j + 1) * TQ, :] = ot.T.astype(BF16)


def _attn(q, k, vt):
    return pl.pallas_call(
        _attn_kernel,
        out_shape=jax.ShapeDtypeStruct((NB, SEQ, MLA_HEADS * MLA_V), BF16),
        grid=(NB, MLA_HEADS),
        in_specs=[
            pl.BlockSpec((1, 1, SEQ, MLA_QK), lambda b, h: (b, h, 0, 0)),
            pl.BlockSpec((1, 1, T, MLA_QK), lambda b, h: (b, h, 0, 0)),
            pl.BlockSpec((1, 1, MLA_V, T), lambda b, h: (b, h, 0, 0)),
        ],
        out_specs=pl.BlockSpec((1, SEQ, MLA_V), lambda b, h: (b, 0, h)),
        scratch_shapes=[pltpu.VMEM((2, T, TQ), F32)],
        compiler_params=_cparams(("arbitrary", "arbitrary")),
        name="mla_attn",
    )(q, k, vt)


def _rope_tables():
    n = SEQ
    rowp = (jnp.arange(n) // GRID_W).astype(F32)
    colp = (jnp.arange(n) % GRID_W).astype(F32)
    inv_freq = ROPE_BASE ** (-jnp.arange(ROPE_F, dtype=F32) / ROPE_F)
    ar = rowp[:, None] * inv_freq
    ac = colp[:, None] * inv_freq
    cos = jnp.concatenate([jnp.cos(ar), jnp.cos(ar), jnp.cos(ac), jnp.cos(ac)], axis=1)
    sin = jnp.concatenate([-jnp.sin(ar), jnp.sin(ar), -jnp.sin(ac), jnp.sin(ac)], axis=1)
    return cos, sin


def _rope_swap_perm():
    f = ROPE_F
    return jnp.concatenate([jnp.arange(f, 2 * f), jnp.arange(0, f), jnp.arange(3 * f, 4 * f), jnp.arange(2 * f, 3 * f)])


def _router_params(w_group, b_group, w_expert, b_expert):
    pad = LANES - MOE_GROUPS - MOE_EXPERTS
    wr = jnp.concatenate([w_group, w_expert, jnp.zeros((D, pad), F32)], axis=1)
    br = jnp.concatenate([b_group, b_expert, jnp.zeros((pad,), F32)]).reshape(1, LANES)
    return wr, br


def kernel(x, c, ctx, c_ctx, ada_w, ada_b, ln_g, ln_b, ab_w_in, pool_w, pool_scale, gdn_conv_w, gdn_a_log, gdn_dt_bias, gdn_norm_w, ab_w_out, mla_w_in, mla_kv_norm, mla_w_ukv, mla_q_norm, mla_w_uq, mla_w_out, moe_w_group, moe_b_group, moe_w_expert, moe_b_expert, moe_w_gate, moe_w_up, moe_w_down):
    assert x.shape == (NB, SEQ, D) and ctx.shape == (NB, CTX, D)
    cv = jnp.concatenate([c, c_ctx[None, :], jnp.zeros((16 - NB - 1, D), F32)], axis=0)
    mod = _ada(cv, ada_w, ada_b)

    w_in = ab_w_in[0]
    w_main = w_in[:, :W_MAIN].astype(BF16)
    o_ab = W_MAIN
    ab_cols = []
    for g in range(NG):
        idx = [o_ab + kind * 2 * GDN_HEADS + d * GDN_HEADS + g * HB + hh
               for kind in range(2) for d in range(2) for hh in range(HB)]
        ab_cols.append(jnp.concatenate([w_in[:, jnp.array(idx)], jnp.zeros((D, LANES - len(idx)), F32)], axis=1))
    w_ab = jnp.concatenate(ab_cols, axis=1)

    def gate_rows(p):
        rows = []
        for g in range(NG):
            vals = jnp.stack([p[d, g * HB + hh] for d in range(2) for hh in range(HB)])
            rows.append(jnp.concatenate([vals, jnp.zeros((LANES - 2 * HB,), F32)]))
        return jnp.stack(rows).reshape(NG, 1, LANES)

    alog_g = gate_rows(gdn_a_log[0])
    dtb_g = gate_rows(gdn_dt_bias[0])
    wbd = jax.scipy.linalg.block_diag(*[pool_w[0, g] for g in range(len(POOL_WINDOWS))]).astype(BF16)

    pool_u, p_main, ab = _inproj0(ctx, x, mod[0], w_main, w_ab)
    pool_y = _pool(pool_u, wbd, pool_scale[0].reshape(1, POOL_WIDTH))
    gdn_y = _gdn(p_main, gdn_conv_w[0], ab, alog_g, dtb_g, gdn_norm_w[0].reshape(1, GDN_DIM))

    w_out0 = ab_w_out[0].astype(BF16)
    wr0, br0 = _router_params(moe_w_group[0], moe_b_group[0], moe_w_expert[0], moe_b_expert[0])
    x1, h2, route = _mix([pool_y, gdn_y], [w_out0[:POOL_WIDTH], w_out0[POOL_WIDTH:]], (ctx, x), mod[0],
                         ln_g[0, 0].reshape(1, D), ln_b[0, 0].reshape(1, D), wr0, br0, True)
    xs = _moe(h2, route, x1, mod[0], ln_g[0, 1].reshape(1, D), ln_b[0, 1].reshape(1, D),
              moe_w_gate, moe_w_up, moe_w_down, 0, True)

    perm = _rope_swap_perm()
    w_in1 = mla_w_in[0]
    win = jnp.concatenate([w_in1, w_in1[:, MLA_KV_LORA:Q_OFF][:, perm]], axis=1).astype(BF16)
    ukv = mla_w_ukv[0].reshape(MLA_KV_LORA, MLA_HEADS, MLA_NOPE + MLA_V)
    wkn = ukv[:, :, :MLA_NOPE].reshape(MLA_KV_LORA, MLA_HEADS * MLA_NOPE).astype(BF16)
    wkv = ukv[:, :, MLA_NOPE:].reshape(MLA_KV_LORA, MLA_HEADS * MLA_V).astype(BF16)
    uq = mla_w_uq[0].reshape(MLA_Q_LORA, MLA_HEADS, MLA_QK)
    wqn = uq[:, :, :MLA_NOPE].reshape(MLA_Q_LORA, MLA_HEADS * MLA_NOPE).astype(BF16)
    wqr = uq[:, :, MLA_NOPE:].reshape(MLA_Q_LORA, MLA_HEADS * MLA_ROPE).astype(BF16)
    wqrs = uq[:, :, MLA_NOPE:][:, :, perm].reshape(MLA_Q_LORA, MLA_HEADS * MLA_ROPE).astype(BF16)
    cosk, sink = _rope_tables()
    cosq = jnp.tile(cosk, (1, MLA_HEADS))
    sinq = jnp.tile(sink, (1, MLA_HEADS))

    q, k, v = _mla_proj(xs, mod[1], win, mla_kv_norm[0].reshape(1, MLA_KV_LORA), mla_q_norm[0].reshape(1, MLA_Q_LORA),
                        wkn, wkv, wqn, wqr, wqrs, cosk, sink, cosq, sinq)
    att = _attn(q, k, v)
    wr1, br1 = _router_params(moe_w_group[1], moe_b_group[1], moe_w_expert[1], moe_b_expert[1])
    x1, h2, route = _mix([att], [mla_w_out[0].astype(BF16)], (xs,), mod[1],
                         ln_g[1, 0].reshape(1, D), ln_b[1, 0].reshape(1, D), wr1, br1, False)
    return _moe(h2, route, x1, mod[1], ln_g[1, 1].reshape(1, D), ln_b[1, 1].reshape(1, D),
                moe_w_gate, moe_w_up, moe_w_down, 1, False)
```

```python
import functools
import math

import jax
import jax.numpy as jnp
from jax import lax
from jax.experimental import pallas as pl
from jax.experimental.pallas import tpu as pltpu

F32 = jnp.float32
BF16 = jnp.bfloat16
I32 = jnp.int32

D = 1024
NB = 8
SEQ = 2048
CTX = 256
T = SEQ + CTX
DEPTH = 2
N_MOD = 6
ALPHA = (2 * DEPTH) ** 0.25
LN_EPS = 1e-5
RMS_EPS = 1e-6

POOL_WINDOWS = (2, 4, 8, 16)
POOL_WIDTH = 256
POOL_GROUP_DIM = 64
GDN_HEADS = 6
GDN_DIM = 128
GDN_WIDTH = GDN_HEADS * GDN_DIM
GDN_CONV = 4
CHUNK = 64

MLA_HEADS = 8
MLA_NOPE = 128
MLA_ROPE = 64
MLA_V = 128
MLA_QK = MLA_NOPE + MLA_ROPE
MLA_Q_LORA = 384
MLA_KV_LORA = 256
GRID_W = 64
ROPE_BASE = 10000.0
ROPE_F = MLA_ROPE // 4

MOE_GROUPS = 4
MOE_PER_GROUP = 8
MOE_EXPERTS = 32
MOE_FF = 512

TT = CTX
NT = T // TT
NTL = SEQ // TT
TM = 256
FFN_DMA_GROUPS = 8
FFN_BUFS = 6
PB = 512
HB = 2
NG = GDN_HEADS // HB
NC = T // CHUNK
NCC = CTX // CHUNK
TQ = 256
LANES = 128
VMEM_LIMIT = 56 * 1024 * 1024


def _dot(a, b):
    return jnp.dot(a, b, preferred_element_type=F32)


def _dot_nt(a, b):
    return lax.dot_general(a, b, (((1,), (1,)), ((), ())), preferred_element_type=F32)


def _dot_tn(a, b):
    return lax.dot_general(a, b, (((0,), (0,)), ((), ())), preferred_element_type=F32)


def _split2(x):
    hi = x.astype(BF16)
    lo = (x - hi.astype(F32)).astype(BF16)
    return hi, lo


def _dot3(a, b):
    ah, al = _split2(a)
    bh, bl = _split2(b)
    return _dot(ah, bh) + (_dot(ah, bl) + _dot(al, bh))


def _silu(x):
    return x * jax.nn.sigmoid(x)


def _softplus(x):
    return jnp.maximum(x, 0.0) + jnp.log1p(jnp.exp(-jnp.abs(x)))


def _layer_norm(v, g, b):
    mu = jnp.mean(v, axis=-1, keepdims=True)
    c = v - mu
    var = jnp.mean(c * c, axis=-1, keepdims=True)
    return c * lax.rsqrt(var + LN_EPS) * g + b


def _mod_slice(mod_ref, row, k):
    return mod_ref[pl.ds(row, 1), k * D:(k + 1) * D]


NBLK = D // LANES


def _to_token_tiles(y):
    return jnp.transpose(jnp.stack([y[:, s * LANES:(s + 1) * LANES] for s in range(NBLK)], axis=0), (1, 0, 2))


def _from_token_tiles(x3):
    xt = jnp.transpose(x3, (1, 0, 2))
    return jnp.concatenate([xt[s] for s in range(NBLK)], axis=1)


def _cparams(sem, vmem=VMEM_LIMIT):
    return pltpu.CompilerParams(dimension_semantics=sem, vmem_limit_bytes=vmem)


def _ada_kernel(cv_ref, w_ref, b_ref, o_ref):
    s = _silu(cv_ref[...])
    o_ref[0] = _dot3(s, w_ref[0]) + b_ref[0]


def _ada(cv, ada_w, ada_b):
    nblk = N_MOD
    return pl.pallas_call(
        _ada_kernel,
        out_shape=jax.ShapeDtypeStruct((DEPTH, 16, N_MOD * D), F32),
        grid=(DEPTH, nblk),
        in_specs=[
            pl.BlockSpec((16, D), lambda l, j: (0, 0)),
            pl.BlockSpec((1, D, D), lambda l, j: (l, 0, j)),
            pl.BlockSpec((1, 1, D), lambda l, j: (l, 0, j)),
        ],
        out_specs=pl.BlockSpec((1, 16, D), lambda l, j: (l, 0, j)),
        compiler_params=_cparams(("arbitrary", "arbitrary")),
        name="ada_mod",
    )(cv, ada_w, ada_b.reshape(DEPTH, 1, N_MOD * D))


W_MAIN = POOL_WIDTH + 4 * GDN_WIDTH
W_AB = NG * LANES


def _inproj0_kernel(c_ref, x_ref, mod_ref, w_ref, wab_ref, pool_ref, main_ref, ab_ref):
    b = pl.program_id(0)
    t = pl.program_id(1)
    row = jnp.where(t == 0, NB, b)
    xt = jnp.where(t == 0, c_ref[0], x_ref[0])
    h = xt * (1.0 + _mod_slice(mod_ref, row, 1)) + _mod_slice(mod_ref, row, 0)
    p = _dot(h.astype(BF16), w_ref[...])
    pool_ref[0] = p[:, :POOL_WIDTH]
    main_ref[0] = p[:, POOL_WIDTH:]
    ab_ref[0] = _dot3(h, wab_ref[...])


def _inproj0(ctx, x, mod, w_main, w_ab):
    return pl.pallas_call(
        _inproj0_kernel,
        out_shape=(
            jax.ShapeDtypeStruct((NB, T, POOL_WIDTH), F32),
            jax.ShapeDtypeStruct((NB, T, 4 * GDN_WIDTH), F32),
            jax.ShapeDtypeStruct((NB, T, W_AB), F32),
        ),
        grid=(NB, NT),
        in_specs=[
            pl.BlockSpec((1, CTX, D), lambda b, t: (b, 0, 0)),
            pl.BlockSpec((1, TT, D), lambda b, t: (b, jnp.maximum(t - 1, 0), 0)),
            pl.BlockSpec((16, N_MOD * D), lambda b, t: (0, 0)),
            pl.BlockSpec((D, W_MAIN), lambda b, t: (0, 0)),
            pl.BlockSpec((D, W_AB), lambda b, t: (0, 0)),
        ],
        out_specs=(
            pl.BlockSpec((1, TT, POOL_WIDTH), lambda b, t: (b, t, 0)),
            pl.BlockSpec((1, TT, 4 * GDN_WIDTH), lambda b, t: (b, t, 0)),
            pl.BlockSpec((1, TT, W_AB), lambda b, t: (b, t, 0)),
        ),
        compiler_params=_cparams(("arbitrary", "arbitrary")),
        name="inproj0",
    )(ctx, x, mod, w_main, w_ab)


PAD_GAP = 16
PAD_CTX = PAD_GAP
PAD_LAT = PAD_CTX + CTX + 2 * PAD_GAP
PAD_ROWS = PAD_LAT + SEQ + PAD_GAP


def _fill_padded(pad_ref, src):
    w = pad_ref.shape[1]
    pad_ref[0:PAD_CTX, :] = jnp.zeros((PAD_CTX, w), F32)
    pad_ref[PAD_CTX + CTX:PAD_LAT, :] = jnp.zeros((2 * PAD_GAP, w), F32)
    pad_ref[PAD_LAT + SEQ:PAD_ROWS, :] = jnp.zeros((PAD_GAP, w), F32)
    pad_ref[PAD_CTX:PAD_CTX + CTX, :] = src(0, CTX)
    pad_ref[PAD_LAT:PAD_LAT + SEQ, :] = src(CTX, SEQ)


def _tile_pad_row(ti):
    return PAD_CTX if ti == 0 else PAD_LAT + (ti - 1) * TT


def _pool_kernel(u_ref, wbd_ref, scale_ref, o_ref, pad_ref):
    _fill_padded(pad_ref, lambda s, n: u_ref[0, s:s + n, :])
    lane = lax.broadcasted_iota(I32, (1, POOL_WIDTH), 1)
    grp = lane // POOL_GROUP_DIM
    win = jnp.zeros((1, POOL_WIDTH), I32)
    for g, w in enumerate(POOL_WINDOWS):
        win = jnp.where(grp == g, w, win)
    left = win // 2
    right = win - 1 - left
    for ti in range(NT):
        seg_len = CTX if ti == 0 else SEQ
        seg_t0 = 0 if ti == 0 else (ti - 1) * TT
        prow = _tile_pad_row(ti)
        tpos = seg_t0 + lax.broadcasted_iota(I32, (TT, 1), 0)
        acc = jnp.zeros((TT, POOL_WIDTH), F32)
        for j in range(-max(POOL_WINDOWS) // 2, max(POOL_WINDOWS) // 2):
            inwin = (j >= -left) & (j <= right)
            acc = acc + jnp.where(inwin, pad_ref[pl.ds(prow + j, TT), :], 0.0)
        cnt = jnp.minimum(tpos + right + 1, seg_len) - jnp.maximum(tpos - left, 0)
        dlt = acc / cnt.astype(F32) - pad_ref[pl.ds(prow, TT), :]
        y = _dot(dlt.astype(BF16), wbd_ref[...]) * scale_ref[...]
        o_ref[0, ti * TT:(ti + 1) * TT, :] = y.astype(BF16)


def _pool(pool_u, wbd, scale):
    return pl.pallas_call(
        _pool_kernel,
        out_shape=jax.ShapeDtypeStruct((NB, T, POOL_WIDTH), BF16),
        grid=(NB,),
        in_specs=[
            pl.BlockSpec((1, T, POOL_WIDTH), lambda b: (b, 0, 0)),
            pl.BlockSpec((POOL_WIDTH, POOL_WIDTH), lambda b: (0, 0)),
            pl.BlockSpec((1, POOL_WIDTH), lambda b: (0, 0)),
        ],
        out_specs=pl.BlockSpec((1, T, POOL_WIDTH), lambda b: (b, 0, 0)),
        scratch_shapes=[pltpu.VMEM((PAD_ROWS, POOL_WIDTH), F32)],
        compiler_params=_cparams(("arbitrary",)),
        name="pool",
    )(pool_u, wbd, scale)


HW = HB * GDN_DIM


NCH = 2 * HB
PW = NCH * CHUNK
GCH = 12


def _block_diag(xp, blk_masks):
    return jnp.concatenate([jnp.where(m, xp, 0.0) for m in blk_masks], axis=0).astype(BF16)


def _gdn_kernel(q_ref, k_ref, v_ref, z_ref, cwq_ref, cwk_ref, cwv_ref, ab_ref, alog_ref, dtb_ref, nw_ref,
                y_ref, pad_ref, qn_ref, kn_ref, vv_ref, m_ref, b_ref, qt_ref, o_ref, egl_ref, s_ref):
    def conv(x_ref, cw_ref, dst_ref, l2, scale):
        _fill_padded(pad_ref, lambda s, n: x_ref[0, s:s + n, :])
        cw = cw_ref[...]
        for ti in range(NT):
            prow = _tile_pad_row(ti)
            acc = jnp.zeros((TT, HW), F32)
            for j in range(GDN_CONV):
                acc = acc + pad_ref[pl.ds(prow - 2 + j, TT), :] * cw[j:j + 1, :]
            y = _silu(acc)
            if l2:
                for hh in range(HB):
                    yh = y[:, hh * GDN_DIM:(hh + 1) * GDN_DIM]
                    yh = yh * lax.rsqrt(jnp.sum(yh * yh, axis=-1, keepdims=True) + RMS_EPS)
                    dst_ref[ti * TT:(ti + 1) * TT, hh * GDN_DIM:(hh + 1) * GDN_DIM] = yh * scale
            else:
                dst_ref[ti * TT:(ti + 1) * TT, :] = y

    conv(q_ref, cwq_ref, qn_ref, True, GDN_DIM ** -0.5)
    conv(k_ref, cwk_ref, kn_ref, True, 1.0)
    conv(v_ref, cwv_ref, vv_ref, False, 1.0)

    ri = lax.broadcasted_iota(I32, (CHUNK, CHUNK), 0)
    ci = lax.broadcasted_iota(I32, (CHUNK, CHUNK), 1)
    tri2 = jnp.concatenate([(ci <= ri).astype(BF16), (ci >= ri).astype(BF16)], axis=0)
    rowp = lax.broadcasted_iota(I32, (CHUNK, PW), 0)
    lanep = lax.broadcasted_iota(I32, (CHUNK, PW), 1)
    blk = lanep // CHUNK
    colp = lanep - blk * CHUNK
    is_fwd = blk < HB
    ahead = jnp.where(is_fwd, rowp - colp, colp - rowp)
    incl_p = ahead >= 0
    strict_p = ahead > 0
    eye_p = (colp == rowp).astype(F32)
    eye_d = (lax.broadcasted_iota(I32, (GDN_DIM, GDN_DIM), 0)
             == lax.broadcasted_iota(I32, (GDN_DIM, GDN_DIM), 1)).astype(BF16)
    blk_masks = [blk == i for i in range(NCH)]
    alog = alog_ref[0]
    dtb = dtb_ref[0]

    def split3(v):
        h1 = v.astype(BF16)
        rem = v - h1.astype(F32)
        h2 = rem.astype(BF16)
        return h1, h2, (rem - h2.astype(F32)).astype(BF16)

    def phase1(it, carry):
        cs = [it * GCH + j for j in range(GCH)]
        r0 = [pl.multiple_of(c * CHUNK, CHUNK) for c in cs]
        m0 = [pl.multiple_of(c * GDN_DIM, GDN_DIM) for c in cs]
        e0 = [pl.multiple_of(c * 8, 8) for c in cs]
        G = range(GCH)
        abt = [ab_ref[0, pl.ds(r0[j], CHUNK), :] for j in G]
        g_all = [-jnp.exp(alog) * _softplus(abt[j] + dtb) for j in G]
        beta_all = [jax.nn.sigmoid(abt[j]) for j in G]
        g3 = [split3(g_all[j]) for j in G]
        gcs = [_dot(tri2, g3[j][0]) + (_dot(tri2, g3[j][1]) + _dot(tri2, g3[j][2])) for j in G]
        gb3 = [split3(jnp.where(strict_p, jnp.concatenate(
            [jnp.broadcast_to(g_all[j][:, i:i + 1], (CHUNK, CHUNK)) for i in range(NCH)], axis=1), 0.0)) for j in G]
        dif2 = [_dot(tri2, gb3[j][0]) + (_dot(tri2, gb3[j][1]) + _dot(tri2, gb3[j][2])) for j in G]
        kh = [[kn_ref[pl.ds(r0[j], CHUNK), hh * GDN_DIM:(hh + 1) * GDN_DIM] for hh in range(HB)] for j in G]
        qh = [[qn_ref[pl.ds(r0[j], CHUNK), hh * GDN_DIM:(hh + 1) * GDN_DIM] for hh in range(HB)] for j in G]
        vh = [[vv_ref[pl.ds(r0[j], CHUNK), hh * GDN_DIM:(hh + 1) * GDN_DIM] for hh in range(HB)] for j in G]
        gram = [[_dot_nt(jnp.concatenate([kh[j][hh].astype(BF16), qh[j][hh].astype(BF16), eye_d], axis=0),
                         kh[j][hh].astype(BF16)) for hh in range(HB)] for j in G]
        gcol1, bcol1, glast, egc1, kdsc = [], [], [], [], []
        for j in G:
            gcol1.append([]); bcol1.append([]); glast.append([]); egc1.append([]); kdsc.append([])
            for i in range(NCH):
                d = i // HB
                gc = gcs[j][d * CHUNK:(d + 1) * CHUNK]
                gcol1[j].append(gc[:, i:i + 1])
                bcol1[j].append(beta_all[j][:, NCH + i:NCH + i + 1])
                glast[j].append(gc[CHUNK - 1:CHUNK, i:i + 1] if d == 0 else gc[0:1, i:i + 1])
                egc1[j].append(jnp.exp(gcol1[j][i]))
                kdsc[j].append(jnp.exp(glast[j][i] - gcol1[j][i]))
                egl_ref[d, i % HB, pl.ds(e0[j], 8), :] = jnp.broadcast_to(jnp.exp(glast[j][i]), (8, LANES))
        lm, a_p = [], []
        for j in G:
            bcol_p = jnp.concatenate([jnp.broadcast_to(b, (CHUNK, CHUNK)) for b in bcol1[j]], axis=1)
            diff = jnp.where(is_fwd, dif2[j][:CHUNK], dif2[j][CHUNK:])
            dec = jnp.where(incl_p, jnp.exp(jnp.where(incl_p, diff, 0.0)), 0.0)
            kk_p = jnp.concatenate([gram[j][i % HB][:CHUNK] for i in range(NCH)], axis=1)
            qk_p = jnp.concatenate([gram[j][i % HB][CHUNK:2 * CHUNK] for i in range(NCH)], axis=1)
            lm.append(jnp.where(strict_p, bcol_p * kk_p * dec, 0.0))
            a_p.append(qk_p * dec)
        x = [eye_p - lm[j] for j in G]
        p = [_dot(lm[j].astype(BF16), _block_diag(lm[j], blk_masks)) for j in G]
        for _ in range(4):
            r = [_dot(jnp.concatenate([x[j], p[j]], axis=0).astype(BF16), _block_diag(p[j], blk_masks)) for j in G]
            x = [x[j] + r[j][:CHUNK] for j in G]
            p = [r[j][CHUNK:] for j in G]
        r = [_dot(x[j].astype(BF16), _block_diag(p[j], blk_masks)) for j in G]
        x = [x[j] + r[j] for j in G]
        rhs = [jnp.concatenate(
            [jnp.concatenate([vh[j][i % HB] * bcol1[j][i], kh[j][i % HB] * (bcol1[j][i] * egc1[j][i])], axis=1)
             for i in range(NCH)], axis=0).astype(BF16) for j in G]
        uw = [_dot(_block_diag(x[j], blk_masks), rhs[j]) for j in G]
        ao = [_dot(_block_diag(a_p[j], blk_masks), uw[j].astype(BF16)) for j in G]
        mb = [[_dot(gram[j][i % HB][2 * CHUNK:].astype(BF16),
                    (uw[j][i * CHUNK:(i + 1) * CHUNK] * kdsc[j][i]).astype(BF16)) for i in range(NCH)] for j in G]
        for j in G:
            for hh in range(HB):
                hs = slice(hh * GDN_DIM, (hh + 1) * GDN_DIM)
                o0 = jnp.zeros((CHUNK, GDN_DIM), F32)
                for d in range(2):
                    i = d * HB + hh
                    rs = slice(i * CHUNK, (i + 1) * CHUNK)
                    b_ref[d, hh, pl.ds(m0[j], GDN_DIM), :] = mb[j][i][:, :GDN_DIM].astype(BF16)
                    m_ref[d, hh, pl.ds(m0[j], GDN_DIM), :] = mb[j][i][:, GDN_DIM:].astype(BF16)
                    qt_ref[d, pl.ds(r0[j], CHUNK), hs] = (qh[j][hh] * egc1[j][i] - ao[j][rs, GDN_DIM:]).astype(BF16)
                    o0 = o0 + ao[j][rs, :GDN_DIM]
                o_ref[pl.ds(r0[j], CHUNK), hs] = o0
        return carry

    lax.fori_loop(0, NC // GCH, phase1, 0)

    s_ref[...] = jnp.zeros(s_ref.shape, F32)

    def phase2(step, carry):
        cb = jnp.where(step < NCC, NCC - 1 - step, NC + NCC - 1 - step)
        chains = [(d, hh, c) for d, c in ((0, step), (1, cb)) for hh in range(HB)]
        rows = [pl.ds(pl.multiple_of(c * CHUNK, CHUNK), CHUNK) for _, _, c in chains]
        mrows = [pl.ds(pl.multiple_of(c * GDN_DIM, GDN_DIM), GDN_DIM) for _, _, c in chains]
        hs = [slice(hh * GDN_DIM, (hh + 1) * GDN_DIM) for _, hh, _ in chains]
        n = range(len(chains))
        st = [s_ref[chains[i][0], chains[i][1]] for i in n]
        stb = [st[i].astype(BF16) for i in n]
        ms = [_dot(m_ref[chains[i][0], chains[i][1], mrows[i], :], stb[i]) for i in n]
        oq = [_dot(qt_ref[chains[i][0], rows[i], hs[i]], stb[i]) for i in n]
        for i in n:
            d, hh, c = chains[i]
            egl = egl_ref[d, hh, pl.ds(pl.multiple_of(c * 8, 8), 1), :]
            s_ref[d, hh] = st[i] * egl - ms[i] + b_ref[d, hh, mrows[i], :].astype(F32)
            o_ref[rows[i], hs[i]] += oq[i]
        return carry

    lax.fori_loop(0, NC, phase2, 0)

    nw = nw_ref[...]
    for ti in range(NT):
        rs = slice(ti * TT, (ti + 1) * TT)
        o = o_ref[rs, :]
        zz = _silu(z_ref[0, rs, :])
        for hh in range(HB):
            hs = slice(hh * GDN_DIM, (hh + 1) * GDN_DIM)
            oh = o[:, hs]
            oh = oh * lax.rsqrt(jnp.mean(oh * oh, axis=-1, keepdims=True) + RMS_EPS) * nw
            y_ref[0, rs, hs] = (oh * zz[:, hs]).astype(BF16)


def _gdn(p_main, conv_w, ab, alog_g, dtb_g, norm_w):
    nhb = GDN_WIDTH // HW
    blk = lambda off: pl.BlockSpec((1, T, HW), lambda b, g: (b, 0, off * nhb + g))
    cblk = lambda off: pl.BlockSpec((GDN_CONV, HW), lambda b, g: (0, off * nhb + g))
    return pl.pallas_call(
        _gdn_kernel,
        out_shape=jax.ShapeDtypeStruct((NB, T, GDN_WIDTH), BF16),
        grid=(NB, NG),
        in_specs=[
            blk(0), blk(1), blk(2), blk(3),
            cblk(0), cblk(1), cblk(2),
            pl.BlockSpec((1, T, LANES), lambda b, g: (b, 0, g)),
            pl.BlockSpec((1, 1, LANES), lambda b, g: (g, 0, 0)),
            pl.BlockSpec((1, 1, LANES), lambda b, g: (g, 0, 0)),
            pl.BlockSpec((1, GDN_DIM), lambda b, g: (0, 0)),
        ],
        out_specs=pl.BlockSpec((1, T, HW), lambda b, g: (b, 0, g)),
        scratch_shapes=[
            pltpu.VMEM((PAD_ROWS, HW), F32),
            pltpu.VMEM((T, HW), F32),
            pltpu.VMEM((T, HW), F32),
            pltpu.VMEM((T, HW), F32),
            pltpu.VMEM((2, HB, NC * GDN_DIM, GDN_DIM), BF16),
            pltpu.VMEM((2, HB, NC * GDN_DIM, GDN_DIM), BF16),
            pltpu.VMEM((2, T, HW), BF16),
            pltpu.VMEM((T, HW), F32),
            pltpu.VMEM((2, HB, NC * 8, LANES), F32),
            pltpu.VMEM((2, HB, GDN_DIM, GDN_DIM), F32),
        ],
        compiler_params=_cparams(("arbitrary", "arbitrary")),
        name="gdn",
    )(p_main, p_main, p_main, p_main, conv_w, conv_w, conv_w, ab, alog_g, dtb_g, norm_w)


NEG = -1e30


def _mix_kernel(n_act, ctx_tile0, *refs):
    act_refs = refs[:n_act]
    w_refs = refs[n_act:2 * n_act]
    rest = refs[2 * n_act:]
    b = pl.program_id(0)
    t = pl.program_id(1)
    if ctx_tile0:
        c_ref, x_ref, mod_ref, lng_ref, lnb_ref, wr_ref, br_ref, x1_ref, h2_ref, route_ref = rest
        row = jnp.where(t == 0, NB, b)
        xt = jnp.where(t == 0, c_ref[0], x_ref[0])
    else:
        x_ref, mod_ref, lng_ref, lnb_ref, wr_ref, br_ref, x1_ref, h2_ref, route_ref = rest
        row = b
        xt = x_ref[0]
    y = _dot(act_refs[0][0], w_refs[0][...])
    for a_ref, w_ref in zip(act_refs[1:], w_refs[1:]):
        y = y + _dot(a_ref[0], w_ref[...])
    x1 = _layer_norm(ALPHA * xt + _mod_slice(mod_ref, row, 2) * y, lng_ref[...], lnb_ref[...])
    x1_ref[0] = x1
    h2 = x1 * (1.0 + _mod_slice(mod_ref, row, 4)) + _mod_slice(mod_ref, row, 3)
    h2_ref[...] = _to_token_tiles(h2)
    logits = _dot3(h2, wr_ref[...]) + br_ref[...]
    lane = lax.broadcasted_iota(I32, (TT, LANES), 1)
    gl = jnp.where(lane < MOE_GROUPS, logits, NEG)
    gm = jnp.max(gl, axis=-1, keepdims=True)
    gsum = jnp.sum(jnp.exp(gl - gm), axis=-1, keepdims=True)
    g_val = 1.0 / gsum
    g_idx = jnp.min(jnp.where(gl == gm, lane, LANES), axis=-1, keepdims=True)
    sel = (lane >= MOE_GROUPS) & (lane < MOE_GROUPS + MOE_EXPERTS) & (((lane - MOE_GROUPS) // MOE_PER_GROUP) == g_idx)
    el = jnp.where(sel, logits, NEG)
    m1 = jnp.max(el, axis=-1, keepdims=True)
    i1 = jnp.min(jnp.where(el == m1, lane, LANES), axis=-1, keepdims=True)
    el2 = jnp.where(lane == i1, NEG, el)
    m2 = jnp.max(el2, axis=-1, keepdims=True)
    i2 = jnp.min(jnp.where(el2 == m2, lane, LANES), axis=-1, keepdims=True)
    ex2 = jnp.exp(m2 - m1)
    w1 = g_val / (1.0 + ex2)
    w2 = w1 * ex2
    e1 = (i1 - MOE_GROUPS).astype(F32)
    e2 = (i2 - MOE_GROUPS).astype(F32)
    route_ref[...] = jnp.where(lane == 0, e1, jnp.where(lane == 1, e2, jnp.where(lane == 2, w1, jnp.where(lane == 3, w2, 0.0))))


def _mix(acts, ws, streams, mod, ln_g, ln_b, wr, br, ctx_tile0):
    ntl = NT if ctx_tile0 else NTL
    if ctx_tile0:
        stream_specs = [pl.BlockSpec((1, CTX, D), lambda b, t: (b, 0, 0)),
                        pl.BlockSpec((1, TT, D), lambda b, t: (b, jnp.maximum(t - 1, 0), 0))]
    else:
        stream_specs = [pl.BlockSpec((1, TT, D), lambda b, t: (b, t + 1, 0))]
    n_tok = NB * ntl * TT
    act_specs = [pl.BlockSpec((1, TT, a.shape[2]), lambda b, t: (b, t, 0)) for a in acts]
    w_specs = [pl.BlockSpec(w.shape, lambda b, t: (0, 0)) for w in ws]
    return pl.pallas_call(
        functools.partial(_mix_kernel, len(acts), ctx_tile0),
        out_shape=(
            jax.ShapeDtypeStruct((NB, ntl * TT, D), F32),
            jax.ShapeDtypeStruct((n_tok, NBLK, LANES), F32),
            jax.ShapeDtypeStruct((n_tok, LANES), F32),
        ),
        grid=(NB, ntl),
        in_specs=act_specs + w_specs + stream_specs + [
            pl.BlockSpec((16, N_MOD * D), lambda b, t: (0, 0)),
            pl.BlockSpec((1, D), lambda b, t: (0, 0)),
            pl.BlockSpec((1, D), lambda b, t: (0, 0)),
            pl.BlockSpec((D, LANES), lambda b, t: (0, 0)),
            pl.BlockSpec((1, LANES), lambda b, t: (0, 0)),
        ],
        out_specs=(
            pl.BlockSpec((1, TT, D), lambda b, t: (b, t, 0)),
            pl.BlockSpec((TT, NBLK, LANES), lambda b, t: (b * ntl + t, 0, 0)),
            pl.BlockSpec((TT, LANES), lambda b, t: (b * ntl + t, 0)),
        ),
        compiler_params=_cparams(("arbitrary", "arbitrary")),
        name="mix_ctx" if ctx_tile0 else "mix_lat",
    )(*acts, *ws, *streams, mod, ln_g, ln_b, wr, br)


def _plan_kernel(route_ref, er_ref, cnt_ref, carry_ref):
    i = pl.program_id(0)

    @pl.when(i == 0)
    def _():
        carry_ref[...] = jnp.zeros(carry_ref.shape, F32)

    r = route_ref[...]
    lane = lax.broadcasted_iota(I32, (PB, LANES), 1)
    lanef = lane.astype(F32)
    ri = lax.broadcasted_iota(I32, (PB, PB), 0)
    ci = lax.broadcasted_iota(I32, (PB, PB), 1)
    before = (ci < ri).astype(BF16)
    oh1 = jnp.where(lanef == r[:, 0:1], 1.0, 0.0)
    oh2 = jnp.where(lanef == r[:, 1:2], 1.0, 0.0)
    c0 = carry_ref[0:1, :]
    tot1 = jnp.sum(oh1, axis=0, keepdims=True)
    tot2 = jnp.sum(oh2, axis=0, keepdims=True)
    r1 = _dot(before, oh1.astype(BF16)) + c0
    r2 = _dot(before, oh2.astype(BF16)) + (c0 + tot1)
    rank1 = jnp.sum(r1 * oh1, axis=-1, keepdims=True)
    rank2 = jnp.sum(r2 * oh2, axis=-1, keepdims=True)
    er = jnp.where(lane < 2, r, jnp.where(lane == 2, rank1, jnp.where(lane == 3, rank2, 0.0)))
    er_ref[...] = er.T[0:8, :]
    cnew = jnp.broadcast_to(c0 + tot1 + tot2, carry_ref.shape)
    carry_ref[...] = cnew
    cnt_ref[...] = cnew


def _plan(route):
    n_tok = route.shape[0]
    return pl.pallas_call(
        _plan_kernel,
        out_shape=(jax.ShapeDtypeStruct((8, n_tok), F32), jax.ShapeDtypeStruct((8, LANES), F32)),
        grid=(n_tok // PB,),
        in_specs=[pl.BlockSpec((PB, LANES), lambda i: (i, 0))],
        out_specs=(pl.BlockSpec((8, PB), lambda i: (0, i)), pl.BlockSpec((8, LANES), lambda i: (0, 0))),
        scratch_shapes=[pltpu.VMEM((8, LANES), F32)],
        compiler_params=_cparams(("arbitrary",)),
        name="moe_plan",
    )(route)


INV_UNROLL = 16


def _invmap_kernel(pos_ref, src_ref):
    n_rows = src_ref.shape[0]
    n_tok = pos_ref.shape[0] // 2

    def clear(g, c):
        for u in range(INV_UNROLL):
            src_ref[g * INV_UNROLL + u] = 0
        return c

    lax.fori_loop(0, n_rows // INV_UNROLL, clear, 0)

    def put(g, c):
        for u in range(INV_UNROLL):
            t = g * INV_UNROLL + u
            src_ref[pos_ref[t]] = t
            src_ref[pos_ref[n_tok + t]] = t
        return c

    lax.fori_loop(0, n_tok // INV_UNROLL, put, 0)


def _invmap(pos_flat, n_rows):
    return pl.pallas_call(
        _invmap_kernel,
        out_shape=jax.ShapeDtypeStruct((n_rows,), I32),
        in_specs=[pl.BlockSpec(memory_space=pltpu.SMEM)],
        out_specs=pl.BlockSpec(memory_space=pltpu.SMEM),
        name="moe_invmap",
    )(pos_flat)


def _ffn_kernel(te_ref, nu_ref, src_ref, h_hbm, wg_ref, wu_ref, wd_ref, y_ref, buf, wgb, wub, wdb, sem):
    i = pl.program_id(0)
    nu = nu_ref[0]

    def issue(tile, slot, lo, hi):
        base = tile * TM
        for r in range(lo, hi):
            pltpu.make_async_copy(h_hbm.at[src_ref[base + r]], buf.at[slot, r], sem.at[slot]).start(priority=r % 2)

    def ffn_tile(slot, next_tile, next_slot):
        per = TM // FFN_DMA_GROUPS
        sent = [0]

        def gather_some():
            if next_tile is not None and sent[0] < FFN_DMA_GROUPS:
                issue(next_tile, next_slot, sent[0] * per, (sent[0] + 1) * per)
                sent[0] += 1

        x = _from_token_tiles(buf[slot]).astype(BF16)
        hcol = MOE_FF // 2
        gather_some()
        parts = []
        for w in (wgb, wub):
            for c in range(2):
                parts.append(_dot(x, w[:, c * hcol:(c + 1) * hcol]))
                gather_some()
        a = jnp.concatenate(parts[0:2], axis=1)
        bb = jnp.concatenate(parts[2:4], axis=1)
        act = (_silu(a) * bb).astype(BF16)
        ys = []
        for c in range(D // hcol):
            ys.append(_dot(act, wdb[:, c * hcol:(c + 1) * hcol]))
            gather_some()
        assert next_tile is None or sent[0] == FFN_DMA_GROUPS
        y_ref[...] = _to_token_tiles(jnp.concatenate(ys, axis=1))

    @pl.when(i == 0)
    def _():
        issue(0, 0, 0, TM)

        for j in range(1, FFN_BUFS - 1):
            @pl.when(nu > j)
            def _():
                issue(j, j, 0, TM)

    @pl.when(i < nu)
    def _():
        slot = i % FFN_BUFS
        pltpu.make_async_copy(h_hbm.at[pl.ds(0, TM)], buf.at[slot], sem.at[slot]).wait()
        e = te_ref[i]
        ep = te_ref[jnp.maximum(i - 1, 0)]

        @pl.when((i == 0) | (e != ep))
        def _():
            wgb[...] = wg_ref[0, 0].astype(BF16)
            wub[...] = wu_ref[0, 0].astype(BF16)
            wdb[...] = wd_ref[0, 0].astype(BF16)

        ahead = i + (FFN_BUFS - 1)

        @pl.when(ahead < nu)
        def _():
            ffn_tile(slot, ahead, ahead % FFN_BUFS)

        @pl.when(ahead >= nu)
        def _():
            ffn_tile(slot, None, None)

    @pl.when(i >= nu)
    def _():
        y_ref[...] = jnp.zeros(y_ref.shape, F32)


def _ffn(tile_expert, n_used, src, h2, wg, wu, wd, layer):
    n_rows = src.shape[0]
    n_tiles = n_rows // TM
    wspec = lambda shape: pl.BlockSpec((1, 1) + shape, lambda i, te, nu, src: (layer, te[i], 0, 0))
    return pl.pallas_call(
        _ffn_kernel,
        out_shape=jax.ShapeDtypeStruct((n_rows, NBLK, LANES), F32),
        grid_spec=pltpu.PrefetchScalarGridSpec(
            num_scalar_prefetch=3,
            grid=(n_tiles,),
            in_specs=[
                pl.BlockSpec(memory_space=pl.ANY),
                wspec((D, MOE_FF)), wspec((D, MOE_FF)), wspec((MOE_FF, D)),
            ],
            out_specs=pl.BlockSpec((TM, NBLK, LANES), lambda i, te, nu, src: (i, 0, 0)),
            scratch_shapes=[
                pltpu.VMEM((FFN_BUFS, TM, NBLK, LANES), F32),
                pltpu.VMEM((D, MOE_FF), BF16),
                pltpu.VMEM((D, MOE_FF), BF16),
                pltpu.VMEM((MOE_FF, D), BF16),
                pltpu.SemaphoreType.DMA((FFN_BUFS,)),
            ],
        ),
        compiler_params=_cparams(("arbitrary",)),
        name="moe_ffn",
    )(tile_expert, n_used, src, h2, wg, wu, wd)


def _combine_kernel(ntl, ctx_tile0, pos_ref, y_hbm, x1_ref, route_ref, mod_ref, lng_ref, lnb_ref, o_ref, buf, sem):
    b = pl.program_id(0)
    t = pl.program_id(1)
    flat = b * ntl + t
    n_tiles = NB * ntl

    def issue(tile, slot):
        base = tile * TT
        for r in range(TT):
            for k in range(2):
                pltpu.make_async_copy(y_hbm.at[pos_ref[k * (n_tiles * TT) + base + r]], buf.at[slot, k * TT + r],
                                      sem.at[slot]).start(priority=k)

    @pl.when(flat == 0)
    def _():
        issue(0, 0)

    slot = flat % 2
    pltpu.make_async_copy(y_hbm.at[pl.ds(0, 2 * TT)], buf.at[slot], sem.at[slot]).wait()

    @pl.when(flat + 1 < n_tiles)
    def _():
        issue(flat + 1, 1 - slot)

    row = jnp.where(t == 0, NB, b) if ctx_tile0 else b
    r = route_ref[...]
    f = (r[:, 2:3] * _from_token_tiles(buf[slot, 0:TT]) + r[:, 3:4] * _from_token_tiles(buf[slot, TT:2 * TT]))
    o_ref[0] = _layer_norm(ALPHA * x1_ref[0] + _mod_slice(mod_ref, row, 5) * f, lng_ref[...], lnb_ref[...])


def _combine(pos_flat, ys, x1, route, mod, ln_g, ln_b, ctx_tile0):
    ntl = NT if ctx_tile0 else NTL
    return pl.pallas_call(
        functools.partial(_combine_kernel, ntl, ctx_tile0),
        out_shape=jax.ShapeDtypeStruct((NB, ntl * TT, D), F32),
        grid_spec=pltpu.PrefetchScalarGridSpec(
            num_scalar_prefetch=1,
            grid=(NB, ntl),
            in_specs=[
                pl.BlockSpec(memory_space=pl.ANY),
                pl.BlockSpec((1, TT, D), lambda b, t, pos: (b, t, 0)),
                pl.BlockSpec((TT, LANES), lambda b, t, pos: (b * ntl + t, 0)),
                pl.BlockSpec((16, N_MOD * D), lambda b, t, pos: (0, 0)),
                pl.BlockSpec((1, D), lambda b, t, pos: (0, 0)),
                pl.BlockSpec((1, D), lambda b, t, pos: (0, 0)),
            ],
            out_specs=pl.BlockSpec((1, TT, D), lambda b, t, pos: (b, t, 0)),
            scratch_shapes=[pltpu.VMEM((2, 2 * TT, NBLK, LANES), F32), pltpu.SemaphoreType.DMA((2,))],
        ),
        compiler_params=_cparams(("arbitrary", "arbitrary")),
        name="moe_combine_ctx" if ctx_tile0 else "moe_combine_lat",
    )(pos_flat, ys, x1, route, mod, ln_g, ln_b)


def _moe(h2, route, x1, mod, ln_g, ln_b, wg, wu, wd, layer, ctx_tile0):
    n_tok = h2.shape[0]
    n_tiles = (2 * n_tok) // TM + MOE_EXPERTS
    er, cnt = _plan(route)
    counts = cnt[0, :MOE_EXPERTS].astype(I32)
    tiles_e = (counts + TM - 1) // TM
    tile_end = jnp.cumsum(tiles_e)
    tile_start = tile_end - tiles_e
    n_used = tile_end[-1]
    tj = jnp.arange(n_tiles, dtype=I32)
    te = jnp.minimum(jnp.sum((tile_end[None, :] <= tj[:, None]).astype(I32), axis=1), MOE_EXPERTS - 1)
    te_last = jnp.max(jnp.where(tj < n_used, te, 0))
    tile_expert = jnp.where(tj < n_used, te, te_last)
    eid = er[0:2].astype(I32)
    first_row = jnp.sum(jnp.where(eid[None] == jnp.arange(MOE_EXPERTS, dtype=I32)[:, None, None],
                                  (tile_start * TM)[:, None, None], 0), axis=0)
    pos_flat = (first_row + er[2:4].astype(I32)).reshape(-1)
    src = _invmap(pos_flat, n_tiles * TM)
    ys = _ffn(tile_expert, n_used.reshape(1), src, h2, wg, wu, wd, layer)
    return _combine(pos_flat, ys, x1, route, mod, ln_g, ln_b, ctx_tile0)


W_IN1 = MLA_KV_LORA + MLA_ROPE + MLA_Q_LORA + MLA_ROPE
Q_OFF = MLA_KV_LORA + MLA_ROPE
KRS_OFF = Q_OFF + MLA_Q_LORA
SM_SCALE = MLA_QK ** -0.5


def _rms(v, w):
    return v * lax.rsqrt(jnp.mean(v * v, axis=-1, keepdims=True) + RMS_EPS) * w


def _mla_proj_kernel(x_ref, mod_ref, win_ref, kvn_ref, qnw_ref, wkn_ref, wkv_ref, wqn_ref, wqr_ref, wqrs_ref,
                     cosk_ref, sink_ref, cosq_ref, sinq_ref, q_out, k_out, v_out):
    b = pl.program_id(0)
    t = pl.program_id(1)
    row = jnp.where(t == 0, NB, b)
    h = x_ref[0] * (1.0 + _mod_slice(mod_ref, row, 1)) + _mod_slice(mod_ref, row, 0)
    p = _dot(h.astype(BF16), win_ref[...])
    ckv = _rms(p[:, :MLA_KV_LORA], kvn_ref[...]).astype(BF16)
    kn = _dot(ckv, wkn_ref[...])
    vv = _dot(ckv, wkv_ref[...])
    kr = p[:, MLA_KV_LORA:Q_OFF]
    krs = p[:, KRS_OFF:KRS_OFF + MLA_ROPE]
    kr = jnp.where(t > 0, kr * cosk_ref[...] + krs * sink_ref[...], kr).astype(BF16)
    ql = _rms(p[:, Q_OFF:KRS_OFF], qnw_ref[...]).astype(BF16)
    qn = _dot(ql, wqn_ref[...]) * SM_SCALE
    qr = (_dot(ql, wqr_ref[...]) * cosq_ref[...] + _dot(ql, wqrs_ref[...]) * sinq_ref[...]) * SM_SCALE
    for hd in range(MLA_HEADS):
        k_out[0, hd, :, 0:MLA_NOPE] = kn[:, hd * MLA_NOPE:(hd + 1) * MLA_NOPE].astype(BF16)
        k_out[0, hd, :, MLA_NOPE:MLA_QK] = kr
        v_out[0, hd] = vv[:, hd * MLA_V:(hd + 1) * MLA_V].T.astype(BF16)
        q_out[0, hd, :, 0:MLA_NOPE] = qn[:, hd * MLA_NOPE:(hd + 1) * MLA_NOPE].astype(BF16)
        q_out[0, hd, :, MLA_NOPE:MLA_QK] = qr[:, hd * MLA_ROPE:(hd + 1) * MLA_ROPE].astype(BF16)


def _mla_proj(xs, mod, win, kvn, qnw, wkn, wkv, wqn, wqr, wqrs, cosk, sink, cosq, sinq):
    full = lambda a: pl.BlockSpec(a.shape, lambda b, t: (0, 0))
    lat = lambda w: pl.BlockSpec((TT, w), lambda b, t: (jnp.maximum(t - 1, 0), 0))
    return pl.pallas_call(
        _mla_proj_kernel,
        out_shape=(
            jax.ShapeDtypeStruct((NB, MLA_HEADS, SEQ, MLA_QK), BF16),
            jax.ShapeDtypeStruct((NB, MLA_HEADS, T, MLA_QK), BF16),
            jax.ShapeDtypeStruct((NB, MLA_HEADS, MLA_V, T), BF16),
        ),
        grid=(NB, NT),
        in_specs=[
            pl.BlockSpec((1, TT, D), lambda b, t: (b, t, 0)),
            pl.BlockSpec((16, N_MOD * D), lambda b, t: (0, 0)),
            full(win), full(kvn), full(qnw), full(wkn), full(wkv), full(wqn), full(wqr), full(wqrs),
            lat(MLA_ROPE), lat(MLA_ROPE), lat(MLA_HEADS * MLA_ROPE), lat(MLA_HEADS * MLA_ROPE),
        ],
        out_specs=(
            pl.BlockSpec((1, MLA_HEADS, TT, MLA_QK), lambda b, t: (b, 0, jnp.maximum(t - 1, 0), 0)),
            pl.BlockSpec((1, MLA_HEADS, TT, MLA_QK), lambda b, t: (b, 0, t, 0)),
            pl.BlockSpec((1, MLA_HEADS, MLA_V, TT), lambda b, t: (b, 0, 0, t)),
        ),
        compiler_params=_cparams(("arbitrary", "arbitrary")),
        name="mla_proj",
    )(xs, mod, win, kvn, qnw, wkn, wkv, wqn, wqr, wqrs, cosk, sink, cosq, sinq)


def _attn_kernel(q_ref, k_ref, vt_ref, o_ref, s_ref):
    k = k_ref[0, 0]
    vt = vt_ref[0, 0]
    nq = SEQ // TQ

    def scores(j):
        s_ref[j % 2] = _dot_nt(k, q_ref[0, 0, j * TQ:(j + 1) * TQ, :])

    scores(0)
    for j in range(nq):
        if j + 1 < nq:
            scores(j + 1)
        s = s_ref[j % 2]
        m = jnp.max(s, axis=0, keepdims=True)
        p = jnp.exp(s - m)
        l = jnp.sum(p, axis=0, keepdims=True)
        ot = _dot(vt, p.astype(BF16)) / l
        o_ref[0, j * TQ:(j + 1) * TQ, :] = ot.T.astype(BF16)


def _attn(q, k, vt):
    return pl.pallas_call(
        _attn_kernel,
        out_shape=jax.ShapeDtypeStruct((NB, SEQ, MLA_HEADS * MLA_V), BF16),
        grid=(NB, MLA_HEADS),
        in_specs=[
            pl.BlockSpec((1, 1, SEQ, MLA_QK), lambda b, h: (b, h, 0, 0)),
            pl.BlockSpec((1, 1, T, MLA_QK), lambda b, h: (b, h, 0, 0)),
            pl.BlockSpec((1, 1, MLA_V, T), lambda b, h: (b, h, 0, 0)),
        ],
        out_specs=pl.BlockSpec((1, SEQ, MLA_V), lambda b, h: (b, 0, h)),
        scratch_shapes=[pltpu.VMEM((2, T, TQ), F32)],
        compiler_params=_cparams(("arbitrary", "arbitrary")),
        name="mla_attn",
    )(q, k, vt)


def _rope_tables():
    n = SEQ
    rowp = (jnp.arange(n) // GRID_W).astype(F32)
    colp = (jnp.arange(n) % GRID_W).astype(F32)
    inv_freq = ROPE_BASE ** (-jnp.arange(ROPE_F, dtype=F32) / ROPE_F)
    ar = rowp[:, None] * inv_freq
    ac = colp[:, None] * inv_freq
    cos = jnp.concatenate([jnp.cos(ar), jnp.cos(ar), jnp.cos(ac), jnp.cos(ac)], axis=1)
    sin = jnp.concatenate([-jnp.sin(ar), jnp.sin(ar), -jnp.sin(ac), jnp.sin(ac)], axis=1)
    return cos, sin


def _rope_swap_perm():
    f = ROPE_F
    return jnp.concatenate([jnp.arange(f, 2 * f), jnp.arange(0, f), jnp.arange(3 * f, 4 * f), jnp.arange(2 * f, 3 * f)])


def _router_params(w_group, b_group, w_expert, b_expert):
    pad = LANES - MOE_GROUPS - MOE_EXPERTS
    wr = jnp.concatenate([w_group, w_expert, jnp.zeros((D, pad), F32)], axis=1)
    br = jnp.concatenate([b_group, b_expert, jnp.zeros((pad,), F32)]).reshape(1, LANES)
    return wr, br


def kernel(x, c, ctx, c_ctx, ada_w, ada_b, ln_g, ln_b, ab_w_in, pool_w, pool_scale, gdn_conv_w, gdn_a_log, gdn_dt_bias, gdn_norm_w, ab_w_out, mla_w_in, mla_kv_norm, mla_w_ukv, mla_q_norm, mla_w_uq, mla_w_out, moe_w_group, moe_b_group, moe_w_expert, moe_b_expert, moe_w_gate, moe_w_up, moe_w_down):
    assert x.shape == (NB, SEQ, D) and ctx.shape == (NB, CTX, D)
    cv = jnp.concatenate([c, c_ctx[None, :], jnp.zeros((16 - NB - 1, D), F32)], axis=0)
    mod = _ada(cv, ada_w, ada_b)

    w_in = ab_w_in[0]
    w_main = w_in[:, :W_MAIN].astype(BF16)
    o_ab = W_MAIN
    ab_cols = []
    for g in range(NG):
        idx = [o_ab + kind * 2 * GDN_HEADS + d * GDN_HEADS + g * HB + hh
               for kind in range(2) for d in range(2) for hh in range(HB)]
        ab_cols.append(jnp.concatenate([w_in[:, jnp.array(idx)], jnp.zeros((D, LANES - len(idx)), F32)], axis=1))
    w_ab = jnp.concatenate(ab_cols, axis=1)

    def gate_rows(p):
        rows = []
        for g in range(NG):
            vals = jnp.stack([p[d, g * HB + hh] for d in range(2) for hh in range(HB)])
            rows.append(jnp.concatenate([vals, jnp.zeros((LANES - 2 * HB,), F32)]))
        return jnp.stack(rows).reshape(NG, 1, LANES)

    alog_g = gate_rows(gdn_a_log[0])
    dtb_g = gate_rows(gdn_dt_bias[0])
    wbd = jax.scipy.linalg.block_diag(*[pool_w[0, g] for g in range(len(POOL_WINDOWS))]).astype(BF16)

    pool_u, p_main, ab = _inproj0(ctx, x, mod[0], w_main, w_ab)
    pool_y = _pool(pool_u, wbd, pool_scale[0].reshape(1, POOL_WIDTH))
    gdn_y = _gdn(p_main, gdn_conv_w[0], ab, alog_g, dtb_g, gdn_norm_w[0].reshape(1, GDN_DIM))

    w_out0 = ab_w_out[0].astype(BF16)
    wr0, br0 = _router_params(moe_w_group[0], moe_b_group[0], moe_w_expert[0], moe_b_expert[0])
    x1, h2, route = _mix([pool_y, gdn_y], [w_out0[:POOL_WIDTH], w_out0[POOL_WIDTH:]], (ctx, x), mod[0],
                         ln_g[0, 0].reshape(1, D), ln_b[0, 0].reshape(1, D), wr0, br0, True)
    xs = _moe(h2, route, x1, mod[0], ln_g[0, 1].reshape(1, D), ln_b[0, 1].reshape(1, D),
              moe_w_gate, moe_w_up, moe_w_down, 0, True)

    perm = _rope_swap_perm()
    w_in1 = mla_w_in[0]
    win = jnp.concatenate([w_in1, w_in1[:, MLA_KV_LORA:Q_OFF][:, perm]], axis=1).astype(BF16)
    ukv = mla_w_ukv[0].reshape(MLA_KV_LORA, MLA_HEADS, MLA_NOPE + MLA_V)
    wkn = ukv[:, :, :MLA_NOPE].reshape(MLA_KV_LORA, MLA_HEADS * MLA_NOPE).astype(BF16)
    wkv = ukv[:, :, MLA_NOPE:].reshape(MLA_KV_LORA, MLA_HEADS * MLA_V).astype(BF16)
    uq = mla_w_uq[0].reshape(MLA_Q_LORA, MLA_HEADS, MLA_QK)
    wqn = uq[:, :, :MLA_NOPE].reshape(MLA_Q_LORA, MLA_HEADS * MLA_NOPE).astype(BF16)
    wqr = uq[:, :, MLA_NOPE:].reshape(MLA_Q_LORA, MLA_HEADS * MLA_ROPE).astype(BF16)
    wqrs = uq[:, :, MLA_NOPE:][:, :, perm].reshape(MLA_Q_LORA, MLA_HEADS * MLA_ROPE).astype(BF16)
    cosk, sink = _rope_tables()
    cosq = jnp.tile(cosk, (1, MLA_HEADS))
    sinq = jnp.tile(sink, (1, MLA_HEADS))

    q, k, v = _mla_proj(xs, mod[1], win, mla_kv_norm[0].reshape(1, MLA_KV_LORA), mla_q_norm[0].reshape(1, MLA_Q_LORA),
                        wkn, wkv, wqn, wqr, wqrs, cosk, sink, cosq, sinq)
    att = _attn(q, k, v)
    wr1, br1 = _router_params(moe_w_group[1], moe_b_group[1], moe_w_expert[1], moe_b_expert[1])
    x1, h2, route = _mix([att], [mla_w_out[0].astype(BF16)], (xs,), mod[1],
                         ln_g[1, 0].reshape(1, D), ln_b[1, 0].reshape(1, D), wr1, br1, False)
    return _moe(h2, route, x1, mod[1], ln_g[1, 1].reshape(1, D), ln_b[1, 1].reshape(1, D),
                moe_w_gate, moe_w_up, moe_w_down, 1, False)
```

```python
import functools
import math

import jax
import jax.numpy as jnp
from jax import lax
from jax.experimental import pallas as pl
from jax.experimental.pallas import tpu as pltpu

F32 = jnp.float32
BF16 = jnp.bfloat16
I32 = jnp.int32

D = 1024
NB = 8
SEQ = 2048
CTX = 256
T = SEQ + CTX
DEPTH = 2
N_MOD = 6
ALPHA = (2 * DEPTH) ** 0.25
LN_EPS = 1e-5
RMS_EPS = 1e-6

POOL_WINDOWS = (2, 4, 8, 16)
POOL_WIDTH = 256
POOL_GROUP_DIM = 64
GDN_HEADS = 6
GDN_DIM = 128
GDN_WIDTH = GDN_HEADS * GDN_DIM
GDN_CONV = 4
CHUNK = 64

MLA_HEADS = 8
MLA_NOPE = 128
MLA_ROPE = 64
MLA_V = 128
MLA_QK = MLA_NOPE + MLA_ROPE
MLA_Q_LORA = 384
MLA_KV_LORA = 256
GRID_W = 64
ROPE_BASE = 10000.0
ROPE_F = MLA_ROPE // 4

MOE_GROUPS = 4
MOE_PER_GROUP = 8
MOE_EXPERTS = 32
MOE_FF = 512

TT = CTX
NT = T // TT
NTL = SEQ // TT
TM = 256
FFN_DMA_GROUPS = 8
FFN_BUFS = 6
PB = 512
HB = 2
NG = GDN_HEADS // HB
NC = T // CHUNK
NCC = CTX // CHUNK
TQ = 256
LANES = 128
VMEM_LIMIT = 56 * 1024 * 1024


def _dot(a, b):
    return jnp.dot(a, b, preferred_element_type=F32)


def _dot_nt(a, b):
    return lax.dot_general(a, b, (((1,), (1,)), ((), ())), preferred_element_type=F32)


def _dot_tn(a, b):
    return lax.dot_general(a, b, (((0,), (0,)), ((), ())), preferred_element_type=F32)


def _split2(x):
    hi = x.astype(BF16)
    lo = (x - hi.astype(F32)).astype(BF16)
    return hi, lo


def _dot3(a, b):
    ah, al = _split2(a)
    bh, bl = _split2(b)
    return _dot(ah, bh) + (_dot(ah, bl) + _dot(al, bh))


def _silu(x):
    return x * jax.nn.sigmoid(x)


def _softplus(x):
    return jnp.maximum(x, 0.0) + jnp.log1p(jnp.exp(-jnp.abs(x)))


def _layer_norm(v, g, b):
    mu = jnp.mean(v, axis=-1, keepdims=True)
    c = v - mu
    var = jnp.mean(c * c, axis=-1, keepdims=True)
    return c * lax.rsqrt(var + LN_EPS) * g + b


def _mod_slice(mod_ref, row, k):
    return mod_ref[pl.ds(row, 1), k * D:(k + 1) * D]


NBLK = D // LANES


def _to_token_tiles(y):
    return jnp.transpose(jnp.stack([y[:, s * LANES:(s + 1) * LANES] for s in range(NBLK)], axis=0), (1, 0, 2))


def _from_token_tiles(x3):
    xt = jnp.transpose(x3, (1, 0, 2))
    return jnp.concatenate([xt[s] for s in range(NBLK)], axis=1)


def _cparams(sem, vmem=VMEM_LIMIT):
    return pltpu.CompilerParams(dimension_semantics=sem, vmem_limit_bytes=vmem)


def _ada_kernel(cv_ref, w_ref, b_ref, o_ref):
    s = _silu(cv_ref[...])
    o_ref[0] = _dot3(s, w_ref[0]) + b_ref[0]


def _ada(cv, ada_w, ada_b):
    nblk = N_MOD
    return pl.pallas_call(
        _ada_kernel,
        out_shape=jax.ShapeDtypeStruct((DEPTH, 16, N_MOD * D), F32),
        grid=(DEPTH, nblk),
        in_specs=[
            pl.BlockSpec((16, D), lambda l, j: (0, 0)),
            pl.BlockSpec((1, D, D), lambda l, j: (l, 0, j)),
            pl.BlockSpec((1, 1, D), lambda l, j: (l, 0, j)),
        ],
        out_specs=pl.BlockSpec((1, 16, D), lambda l, j: (l, 0, j)),
        compiler_params=_cparams(("arbitrary", "arbitrary")),
        name="ada_mod",
    )(cv, ada_w, ada_b.reshape(DEPTH, 1, N_MOD * D))


W_MAIN = POOL_WIDTH + 4 * GDN_WIDTH
W_AB = NG * LANES


def _inproj0_kernel(c_ref, x_ref, mod_ref, w_ref, wab_ref, pool_ref, main_ref, ab_ref):
    b = pl.program_id(0)
    t = pl.program_id(1)
    row = jnp.where(t == 0, NB, b)
    xt = jnp.where(t == 0, c_ref[0], x_ref[0])
    h = xt * (1.0 + _mod_slice(mod_ref, row, 1)) + _mod_slice(mod_ref, row, 0)
    p = _dot(h.astype(BF16), w_ref[...])
    pool_ref[0] = p[:, :POOL_WIDTH]
    main_ref[0] = p[:, POOL_WIDTH:]
    ab_ref[0] = _dot3(h, wab_ref[...])


def _inproj0(ctx, x, mod, w_main, w_ab):
    return pl.pallas_call(
        _inproj0_kernel,
        out_shape=(
            jax.ShapeDtypeStruct((NB, T, POOL_WIDTH), F32),
            jax.ShapeDtypeStruct((NB, T, 4 * GDN_WIDTH), F32),
            jax.ShapeDtypeStruct((NB, T, W_AB), F32),
        ),
        grid=(NB, NT),
        in_specs=[
            pl.BlockSpec((1, CTX, D), lambda b, t: (b, 0, 0)),
            pl.BlockSpec((1, TT, D), lambda b, t: (b, jnp.maximum(t - 1, 0), 0)),
            pl.BlockSpec((16, N_MOD * D), lambda b, t: (0, 0)),
            pl.BlockSpec((D, W_MAIN), lambda b, t: (0, 0)),
            pl.BlockSpec((D, W_AB), lambda b, t: (0, 0)),
        ],
        out_specs=(
            pl.BlockSpec((1, TT, POOL_WIDTH), lambda b, t: (b, t, 0)),
            pl.BlockSpec((1, TT, 4 * GDN_WIDTH), lambda b, t: (b, t, 0)),
            pl.BlockSpec((1, TT, W_AB), lambda b, t: (b, t, 0)),
        ),
        compiler_params=_cparams(("arbitrary", "arbitrary")),
        name="inproj0",
    )(ctx, x, mod, w_main, w_ab)


PAD_GAP = 16
PAD_CTX = PAD_GAP
PAD_LAT = PAD_CTX + CTX + 2 * PAD_GAP
PAD_ROWS = PAD_LAT + SEQ + PAD_GAP


def _fill_padded(pad_ref, src):
    w = pad_ref.shape[1]
    pad_ref[0:PAD_CTX, :] = jnp.zeros((PAD_CTX, w), F32)
    pad_ref[PAD_CTX + CTX:PAD_LAT, :] = jnp.zeros((2 * PAD_GAP, w), F32)
    pad_ref[PAD_LAT + SEQ:PAD_ROWS, :] = jnp.zeros((PAD_GAP, w), F32)
    pad_ref[PAD_CTX:PAD_CTX + CTX, :] = src(0, CTX)
    pad_ref[PAD_LAT:PAD_LAT + SEQ, :] = src(CTX, SEQ)


def _tile_pad_row(ti):
    return PAD_CTX if ti == 0 else PAD_LAT + (ti - 1) * TT


def _pool_kernel(u_ref, wbd_ref, scale_ref, o_ref, pad_ref):
    _fill_padded(pad_ref, lambda s, n: u_ref[0, s:s + n, :])
    lane = lax.broadcasted_iota(I32, (1, POOL_WIDTH), 1)
    grp = lane // POOL_GROUP_DIM
    win = jnp.zeros((1, POOL_WIDTH), I32)
    for g, w in enumerate(POOL_WINDOWS):
        win = jnp.where(grp == g, w, win)
    left = win // 2
    right = win - 1 - left
    for ti in range(NT):
        seg_len = CTX if ti == 0 else SEQ
        seg_t0 = 0 if ti == 0 else (ti - 1) * TT
        prow = _tile_pad_row(ti)
        tpos = seg_t0 + lax.broadcasted_iota(I32, (TT, 1), 0)
        acc = jnp.zeros((TT, POOL_WIDTH), F32)
        for j in range(-max(POOL_WINDOWS) // 2, max(POOL_WINDOWS) // 2):
            inwin = (j >= -left) & (j <= right)
            acc = acc + jnp.where(inwin, pad_ref[pl.ds(prow + j, TT), :], 0.0)
        cnt = jnp.minimum(tpos + right + 1, seg_len) - jnp.maximum(tpos - left, 0)
        dlt = acc / cnt.astype(F32) - pad_ref[pl.ds(prow, TT), :]
        y = _dot(dlt.astype(BF16), wbd_ref[...]) * scale_ref[...]
        o_ref[0, ti * TT:(ti + 1) * TT, :] = y.astype(BF16)


def _pool(pool_u, wbd, scale):
    return pl.pallas_call(
        _pool_kernel,
        out_shape=jax.ShapeDtypeStruct((NB, T, POOL_WIDTH), BF16),
        grid=(NB,),
        in_specs=[
            pl.BlockSpec((1, T, POOL_WIDTH), lambda b: (b, 0, 0)),
            pl.BlockSpec((POOL_WIDTH, POOL_WIDTH), lambda b: (0, 0)),
            pl.BlockSpec((1, POOL_WIDTH), lambda b: (0, 0)),
        ],
        out_specs=pl.BlockSpec((1, T, POOL_WIDTH), lambda b: (b, 0, 0)),
        scratch_shapes=[pltpu.VMEM((PAD_ROWS, POOL_WIDTH), F32)],
        compiler_params=_cparams(("arbitrary",)),
        name="pool",
    )(pool_u, wbd, scale)


HW = HB * GDN_DIM


NCH = 2 * HB
PW = NCH * CHUNK
GCH = 12


def _block_diag(xp, blk_masks):
    return jnp.concatenate([jnp.where(m, xp, 0.0) for m in blk_masks], axis=0).astype(BF16)


def _gdn_kernel(q_ref, k_ref, v_ref, z_ref, cwq_ref, cwk_ref, cwv_ref, ab_ref, alog_ref, dtb_ref, nw_ref,
                y_ref, pad_ref, qn_ref, kn_ref, vv_ref, m_ref, b_ref, qt_ref, o_ref, egl_ref, s_ref):
    def conv(x_ref, cw_ref, dst_ref, l2, scale):
        _fill_padded(pad_ref, lambda s, n: x_ref[0, s:s + n, :])
        cw = cw_ref[...]
        for ti in range(NT):
            prow = _tile_pad_row(ti)
            acc = jnp.zeros((TT, HW), F32)
            for j in range(GDN_CONV):
                acc = acc + pad_ref[pl.ds(prow - 2 + j, TT), :] * cw[j:j + 1, :]
            y = _silu(acc)
            if l2:
                for hh in range(HB):
                    yh = y[:, hh * GDN_DIM:(hh + 1) * GDN_DIM]
                    yh = yh * lax.rsqrt(jnp.sum(yh * yh, axis=-1, keepdims=True) + RMS_EPS)
                    dst_ref[ti * TT:(ti + 1) * TT, hh * GDN_DIM:(hh + 1) * GDN_DIM] = yh * scale
            else:
                dst_ref[ti * TT:(ti + 1) * TT, :] = y

    conv(q_ref, cwq_ref, qn_ref, True, GDN_DIM ** -0.5)
    conv(k_ref, cwk_ref, kn_ref, True, 1.0)
    conv(v_ref, cwv_ref, vv_ref, False, 1.0)

    ri = lax.broadcasted_iota(I32, (CHUNK, CHUNK), 0)
    ci = lax.broadcasted_iota(I32, (CHUNK, CHUNK), 1)
    tri2 = jnp.concatenate([(ci <= ri).astype(BF16), (ci >= ri).astype(BF16)], axis=0)
    rowp = lax.broadcasted_iota(I32, (CHUNK, PW), 0)
    lanep = lax.broadcasted_iota(I32, (CHUNK, PW), 1)
    blk = lanep // CHUNK
    colp = lanep - blk * CHUNK
    is_fwd = blk < HB
    ahead = jnp.where(is_fwd, rowp - colp, colp - rowp)
    incl_p = ahead >= 0
    strict_p = ahead > 0
    eye_p = (colp == rowp).astype(F32)
    eye_d = (lax.broadcasted_iota(I32, (GDN_DIM, GDN_DIM), 0)
             == lax.broadcasted_iota(I32, (GDN_DIM, GDN_DIM), 1)).astype(BF16)
    blk_masks = [blk == i for i in range(NCH)]
    alog = alog_ref[0]
    dtb = dtb_ref[0]

    def split3(v):
        h1 = v.astype(BF16)
        rem = v - h1.astype(F32)
        h2 = rem.astype(BF16)
        return h1, h2, (rem - h2.astype(F32)).astype(BF16)

    def phase1(it, carry):
        cs = [it * GCH + j for j in range(GCH)]
        r0 = [pl.multiple_of(c * CHUNK, CHUNK) for c in cs]
        m0 = [pl.multiple_of(c * GDN_DIM, GDN_DIM) for c in cs]
        e0 = [pl.multiple_of(c * 8, 8) for c in cs]
        G = range(GCH)
        abt = [ab_ref[0, pl.ds(r0[j], CHUNK), :] for j in G]
        g_all = [-jnp.exp(alog) * _softplus(abt[j] + dtb) for j in G]
        beta_all = [jax.nn.sigmoid(abt[j]) for j in G]
        g3 = [split3(g_all[j]) for j in G]
        gcs = [_dot(tri2, g3[j][0]) + (_dot(tri2, g3[j][1]) + _dot(tri2, g3[j][2])) for j in G]
        gb3 = [split3(jnp.where(strict_p, jnp.concatenate(
            [jnp.broadcast_to(g_all[j][:, i:i + 1], (CHUNK, CHUNK)) for i in range(NCH)], axis=1), 0.0)) for j in G]
        dif2 = [_dot(tri2, gb3[j][0]) + (_dot(tri2, gb3[j][1]) + _dot(tri2, gb3[j][2])) for j in G]
        kh = [[kn_ref[pl.ds(r0[j], CHUNK), hh * GDN_DIM:(hh + 1) * GDN_DIM] for hh in range(HB)] for j in G]
        qh = [[qn_ref[pl.ds(r0[j], CHUNK), hh * GDN_DIM:(hh + 1) * GDN_DIM] for hh in range(HB)] for j in G]
        vh = [[vv_ref[pl.ds(r0[j], CHUNK), hh * GDN_DIM:(hh + 1) * GDN_DIM] for hh in range(HB)] for j in G]
        gram = [[_dot_nt(jnp.concatenate([kh[j][hh].astype(BF16), qh[j][hh].astype(BF16), eye_d], axis=0),
                         kh[j][hh].astype(BF16)) for hh in range(HB)] for j in G]
        gcol1, bcol1, glast, egc1, kdsc = [], [], [], [], []
        for j in G:
            gcol1.append([]); bcol1.append([]); glast.append([]); egc1.append([]); kdsc.append([])
            for i in range(NCH):
                d = i // HB
                gc = gcs[j][d * CHUNK:(d + 1) * CHUNK]
                gcol1[j].append(gc[:, i:i + 1])
                bcol1[j].append(beta_all[j][:, NCH + i:NCH + i + 1])
                glast[j].append(gc[CHUNK - 1:CHUNK, i:i + 1] if d == 0 else gc[0:1, i:i + 1])
                egc1[j].append(jnp.exp(gcol1[j][i]))
                kdsc[j].append(jnp.exp(glast[j][i] - gcol1[j][i]))
                egl_ref[d, i % HB, pl.ds(e0[j], 8), :] = jnp.broadcast_to(jnp.exp(glast[j][i]), (8, LANES))
        lm, a_p = [], []
        for j in G:
            bcol_p = jnp.concatenate([jnp.broadcast_to(b, (CHUNK, CHUNK)) for b in bcol1[j]], axis=1)
            diff = jnp.where(is_fwd, dif2[j][:CHUNK], dif2[j][CHUNK:])
            dec = jnp.where(incl_p, jnp.exp(jnp.where(incl_p, diff, 0.0)), 0.0)
            kk_p = jnp.concatenate([gram[j][i % HB][:CHUNK] for i in range(NCH)], axis=1)
            qk_p = jnp.concatenate([gram[j][i % HB][CHUNK:2 * CHUNK] for i in range(NCH)], axis=1)
            lm.append(jnp.where(strict_p, bcol_p * kk_p * dec, 0.0))
            a_p.append(qk_p * dec)
        x = [eye_p - lm[j] for j in G]
        p = [_dot(lm[j].astype(BF16), _block_diag(lm[j], blk_masks)) for j in G]
        for _ in range(4):
            r = [_dot(jnp.concatenate([x[j], p[j]], axis=0).astype(BF16), _block_diag(p[j], blk_masks)) for j in G]
            x = [x[j] + r[j][:CHUNK] for j in G]
            p = [r[j][CHUNK:] for j in G]
        r = [_dot(x[j].astype(BF16), _block_diag(p[j], blk_masks)) for j in G]
        x = [x[j] + r[j] for j in G]
        rhs = [jnp.concatenate(
            [jnp.concatenate([vh[j][i % HB] * bcol1[j][i], kh[j][i % HB] * (bcol1[j][i] * egc1[j][i])], axis=1)
             for i in range(NCH)], axis=0).astype(BF16) for j in G]
        uw = [_dot(_block_diag(x[j], blk_masks), rhs[j]) for j in G]
        ao = [_dot(_block_diag(a_p[j], blk_masks), uw[j].astype(BF16)) for j in G]
        mb = [[_dot(gram[j][i % HB][2 * CHUNK:].astype(BF16),
                    (uw[j][i * CHUNK:(i + 1) * CHUNK] * kdsc[j][i]).astype(BF16)) for i in range(NCH)] for j in G]
        for j in G:
            for hh in range(HB):
                hs = slice(hh * GDN_DIM, (hh + 1) * GDN_DIM)
                o0 = jnp.zeros((CHUNK, GDN_DIM), F32)
                for d in range(2):
                    i = d * HB + hh
                    rs = slice(i * CHUNK, (i + 1) * CHUNK)
                    b_ref[d, hh, pl.ds(m0[j], GDN_DIM), :] = mb[j][i][:, :GDN_DIM].astype(BF16)
                    m_ref[d, hh, pl.ds(m0[j], GDN_DIM), :] = mb[j][i][:, GDN_DIM:].astype(BF16)
                    qt_ref[d, pl.ds(r0[j], CHUNK), hs] = (qh[j][hh] * egc1[j][i] - ao[j][rs, GDN_DIM:]).astype(BF16)
                    o0 = o0 + ao[j][rs, :GDN_DIM]
                o_ref[pl.ds(r0[j], CHUNK), hs] = o0
        return carry

    lax.fori_loop(0, NC // GCH, phase1, 0)

    s_ref[...] = jnp.zeros(s_ref.shape, F32)

    def phase2(step, carry):
        cb = jnp.where(step < NCC, NCC - 1 - step, NC + NCC - 1 - step)
        chains = [(d, hh, c) for d, c in ((0, step), (1, cb)) for hh in range(HB)]
        rows = [pl.ds(pl.multiple_of(c * CHUNK, CHUNK), CHUNK) for _, _, c in chains]
        mrows = [pl.ds(pl.multiple_of(c * GDN_DIM, GDN_DIM), GDN_DIM) for _, _, c in chains]
        hs = [slice(hh * GDN_DIM, (hh + 1) * GDN_DIM) for _, hh, _ in chains]
        n = range(len(chains))
        st = [s_ref[chains[i][0], chains[i][1]] for i in n]
        stb = [st[i].astype(BF16) for i in n]
        ms = [_dot(m_ref[chains[i][0], chains[i][1], mrows[i], :], stb[i]) for i in n]
        oq = [_dot(qt_ref[chains[i][0], rows[i], hs[i]], stb[i]) for i in n]
        for i in n:
            d, hh, c = chains[i]
            egl = egl_ref[d, hh, pl.ds(pl.multiple_of(c * 8, 8), 1), :]
            s_ref[d, hh] = st[i] * egl - ms[i] + b_ref[d, hh, mrows[i], :].astype(F32)
            o_ref[rows[i], hs[i]] += oq[i]
        return carry

    lax.fori_loop(0, NC, phase2, 0, unroll=4)

    nw = nw_ref[...]
    for ti in range(NT):
        rs = slice(ti * TT, (ti + 1) * TT)
        o = o_ref[rs, :]
        zz = _silu(z_ref[0, rs, :])
        for hh in range(HB):
            hs = slice(hh * GDN_DIM, (hh + 1) * GDN_DIM)
            oh = o[:, hs]
            oh = oh * lax.rsqrt(jnp.mean(oh * oh, axis=-1, keepdims=True) + RMS_EPS) * nw
            y_ref[0, rs, hs] = (oh * zz[:, hs]).astype(BF16)


def _gdn(p_main, conv_w, ab, alog_g, dtb_g, norm_w):
    nhb = GDN_WIDTH // HW
    blk = lambda off: pl.BlockSpec((1, T, HW), lambda b, g: (b, 0, off * nhb + g))
    cblk = lambda off: pl.BlockSpec((GDN_CONV, HW), lambda b, g: (0, off * nhb + g))
    return pl.pallas_call(
        _gdn_kernel,
        out_shape=jax.ShapeDtypeStruct((NB, T, GDN_WIDTH), BF16),
        grid=(NB, NG),
        in_specs=[
            blk(0), blk(1), blk(2), blk(3),
            cblk(0), cblk(1), cblk(2),
            pl.BlockSpec((1, T, LANES), lambda b, g: (b, 0, g)),
            pl.BlockSpec((1, 1, LANES), lambda b, g: (g, 0, 0)),
            pl.BlockSpec((1, 1, LANES), lambda b, g: (g, 0, 0)),
            pl.BlockSpec((1, GDN_DIM), lambda b, g: (0, 0)),
        ],
        out_specs=pl.BlockSpec((1, T, HW), lambda b, g: (b, 0, g)),
        scratch_shapes=[
            pltpu.VMEM((PAD_ROWS, HW), F32),
            pltpu.VMEM((T, HW), F32),
            pltpu.VMEM((T, HW), F32),
            pltpu.VMEM((T, HW), F32),
            pltpu.VMEM((2, HB, NC * GDN_DIM, GDN_DIM), BF16),
            pltpu.VMEM((2, HB, NC * GDN_DIM, GDN_DIM), BF16),
            pltpu.VMEM((2, T, HW), BF16),
            pltpu.VMEM((T, HW), F32),
            pltpu.VMEM((2, HB, NC * 8, LANES), F32),
            pltpu.VMEM((2, HB, GDN_DIM, GDN_DIM), F32),
        ],
        compiler_params=_cparams(("arbitrary", "arbitrary")),
        name="gdn",
    )(p_main, p_main, p_main, p_main, conv_w, conv_w, conv_w, ab, alog_g, dtb_g, norm_w)


NEG = -1e30


def _mix_kernel(n_act, ctx_tile0, *refs):
    act_refs = refs[:n_act]
    w_refs = refs[n_act:2 * n_act]
    rest = refs[2 * n_act:]
    b = pl.program_id(0)
    t = pl.program_id(1)
    if ctx_tile0:
        c_ref, x_ref, mod_ref, lng_ref, lnb_ref, wr_ref, br_ref, x1_ref, h2_ref, route_ref = rest
        row = jnp.where(t == 0, NB, b)
        xt = jnp.where(t == 0, c_ref[0], x_ref[0])
    else:
        x_ref, mod_ref, lng_ref, lnb_ref, wr_ref, br_ref, x1_ref, h2_ref, route_ref = rest
        row = b
        xt = x_ref[0]
    y = _dot(act_refs[0][0], w_refs[0][...])
    for a_ref, w_ref in zip(act_refs[1:], w_refs[1:]):
        y = y + _dot(a_ref[0], w_ref[...])
    x1 = _layer_norm(ALPHA * xt + _mod_slice(mod_ref, row, 2) * y, lng_ref[...], lnb_ref[...])
    x1_ref[0] = x1
    h2 = x1 * (1.0 + _mod_slice(mod_ref, row, 4)) + _mod_slice(mod_ref, row, 3)
    h2_ref[...] = _to_token_tiles(h2)
    logits = _dot3(h2, wr_ref[...]) + br_ref[...]
    lane = lax.broadcasted_iota(I32, (TT, LANES), 1)
    gl = jnp.where(lane < MOE_GROUPS, logits, NEG)
    gm = jnp.max(gl, axis=-1, keepdims=True)
    gsum = jnp.sum(jnp.exp(gl - gm), axis=-1, keepdims=True)
    g_val = 1.0 / gsum
    g_idx = jnp.min(jnp.where(gl == gm, lane, LANES), axis=-1, keepdims=True)
    sel = (lane >= MOE_GROUPS) & (lane < MOE_GROUPS + MOE_EXPERTS) & (((lane - MOE_GROUPS) // MOE_PER_GROUP) == g_idx)
    el = jnp.where(sel, logits, NEG)
    m1 = jnp.max(el, axis=-1, keepdims=True)
    i1 = jnp.min(jnp.where(el == m1, lane, LANES), axis=-1, keepdims=True)
    el2 = jnp.where(lane == i1, NEG, el)
    m2 = jnp.max(el2, axis=-1, keepdims=True)
    i2 = jnp.min(jnp.where(el2 == m2, lane, LANES), axis=-1, keepdims=True)
    ex2 = jnp.exp(m2 - m1)
    w1 = g_val / (1.0 + ex2)
    w2 = w1 * ex2
    e1 = (i1 - MOE_GROUPS).astype(F32)
    e2 = (i2 - MOE_GROUPS).astype(F32)
    route_ref[...] = jnp.where(lane == 0, e1, jnp.where(lane == 1, e2, jnp.where(lane == 2, w1, jnp.where(lane == 3, w2, 0.0))))


def _mix(acts, ws, streams, mod, ln_g, ln_b, wr, br, ctx_tile0):
    ntl = NT if ctx_tile0 else NTL
    if ctx_tile0:
        stream_specs = [pl.BlockSpec((1, CTX, D), lambda b, t: (b, 0, 0)),
                        pl.BlockSpec((1, TT, D), lambda b, t: (b, jnp.maximum(t - 1, 0), 0))]
    else:
        stream_specs = [pl.BlockSpec((1, TT, D), lambda b, t: (b, t + 1, 0))]
    n_tok = NB * ntl * TT
    act_specs = [pl.BlockSpec((1, TT, a.shape[2]), lambda b, t: (b, t, 0)) for a in acts]
    w_specs = [pl.BlockSpec(w.shape, lambda b, t: (0, 0)) for w in ws]
    return pl.pallas_call(
        functools.partial(_mix_kernel, len(acts), ctx_tile0),
        out_shape=(
            jax.ShapeDtypeStruct((NB, ntl * TT, D), F32),
            jax.ShapeDtypeStruct((n_tok, NBLK, LANES), F32),
            jax.ShapeDtypeStruct((n_tok, LANES), F32),
        ),
        grid=(NB, ntl),
        in_specs=act_specs + w_specs + stream_specs + [
            pl.BlockSpec((16, N_MOD * D), lambda b, t: (0, 0)),
            pl.BlockSpec((1, D), lambda b, t: (0, 0)),
            pl.BlockSpec((1, D), lambda b, t: (0, 0)),
            pl.BlockSpec((D, LANES), lambda b, t: (0, 0)),
            pl.BlockSpec((1, LANES), lambda b, t: (0, 0)),
        ],
        out_specs=(
            pl.BlockSpec((1, TT, D), lambda b, t: (b, t, 0)),
            pl.BlockSpec((TT, NBLK, LANES), lambda b, t: (b * ntl + t, 0, 0)),
            pl.BlockSpec((TT, LANES), lambda b, t: (b * ntl + t, 0)),
        ),
        compiler_params=_cparams(("arbitrary", "arbitrary")),
        name="mix_ctx" if ctx_tile0 else "mix_lat",
    )(*acts, *ws, *streams, mod, ln_g, ln_b, wr, br)


def _plan_kernel(route_ref, er_ref, cnt_ref, carry_ref):
    i = pl.program_id(0)

    @pl.when(i == 0)
    def _():
        carry_ref[...] = jnp.zeros(carry_ref.shape, F32)

    r = route_ref[...]
    lane = lax.broadcasted_iota(I32, (PB, LANES), 1)
    lanef = lane.astype(F32)
    ri = lax.broadcasted_iota(I32, (PB, PB), 0)
    ci = lax.broadcasted_iota(I32, (PB, PB), 1)
    before = (ci < ri).astype(BF16)
    oh1 = jnp.where(lanef == r[:, 0:1], 1.0, 0.0)
    oh2 = jnp.where(lanef == r[:, 1:2], 1.0, 0.0)
    c0 = carry_ref[0:1, :]
    tot1 = jnp.sum(oh1, axis=0, keepdims=True)
    tot2 = jnp.sum(oh2, axis=0, keepdims=True)
    r1 = _dot(before, oh1.astype(BF16)) + c0
    r2 = _dot(before, oh2.astype(BF16)) + (c0 + tot1)
    rank1 = jnp.sum(r1 * oh1, axis=-1, keepdims=True)
    rank2 = jnp.sum(r2 * oh2, axis=-1, keepdims=True)
    er = jnp.where(lane < 2, r, jnp.where(lane == 2, rank1, jnp.where(lane == 3, rank2, 0.0)))
    er_ref[...] = er.T[0:8, :]
    cnew = jnp.broadcast_to(c0 + tot1 + tot2, carry_ref.shape)
    carry_ref[...] = cnew
    cnt_ref[...] = cnew


def _plan(route):
    n_tok = route.shape[0]
    return pl.pallas_call(
        _plan_kernel,
        out_shape=(jax.ShapeDtypeStruct((8, n_tok), F32), jax.ShapeDtypeStruct((8, LANES), F32)),
        grid=(n_tok // PB,),
        in_specs=[pl.BlockSpec((PB, LANES), lambda i: (i, 0))],
        out_specs=(pl.BlockSpec((8, PB), lambda i: (0, i)), pl.BlockSpec((8, LANES), lambda i: (0, 0))),
        scratch_shapes=[pltpu.VMEM((8, LANES), F32)],
        compiler_params=_cparams(("arbitrary",)),
        name="moe_plan",
    )(route)


INV_UNROLL = 16


def _invmap_kernel(pos_ref, src_ref):
    n_rows = src_ref.shape[0]
    n_tok = pos_ref.shape[0] // 2

    def clear(g, c):
        for u in range(INV_UNROLL):
            src_ref[g * INV_UNROLL + u] = 0
        return c

    lax.fori_loop(0, n_rows // INV_UNROLL, clear, 0)

    def put(g, c):
        for u in range(INV_UNROLL):
            t = g * INV_UNROLL + u
            src_ref[pos_ref[t]] = t
            src_ref[pos_ref[n_tok + t]] = t
        return c

    lax.fori_loop(0, n_tok // INV_UNROLL, put, 0)


def _invmap(pos_flat, n_rows):
    return pl.pallas_call(
        _invmap_kernel,
        out_shape=jax.ShapeDtypeStruct((n_rows,), I32),
        in_specs=[pl.BlockSpec(memory_space=pltpu.SMEM)],
        out_specs=pl.BlockSpec(memory_space=pltpu.SMEM),
        name="moe_invmap",
    )(pos_flat)


def _ffn_kernel(te_ref, nu_ref, src_ref, h_hbm, wg_ref, wu_ref, wd_ref, y_ref, buf, wgb, wub, wdb, sem):
    i = pl.program_id(0)
    nu = nu_ref[0]

    def issue(tile, slot, lo, hi):
        base = tile * TM
        for r in range(lo, hi):
            pltpu.make_async_copy(h_hbm.at[src_ref[base + r]], buf.at[slot, r], sem.at[slot]).start(priority=r % 2)

    def ffn_tile(slot, next_tile, next_slot):
        per = TM // FFN_DMA_GROUPS
        sent = [0]

        def gather_some():
            if next_tile is not None and sent[0] < FFN_DMA_GROUPS:
                issue(next_tile, next_slot, sent[0] * per, (sent[0] + 1) * per)
                sent[0] += 1

        x = _from_token_tiles(buf[slot]).astype(BF16)
        hcol = MOE_FF // 2
        gather_some()
        parts = []
        for w in (wgb, wub):
            for c in range(2):
                parts.append(_dot(x, w[:, c * hcol:(c + 1) * hcol]))
                gather_some()
        a = jnp.concatenate(parts[0:2], axis=1)
        bb = jnp.concatenate(parts[2:4], axis=1)
        act = (_silu(a) * bb).astype(BF16)
        ys = []
        for c in range(D // hcol):
            ys.append(_dot(act, wdb[:, c * hcol:(c + 1) * hcol]))
            gather_some()
        assert next_tile is None or sent[0] == FFN_DMA_GROUPS
        y_ref[...] = _to_token_tiles(jnp.concatenate(ys, axis=1))

    @pl.when(i == 0)
    def _():
        issue(0, 0, 0, TM)

        for j in range(1, FFN_BUFS - 1):
            @pl.when(nu > j)
            def _():
                issue(j, j, 0, TM)

    @pl.when(i < nu)
    def _():
        slot = i % FFN_BUFS
        pltpu.make_async_copy(h_hbm.at[pl.ds(0, TM)], buf.at[slot], sem.at[slot]).wait()
        e = te_ref[i]
        ep = te_ref[jnp.maximum(i - 1, 0)]

        @pl.when((i == 0) | (e != ep))
        def _():
            wgb[...] = wg_ref[0, 0].astype(BF16)
            wub[...] = wu_ref[0, 0].astype(BF16)
            wdb[...] = wd_ref[0, 0].astype(BF16)

        ahead = i + (FFN_BUFS - 1)

        @pl.when(ahead < nu)
        def _():
            ffn_tile(slot, ahead, ahead % FFN_BUFS)

        @pl.when(ahead >= nu)
        def _():
            ffn_tile(slot, None, None)

    @pl.when(i >= nu)
    def _():
        y_ref[...] = jnp.zeros(y_ref.shape, F32)


def _ffn(tile_expert, n_used, src, h2, wg, wu, wd, layer):
    n_rows = src.shape[0]
    n_tiles = n_rows // TM
    wspec = lambda shape: pl.BlockSpec((1, 1) + shape, lambda i, te, nu, src: (layer, te[i], 0, 0))
    return pl.pallas_call(
        _ffn_kernel,
        out_shape=jax.ShapeDtypeStruct((n_rows, NBLK, LANES), F32),
        grid_spec=pltpu.PrefetchScalarGridSpec(
            num_scalar_prefetch=3,
            grid=(n_tiles,),
            in_specs=[
                pl.BlockSpec(memory_space=pl.ANY),
                wspec((D, MOE_FF)), wspec((D, MOE_FF)), wspec((MOE_FF, D)),
            ],
            out_specs=pl.BlockSpec((TM, NBLK, LANES), lambda i, te, nu, src: (i, 0, 0)),
            scratch_shapes=[
                pltpu.VMEM((FFN_BUFS, TM, NBLK, LANES), F32),
                pltpu.VMEM((D, MOE_FF), BF16),
                pltpu.VMEM((D, MOE_FF), BF16),
                pltpu.VMEM((MOE_FF, D), BF16),
                pltpu.SemaphoreType.DMA((FFN_BUFS,)),
            ],
        ),
        compiler_params=_cparams(("arbitrary",)),
        name="moe_ffn",
    )(tile_expert, n_used, src, h2, wg, wu, wd)


def _combine_kernel(ntl, ctx_tile0, pos_ref, y_hbm, x1_ref, route_ref, mod_ref, lng_ref, lnb_ref, o_ref, buf, sem):
    b = pl.program_id(0)
    t = pl.program_id(1)
    flat = b * ntl + t
    n_tiles = NB * ntl

    def issue(tile, slot):
        base = tile * TT
        for r in range(TT):
            for k in range(2):
                pltpu.make_async_copy(y_hbm.at[pos_ref[k * (n_tiles * TT) + base + r]], buf.at[slot, k * TT + r],
                                      sem.at[slot]).start(priority=k)

    @pl.when(flat == 0)
    def _():
        issue(0, 0)

    slot = flat % 2
    pltpu.make_async_copy(y_hbm.at[pl.ds(0, 2 * TT)], buf.at[slot], sem.at[slot]).wait()

    @pl.when(flat + 1 < n_tiles)
    def _():
        issue(flat + 1, 1 - slot)

    row = jnp.where(t == 0, NB, b) if ctx_tile0 else b
    r = route_ref[...]
    f = (r[:, 2:3] * _from_token_tiles(buf[slot, 0:TT]) + r[:, 3:4] * _from_token_tiles(buf[slot, TT:2 * TT]))
    o_ref[0] = _layer_norm(ALPHA * x1_ref[0] + _mod_slice(mod_ref, row, 5) * f, lng_ref[...], lnb_ref[...])


def _combine(pos_flat, ys, x1, route, mod, ln_g, ln_b, ctx_tile0):
    ntl = NT if ctx_tile0 else NTL
    return pl.pallas_call(
        functools.partial(_combine_kernel, ntl, ctx_tile0),
        out_shape=jax.ShapeDtypeStruct((NB, ntl * TT, D), F32),
        grid_spec=pltpu.PrefetchScalarGridSpec(
            num_scalar_prefetch=1,
            grid=(NB, ntl),
            in_specs=[
                pl.BlockSpec(memory_space=pl.ANY),
                pl.BlockSpec((1, TT, D), lambda b, t, pos: (b, t, 0)),
                pl.BlockSpec((TT, LANES), lambda b, t, pos: (b * ntl + t, 0)),
                pl.BlockSpec((16, N_MOD * D), lambda b, t, pos: (0, 0)),
                pl.BlockSpec((1, D), lambda b, t, pos: (0, 0)),
                pl.BlockSpec((1, D), lambda b, t, pos: (0, 0)),
            ],
            out_specs=pl.BlockSpec((1, TT, D), lambda b, t, pos: (b, t, 0)),
            scratch_shapes=[pltpu.VMEM((2, 2 * TT, NBLK, LANES), F32), pltpu.SemaphoreType.DMA((2,))],
        ),
        compiler_params=_cparams(("arbitrary", "arbitrary")),
        name="moe_combine_ctx" if ctx_tile0 else "moe_combine_lat",
    )(pos_flat, ys, x1, route, mod, ln_g, ln_b)


def _moe(h2, route, x1, mod, ln_g, ln_b, wg, wu, wd, layer, ctx_tile0):
    n_tok = h2.shape[0]
    n_tiles = (2 * n_tok) // TM + MOE_EXPERTS
    er, cnt = _plan(route)
    counts = cnt[0, :MOE_EXPERTS].astype(I32)
    tiles_e = (counts + TM - 1) // TM
    tile_end = jnp.cumsum(tiles_e)
    tile_start = tile_end - tiles_e
    n_used = tile_end[-1]
    tj = jnp.arange(n_tiles, dtype=I32)
    te = jnp.minimum(jnp.sum((tile_end[None, :] <= tj[:, None]).astype(I32), axis=1), MOE_EXPERTS - 1)
    te_last = jnp.max(jnp.where(tj < n_used, te, 0))
    tile_expert = jnp.where(tj < n_used, te, te_last)
    eid = er[0:2].astype(I32)
    first_row = jnp.sum(jnp.where(eid[None] == jnp.arange(MOE_EXPERTS, dtype=I32)[:, None, None],
                                  (tile_start * TM)[:, None, None], 0), axis=0)
    pos_flat = (first_row + er[2:4].astype(I32)).reshape(-1)
    src = _invmap(pos_flat, n_tiles * TM)
    ys = _ffn(tile_expert, n_used.reshape(1), src, h2, wg, wu, wd, layer)
    return _combine(pos_flat, ys, x1, route, mod, ln_g, ln_b, ctx_tile0)


W_IN1 = MLA_KV_LORA + MLA_ROPE + MLA_Q_LORA + MLA_ROPE
Q_OFF = MLA_KV_LORA + MLA_ROPE
KRS_OFF = Q_OFF + MLA_Q_LORA
SM_SCALE = MLA_QK ** -0.5


def _rms(v, w):
    return v * lax.rsqrt(jnp.mean(v * v, axis=-1, keepdims=True) + RMS_EPS) * w


def _mla_proj_kernel(x_ref, mod_ref, win_ref, kvn_ref, qnw_ref, wkn_ref, wkv_ref, wqn_ref, wqr_ref, wqrs_ref,
                     cosk_ref, sink_ref, cosq_ref, sinq_ref, q_out, k_out, v_out):
    b = pl.program_id(0)
    t = pl.program_id(1)
    row = jnp.where(t == 0, NB, b)
    h = x_ref[0] * (1.0 + _mod_slice(mod_ref, row, 1)) + _mod_slice(mod_ref, row, 0)
    p = _dot(h.astype(BF16), win_ref[...])
    ckv = _rms(p[:, :MLA_KV_LORA], kvn_ref[...]).astype(BF16)
    kn = _dot(ckv, wkn_ref[...])
    vv = _dot(ckv, wkv_ref[...])
    kr = p[:, MLA_KV_LORA:Q_OFF]
    krs = p[:, KRS_OFF:KRS_OFF + MLA_ROPE]
    kr = jnp.where(t > 0, kr * cosk_ref[...] + krs * sink_ref[...], kr).astype(BF16)
    ql = _rms(p[:, Q_OFF:KRS_OFF], qnw_ref[...]).astype(BF16)
    qn = _dot(ql, wqn_ref[...]) * SM_SCALE
    qr = (_dot(ql, wqr_ref[...]) * cosq_ref[...] + _dot(ql, wqrs_ref[...]) * sinq_ref[...]) * SM_SCALE
    for hd in range(MLA_HEADS):
        k_out[0, hd, :, 0:MLA_NOPE] = kn[:, hd * MLA_NOPE:(hd + 1) * MLA_NOPE].astype(BF16)
        k_out[0, hd, :, MLA_NOPE:MLA_QK] = kr
        v_out[0, hd] = vv[:, hd * MLA_V:(hd + 1) * MLA_V].T.astype(BF16)
        q_out[0, hd, :, 0:MLA_NOPE] = qn[:, hd * MLA_NOPE:(hd + 1) * MLA_NOPE].astype(BF16)
        q_out[0, hd, :, MLA_NOPE:MLA_QK] = qr[:, hd * MLA_ROPE:(hd + 1) * MLA_ROPE].astype(BF16)


def _mla_proj(xs, mod, win, kvn, qnw, wkn, wkv, wqn, wqr, wqrs, cosk, sink, cosq, sinq):
    full = lambda a: pl.BlockSpec(a.shape, lambda b, t: (0, 0))
    lat = lambda w: pl.BlockSpec((TT, w), lambda b, t: (jnp.maximum(t - 1, 0), 0))
    return pl.pallas_call(
        _mla_proj_kernel,
        out_shape=(
            jax.ShapeDtypeStruct((NB, MLA_HEADS, SEQ, MLA_QK), BF16),
            jax.ShapeDtypeStruct((NB, MLA_HEADS, T, MLA_QK), BF16),
            jax.ShapeDtypeStruct((NB, MLA_HEADS, MLA_V, T), BF16),
        ),
        grid=(NB, NT),
        in_specs=[
            pl.BlockSpec((1, TT, D), lambda b, t: (b, t, 0)),
            pl.BlockSpec((16, N_MOD * D), lambda b, t: (0, 0)),
            full(win), full(kvn), full(qnw), full(wkn), full(wkv), full(wqn), full(wqr), full(wqrs),
            lat(MLA_ROPE), lat(MLA_ROPE), lat(MLA_HEADS * MLA_ROPE), lat(MLA_HEADS * MLA_ROPE),
        ],
        out_specs=(
            pl.BlockSpec((1, MLA_HEADS, TT, MLA_QK), lambda b, t: (b, 0, jnp.maximum(t - 1, 0), 0)),
            pl.BlockSpec((1, MLA_HEADS, TT, MLA_QK), lambda b, t: (b, 0, t, 0)),
            pl.BlockSpec((1, MLA_HEADS, MLA_V, TT), lambda b, t: (b, 0, 0, t)),
        ),
        compiler_params=_cparams(("arbitrary", "arbitrary")),
        name="mla_proj",
    )(xs, mod, win, kvn, qnw, wkn, wkv, wqn, wqr, wqrs, cosk, sink, cosq, sinq)


AH = 2


def _attn_kernel(q_ref, k_ref, vt_ref, o_ref, s_ref):
    nq = SEQ // TQ

    def scores(j):
        for a in range(AH):
            s_ref[a, j % 2] = _dot_nt(k_ref[0, a], q_ref[0, a, j * TQ:(j + 1) * TQ, :])

    scores(0)
    for j in range(nq):
        if j + 1 < nq:
            scores(j + 1)
        ps, ls = [], []
        for a in range(AH):
            s = s_ref[a, j % 2]
            p = jnp.exp(s - jnp.max(s, axis=0, keepdims=True))
            ls.append(jnp.sum(p, axis=0, keepdims=True))
            ps.append(p.astype(BF16))
        ots = [_dot(vt_ref[0, a], ps[a]) / ls[a] for a in range(AH)]
        for a in range(AH):
            o_ref[0, j * TQ:(j + 1) * TQ, a * MLA_V:(a + 1) * MLA_V] = ots[a].T.astype(BF16)


def _attn(q, k, vt):
    return pl.pallas_call(
        _attn_kernel,
        out_shape=jax.ShapeDtypeStruct((NB, SEQ, MLA_HEADS * MLA_V), BF16),
        grid=(NB, MLA_HEADS // AH),
        in_specs=[
            pl.BlockSpec((1, AH, SEQ, MLA_QK), lambda b, h: (b, h, 0, 0)),
            pl.BlockSpec((1, AH, T, MLA_QK), lambda b, h: (b, h, 0, 0)),
            pl.BlockSpec((1, AH, MLA_V, T), lambda b, h: (b, h, 0, 0)),
        ],
        out_specs=pl.BlockSpec((1, SEQ, AH * MLA_V), lambda b, h: (b, 0, h)),
        scratch_shapes=[pltpu.VMEM((AH, 2, T, TQ), F32)],
        compiler_params=_cparams(("arbitrary", "arbitrary")),
        name="mla_attn",
    )(q, k, vt)


def _rope_tables():
    n = SEQ
    rowp = (jnp.arange(n) // GRID_W).astype(F32)
    colp = (jnp.arange(n) % GRID_W).astype(F32)
    inv_freq = ROPE_BASE ** (-jnp.arange(ROPE_F, dtype=F32) / ROPE_F)
    ar = rowp[:, None] * inv_freq
    ac = colp[:, None] * inv_freq
    cos = jnp.concatenate([jnp.cos(ar), jnp.cos(ar), jnp.cos(ac), jnp.cos(ac)], axis=1)
    sin = jnp.concatenate([-jnp.sin(ar), jnp.sin(ar), -jnp.sin(ac), jnp.sin(ac)], axis=1)
    return cos, sin


def _rope_swap_perm():
    f = ROPE_F
    return jnp.concatenate([jnp.arange(f, 2 * f), jnp.arange(0, f), jnp.arange(3 * f, 4 * f), jnp.arange(2 * f, 3 * f)])


def _router_params(w_group, b_group, w_expert, b_expert):
    pad = LANES - MOE_GROUPS - MOE_EXPERTS
    wr = jnp.concatenate([w_group, w_expert, jnp.zeros((D, pad), F32)], axis=1)
    br = jnp.concatenate([b_group, b_expert, jnp.zeros((pad,), F32)]).reshape(1, LANES)
    return wr, br


def kernel(x, c, ctx, c_ctx, ada_w, ada_b, ln_g, ln_b, ab_w_in, pool_w, pool_scale, gdn_conv_w, gdn_a_log, gdn_dt_bias, gdn_norm_w, ab_w_out, mla_w_in, mla_kv_norm, mla_w_ukv, mla_q_norm, mla_w_uq, mla_w_out, moe_w_group, moe_b_group, moe_w_expert, moe_b_expert, moe_w_gate, moe_w_up, moe_w_down):
    assert x.shape == (NB, SEQ, D) and ctx.shape == (NB, CTX, D)
    cv = jnp.concatenate([c, c_ctx[None, :], jnp.zeros((16 - NB - 1, D), F32)], axis=0)
    mod = _ada(cv, ada_w, ada_b)

    w_in = ab_w_in[0]
    w_main = w_in[:, :W_MAIN].astype(BF16)
    o_ab = W_MAIN
    ab_cols = []
    for g in range(NG):
        idx = [o_ab + kind * 2 * GDN_HEADS + d * GDN_HEADS + g * HB + hh
               for kind in range(2) for d in range(2) for hh in range(HB)]
        ab_cols.append(jnp.concatenate([w_in[:, jnp.array(idx)], jnp.zeros((D, LANES - len(idx)), F32)], axis=1))
    w_ab = jnp.concatenate(ab_cols, axis=1)

    def gate_rows(p):
        rows = []
        for g in range(NG):
            vals = jnp.stack([p[d, g * HB + hh] for d in range(2) for hh in range(HB)])
            rows.append(jnp.concatenate([vals, jnp.zeros((LANES - 2 * HB,), F32)]))
        return jnp.stack(rows).reshape(NG, 1, LANES)

    alog_g = gate_rows(gdn_a_log[0])
    dtb_g = gate_rows(gdn_dt_bias[0])
    wbd = jax.scipy.linalg.block_diag(*[pool_w[0, g] for g in range(len(POOL_WINDOWS))]).astype(BF16)

    pool_u, p_main, ab = _inproj0(ctx, x, mod[0], w_main, w_ab)
    pool_y = _pool(pool_u, wbd, pool_scale[0].reshape(1, POOL_WIDTH))
    gdn_y = _gdn(p_main, gdn_conv_w[0], ab, alog_g, dtb_g, gdn_norm_w[0].reshape(1, GDN_DIM))

    w_out0 = ab_w_out[0].astype(BF16)
    wr0, br0 = _router_params(moe_w_group[0], moe_b_group[0], moe_w_expert[0], moe_b_expert[0])
    x1, h2, route = _mix([pool_y, gdn_y], [w_out0[:POOL_WIDTH], w_out0[POOL_WIDTH:]], (ctx, x), mod[0],
                         ln_g[0, 0].reshape(1, D), ln_b[0, 0].reshape(1, D), wr0, br0, True)
    xs = _moe(h2, route, x1, mod[0], ln_g[0, 1].reshape(1, D), ln_b[0, 1].reshape(1, D),
              moe_w_gate, moe_w_up, moe_w_down, 0, True)

    perm = _rope_swap_perm()
    w_in1 = mla_w_in[0]
    win = jnp.concatenate([w_in1, w_in1[:, MLA_KV_LORA:Q_OFF][:, perm]], axis=1).astype(BF16)
    ukv = mla_w_ukv[0].reshape(MLA_KV_LORA, MLA_HEADS, MLA_NOPE + MLA_V)
    wkn = ukv[:, :, :MLA_NOPE].reshape(MLA_KV_LORA, MLA_HEADS * MLA_NOPE).astype(BF16)
    wkv = ukv[:, :, MLA_NOPE:].reshape(MLA_KV_LORA, MLA_HEADS * MLA_V).astype(BF16)
    uq = mla_w_uq[0].reshape(MLA_Q_LORA, MLA_HEADS, MLA_QK)
    wqn = uq[:, :, :MLA_NOPE].reshape(MLA_Q_LORA, MLA_HEADS * MLA_NOPE).astype(BF16)
    wqr = uq[:, :, MLA_NOPE:].reshape(MLA_Q_LORA, MLA_HEADS * MLA_ROPE).astype(BF16)
    wqrs = uq[:, :, MLA_NOPE:][:, :, perm].reshape(MLA_Q_LORA, MLA_HEADS * MLA_ROPE).astype(BF16)
    cosk, sink = _rope_tables()
    cosq = jnp.tile(cosk, (1, MLA_HEADS))
    sinq = jnp.tile(sink, (1, MLA_HEADS))

    q, k, v = _mla_proj(xs, mod[1], win, mla_kv_norm[0].reshape(1, MLA_KV_LORA), mla_q_norm[0].reshape(1, MLA_Q_LORA),
                        wkn, wkv, wqn, wqr, wqrs, cosk, sink, cosq, sinq)
    att = _attn(q, k, v)
    wr1, br1 = _router_params(moe_w_group[1], moe_b_group[1], moe_w_expert[1], moe_b_expert[1])
    x1, h2, route = _mix([att], [mla_w_out[0].astype(BF16)], (xs,), mod[1],
                         ln_g[1, 0].reshape(1, D), ln_b[1, 0].reshape(1, D), wr1, br1, False)
    return _moe(h2, route, x1, mod[1], ln_g[1, 1].reshape(1, D), ln_b[1, 1].reshape(1, D),
                moe_w_gate, moe_w_up, moe_w_down, 1, False)
```

```python
import functools
import math

import jax
import jax.numpy as jnp
from jax import lax
from jax.experimental import pallas as pl
from jax.experimental.pallas import tpu as pltpu

F32 = jnp.float32
BF16 = jnp.bfloat16
I32 = jnp.int32

D = 1024
NB = 8
SEQ = 2048
CTX = 256
T = SEQ + CTX
DEPTH = 2
N_MOD = 6
ALPHA = (2 * DEPTH) ** 0.25
LN_EPS = 1e-5
RMS_EPS = 1e-6

POOL_WINDOWS = (2, 4, 8, 16)
POOL_WIDTH = 256
POOL_GROUP_DIM = 64
GDN_HEADS = 6
GDN_DIM = 128
GDN_WIDTH = GDN_HEADS * GDN_DIM
GDN_CONV = 4
CHUNK = 64

MLA_HEADS = 8
MLA_NOPE = 128
MLA_ROPE = 64
MLA_V = 128
MLA_QK = MLA_NOPE + MLA_ROPE
MLA_Q_LORA = 384
MLA_KV_LORA = 256
GRID_W = 64
ROPE_BASE = 10000.0
ROPE_F = MLA_ROPE // 4

MOE_GROUPS = 4
MOE_PER_GROUP = 8
MOE_EXPERTS = 32
MOE_FF = 512

TT = CTX
NT = T // TT
NTL = SEQ // TT
TM = 256
FFN_DMA_GROUPS = 8
FFN_BUFS = 6
PB = 512
HB = 2
NG = GDN_HEADS // HB
NC = T // CHUNK
NCC = CTX // CHUNK
TQ = 256
LANES = 128
VMEM_LIMIT = 56 * 1024 * 1024


def _dot(a, b):
    return jnp.dot(a, b, preferred_element_type=F32)


def _dot_nt(a, b):
    return lax.dot_general(a, b, (((1,), (1,)), ((), ())), preferred_element_type=F32)


def _dot_tn(a, b):
    return lax.dot_general(a, b, (((0,), (0,)), ((), ())), preferred_element_type=F32)


def _split2(x):
    hi = x.astype(BF16)
    lo = (x - hi.astype(F32)).astype(BF16)
    return hi, lo


def _dot3(a, b):
    ah, al = _split2(a)
    bh, bl = _split2(b)
    return _dot(ah, bh) + (_dot(ah, bl) + _dot(al, bh))


def _silu(x):
    return x * jax.nn.sigmoid(x)


def _softplus(x):
    return jnp.maximum(x, 0.0) + jnp.log1p(jnp.exp(-jnp.abs(x)))


def _layer_norm(v, g, b):
    mu = jnp.mean(v, axis=-1, keepdims=True)
    c = v - mu
    var = jnp.mean(c * c, axis=-1, keepdims=True)
    return c * lax.rsqrt(var + LN_EPS) * g + b


def _mod_slice(mod_ref, row, k):
    return mod_ref[pl.ds(row, 1), k * D:(k + 1) * D]


NBLK = D // LANES


def _to_token_tiles(y):
    return jnp.transpose(jnp.stack([y[:, s * LANES:(s + 1) * LANES] for s in range(NBLK)], axis=0), (1, 0, 2))


def _from_token_tiles(x3):
    xt = jnp.transpose(x3, (1, 0, 2))
    return jnp.concatenate([xt[s] for s in range(NBLK)], axis=1)


def _cparams(sem, vmem=VMEM_LIMIT):
    return pltpu.CompilerParams(dimension_semantics=sem, vmem_limit_bytes=vmem)


def _ada_kernel(cv_ref, w_ref, b_ref, o_ref):
    s = _silu(cv_ref[...])
    o_ref[0] = _dot3(s, w_ref[0]) + b_ref[0]


def _ada(cv, ada_w, ada_b):
    nblk = N_MOD
    return pl.pallas_call(
        _ada_kernel,
        out_shape=jax.ShapeDtypeStruct((DEPTH, 16, N_MOD * D), F32),
        grid=(DEPTH, nblk),
        in_specs=[
            pl.BlockSpec((16, D), lambda l, j: (0, 0)),
            pl.BlockSpec((1, D, D), lambda l, j: (l, 0, j)),
            pl.BlockSpec((1, 1, D), lambda l, j: (l, 0, j)),
        ],
        out_specs=pl.BlockSpec((1, 16, D), lambda l, j: (l, 0, j)),
        compiler_params=_cparams(("arbitrary", "arbitrary")),
        name="ada_mod",
    )(cv, ada_w, ada_b.reshape(DEPTH, 1, N_MOD * D))


W_MAIN = POOL_WIDTH + 4 * GDN_WIDTH
W_AB = NG * LANES


def _inproj0_kernel(c_ref, x_ref, mod_ref, w_ref, wab_ref, pool_ref, main_ref, ab_ref):
    b = pl.program_id(0)
    t = pl.program_id(1)
    row = jnp.where(t == 0, NB, b)
    xt = jnp.where(t == 0, c_ref[0], x_ref[0])
    h = xt * (1.0 + _mod_slice(mod_ref, row, 1)) + _mod_slice(mod_ref, row, 0)
    p = _dot(h.astype(BF16), w_ref[...])
    pool_ref[0] = p[:, :POOL_WIDTH]
    main_ref[0] = p[:, POOL_WIDTH:]
    ab_ref[0] = _dot3(h, wab_ref[...])


def _inproj0(ctx, x, mod, w_main, w_ab):
    return pl.pallas_call(
        _inproj0_kernel,
        out_shape=(
            jax.ShapeDtypeStruct((NB, T, POOL_WIDTH), F32),
            jax.ShapeDtypeStruct((NB, T, 4 * GDN_WIDTH), F32),
            jax.ShapeDtypeStruct((NB, T, W_AB), F32),
        ),
        grid=(NB, NT),
        in_specs=[
            pl.BlockSpec((1, CTX, D), lambda b, t: (b, 0, 0)),
            pl.BlockSpec((1, TT, D), lambda b, t: (b, jnp.maximum(t - 1, 0), 0)),
            pl.BlockSpec((16, N_MOD * D), lambda b, t: (0, 0)),
            pl.BlockSpec((D, W_MAIN), lambda b, t: (0, 0)),
            pl.BlockSpec((D, W_AB), lambda b, t: (0, 0)),
        ],
        out_specs=(
            pl.BlockSpec((1, TT, POOL_WIDTH), lambda b, t: (b, t, 0)),
            pl.BlockSpec((1, TT, 4 * GDN_WIDTH), lambda b, t: (b, t, 0)),
            pl.BlockSpec((1, TT, W_AB), lambda b, t: (b, t, 0)),
        ),
        compiler_params=_cparams(("arbitrary", "arbitrary")),
        name="inproj0",
    )(ctx, x, mod, w_main, w_ab)


PAD_GAP = 16
PAD_CTX = PAD_GAP
PAD_LAT = PAD_CTX + CTX + 2 * PAD_GAP
PAD_ROWS = PAD_LAT + SEQ + PAD_GAP


def _fill_padded(pad_ref, src):
    w = pad_ref.shape[1]
    pad_ref[0:PAD_CTX, :] = jnp.zeros((PAD_CTX, w), F32)
    pad_ref[PAD_CTX + CTX:PAD_LAT, :] = jnp.zeros((2 * PAD_GAP, w), F32)
    pad_ref[PAD_LAT + SEQ:PAD_ROWS, :] = jnp.zeros((PAD_GAP, w), F32)
    pad_ref[PAD_CTX:PAD_CTX + CTX, :] = src(0, CTX)
    pad_ref[PAD_LAT:PAD_LAT + SEQ, :] = src(CTX, SEQ)


def _tile_pad_row(ti):
    return PAD_CTX if ti == 0 else PAD_LAT + (ti - 1) * TT


def _pool_kernel(u_ref, wbd_ref, scale_ref, o_ref, pad_ref):
    _fill_padded(pad_ref, lambda s, n: u_ref[0, s:s + n, :])
    lane = lax.broadcasted_iota(I32, (1, POOL_WIDTH), 1)
    grp = lane // POOL_GROUP_DIM
    win = jnp.zeros((1, POOL_WIDTH), I32)
    for g, w in enumerate(POOL_WINDOWS):
        win = jnp.where(grp == g, w, win)
    left = win // 2
    right = win - 1 - left
    for ti in range(NT):
        seg_len = CTX if ti == 0 else SEQ
        seg_t0 = 0 if ti == 0 else (ti - 1) * TT
        prow = _tile_pad_row(ti)
        tpos = seg_t0 + lax.broadcasted_iota(I32, (TT, 1), 0)
        acc = jnp.zeros((TT, POOL_WIDTH), F32)
        for j in range(-max(POOL_WINDOWS) // 2, max(POOL_WINDOWS) // 2):
            inwin = (j >= -left) & (j <= right)
            acc = acc + jnp.where(inwin, pad_ref[pl.ds(prow + j, TT), :], 0.0)
        cnt = jnp.minimum(tpos + right + 1, seg_len) - jnp.maximum(tpos - left, 0)
        dlt = acc / cnt.astype(F32) - pad_ref[pl.ds(prow, TT), :]
        y = _dot(dlt.astype(BF16), wbd_ref[...]) * scale_ref[...]
        o_ref[0, ti * TT:(ti + 1) * TT, :] = y.astype(BF16)


def _pool(pool_u, wbd, scale):
    return pl.pallas_call(
        _pool_kernel,
        out_shape=jax.ShapeDtypeStruct((NB, T, POOL_WIDTH), BF16),
        grid=(NB,),
        in_specs=[
            pl.BlockSpec((1, T, POOL_WIDTH), lambda b: (b, 0, 0)),
            pl.BlockSpec((POOL_WIDTH, POOL_WIDTH), lambda b: (0, 0)),
            pl.BlockSpec((1, POOL_WIDTH), lambda b: (0, 0)),
        ],
        out_specs=pl.BlockSpec((1, T, POOL_WIDTH), lambda b: (b, 0, 0)),
        scratch_shapes=[pltpu.VMEM((PAD_ROWS, POOL_WIDTH), F32)],
        compiler_params=_cparams(("arbitrary",)),
        name="pool",
    )(pool_u, wbd, scale)


HW = HB * GDN_DIM


NCH = 2 * HB
PW = NCH * CHUNK
GCH = 12


def _block_diag(xp, blk_masks):
    return jnp.concatenate([jnp.where(m, xp, 0.0) for m in blk_masks], axis=0).astype(BF16)


def _gdn_kernel(q_ref, k_ref, v_ref, z_ref, cwq_ref, cwk_ref, cwv_ref, ab_ref, alog_ref, dtb_ref, nw_ref,
                y_ref, pad_ref, qn_ref, kn_ref, vv_ref, m_ref, b_ref, qt_ref, o_ref, egl_ref, s_ref):
    def conv(x_ref, cw_ref, dst_ref, l2, scale):
        _fill_padded(pad_ref, lambda s, n: x_ref[0, s:s + n, :])
        cw = cw_ref[...]
        for ti in range(NT):
            prow = _tile_pad_row(ti)
            acc = jnp.zeros((TT, HW), F32)
            for j in range(GDN_CONV):
                acc = acc + pad_ref[pl.ds(prow - 2 + j, TT), :] * cw[j:j + 1, :]
            y = _silu(acc)
            if l2:
                for hh in range(HB):
                    yh = y[:, hh * GDN_DIM:(hh + 1) * GDN_DIM]
                    yh = yh * lax.rsqrt(jnp.sum(yh * yh, axis=-1, keepdims=True) + RMS_EPS)
                    dst_ref[ti * TT:(ti + 1) * TT, hh * GDN_DIM:(hh + 1) * GDN_DIM] = yh * scale
            else:
                dst_ref[ti * TT:(ti + 1) * TT, :] = y

    conv(q_ref, cwq_ref, qn_ref, True, GDN_DIM ** -0.5)
    conv(k_ref, cwk_ref, kn_ref, True, 1.0)
    conv(v_ref, cwv_ref, vv_ref, False, 1.0)

    ri = lax.broadcasted_iota(I32, (CHUNK, CHUNK), 0)
    ci = lax.broadcasted_iota(I32, (CHUNK, CHUNK), 1)
    tri2 = jnp.concatenate([(ci <= ri).astype(BF16), (ci >= ri).astype(BF16)], axis=0)
    rowp = lax.broadcasted_iota(I32, (CHUNK, PW), 0)
    lanep = lax.broadcasted_iota(I32, (CHUNK, PW), 1)
    blk = lanep // CHUNK
    colp = lanep - blk * CHUNK
    is_fwd = blk < HB
    ahead = jnp.where(is_fwd, rowp - colp, colp - rowp)
    incl_p = ahead >= 0
    strict_p = ahead > 0
    eye_p = (colp == rowp).astype(F32)
    eye_d = (lax.broadcasted_iota(I32, (GDN_DIM, GDN_DIM), 0)
             == lax.broadcasted_iota(I32, (GDN_DIM, GDN_DIM), 1)).astype(BF16)
    blk_masks = [blk == i for i in range(NCH)]
    alog = alog_ref[0]
    dtb = dtb_ref[0]

    def split3(v):
        h1 = v.astype(BF16)
        rem = v - h1.astype(F32)
        h2 = rem.astype(BF16)
        return h1, h2, (rem - h2.astype(F32)).astype(BF16)

    def phase1(it, carry):
        cs = [it * GCH + j for j in range(GCH)]
        r0 = [pl.multiple_of(c * CHUNK, CHUNK) for c in cs]
        m0 = [pl.multiple_of(c * GDN_DIM, GDN_DIM) for c in cs]
        e0 = [pl.multiple_of(c * 8, 8) for c in cs]
        G = range(GCH)
        abt = [ab_ref[0, pl.ds(r0[j], CHUNK), :] for j in G]
        g_all = [-jnp.exp(alog) * _softplus(abt[j] + dtb) for j in G]
        beta_all = [jax.nn.sigmoid(abt[j]) for j in G]
        g3 = [split3(g_all[j]) for j in G]
        gcs = [_dot(tri2, g3[j][0]) + (_dot(tri2, g3[j][1]) + _dot(tri2, g3[j][2])) for j in G]
        gb3 = [split3(jnp.where(strict_p, jnp.concatenate(
            [jnp.broadcast_to(g_all[j][:, i:i + 1], (CHUNK, CHUNK)) for i in range(NCH)], axis=1), 0.0)) for j in G]
        dif2 = [_dot(tri2, gb3[j][0]) + (_dot(tri2, gb3[j][1]) + _dot(tri2, gb3[j][2])) for j in G]
        kh = [[kn_ref[pl.ds(r0[j], CHUNK), hh * GDN_DIM:(hh + 1) * GDN_DIM] for hh in range(HB)] for j in G]
        qh = [[qn_ref[pl.ds(r0[j], CHUNK), hh * GDN_DIM:(hh + 1) * GDN_DIM] for hh in range(HB)] for j in G]
        vh = [[vv_ref[pl.ds(r0[j], CHUNK), hh * GDN_DIM:(hh + 1) * GDN_DIM] for hh in range(HB)] for j in G]
        gram = [[_dot_nt(jnp.concatenate([kh[j][hh].astype(BF16), qh[j][hh].astype(BF16), eye_d], axis=0),
                         kh[j][hh].astype(BF16)) for hh in range(HB)] for j in G]
        gcol1, bcol1, glast, egc1, kdsc = [], [], [], [], []
        for j in G:
            gcol1.append([]); bcol1.append([]); glast.append([]); egc1.append([]); kdsc.append([])
            for i in range(NCH):
                d = i // HB
                gc = gcs[j][d * CHUNK:(d + 1) * CHUNK]
                gcol1[j].append(gc[:, i:i + 1])
                bcol1[j].append(beta_all[j][:, NCH + i:NCH + i + 1])
                glast[j].append(gc[CHUNK - 1:CHUNK, i:i + 1] if d == 0 else gc[0:1, i:i + 1])
                egc1[j].append(jnp.exp(gcol1[j][i]))
                kdsc[j].append(jnp.exp(glast[j][i] - gcol1[j][i]))
                egl_ref[d, i % HB, pl.ds(e0[j], 8), :] = jnp.broadcast_to(jnp.exp(glast[j][i]), (8, LANES))
        lm, a_p = [], []
        for j in G:
            bcol_p = jnp.concatenate([jnp.broadcast_to(b, (CHUNK, CHUNK)) for b in bcol1[j]], axis=1)
            diff = jnp.where(is_fwd, dif2[j][:CHUNK], dif2[j][CHUNK:])
            dec = jnp.where(incl_p, jnp.exp(jnp.where(incl_p, diff, 0.0)), 0.0)
            kk_p = jnp.concatenate([gram[j][i % HB][:CHUNK] for i in range(NCH)], axis=1)
            qk_p = jnp.concatenate([gram[j][i % HB][CHUNK:2 * CHUNK] for i in range(NCH)], axis=1)
            lm.append(jnp.where(strict_p, bcol_p * kk_p * dec, 0.0))
            a_p.append(qk_p * dec)
        x = [eye_p - lm[j] for j in G]
        p = [_dot(lm[j].astype(BF16), _block_diag(lm[j], blk_masks)) for j in G]
        for _ in range(4):
            r = [_dot(jnp.concatenate([x[j], p[j]], axis=0).astype(BF16), _block_diag(p[j], blk_masks)) for j in G]
            x = [x[j] + r[j][:CHUNK] for j in G]
            p = [r[j][CHUNK:] for j in G]
        r = [_dot(x[j].astype(BF16), _block_diag(p[j], blk_masks)) for j in G]
        x = [x[j] + r[j] for j in G]
        rhs = [jnp.concatenate(
            [jnp.concatenate([vh[j][i % HB] * bcol1[j][i], kh[j][i % HB] * (bcol1[j][i] * egc1[j][i])], axis=1)
             for i in range(NCH)], axis=0).astype(BF16) for j in G]
        uw = [_dot(_block_diag(x[j], blk_masks), rhs[j]) for j in G]
        ao = [_dot(_block_diag(a_p[j], blk_masks), uw[j].astype(BF16)) for j in G]
        mb = [[_dot(gram[j][i % HB][2 * CHUNK:].astype(BF16),
                    (uw[j][i * CHUNK:(i + 1) * CHUNK] * kdsc[j][i]).astype(BF16)) for i in range(NCH)] for j in G]
        for j in G:
            for hh in range(HB):
                hs = slice(hh * GDN_DIM, (hh + 1) * GDN_DIM)
                o0 = jnp.zeros((CHUNK, GDN_DIM), F32)
                for d in range(2):
                    i = d * HB + hh
                    rs = slice(i * CHUNK, (i + 1) * CHUNK)
                    b_ref[d, hh, pl.ds(m0[j], GDN_DIM), :] = mb[j][i][:, :GDN_DIM].astype(BF16)
                    m_ref[d, hh, pl.ds(m0[j], GDN_DIM), :] = mb[j][i][:, GDN_DIM:].astype(BF16)
                    qt_ref[d, pl.ds(r0[j], CHUNK), hs] = (qh[j][hh] * egc1[j][i] - ao[j][rs, GDN_DIM:]).astype(BF16)
                    o0 = o0 + ao[j][rs, :GDN_DIM]
                o_ref[pl.ds(r0[j], CHUNK), hs] = o0
        return carry

    lax.fori_loop(0, NC // GCH, phase1, 0)

    s_ref[...] = jnp.zeros(s_ref.shape, F32)

    def phase2(step, carry):
        cb = jnp.where(step < NCC, NCC - 1 - step, NC + NCC - 1 - step)
        chains = [(d, hh, c) for d, c in ((0, step), (1, cb)) for hh in range(HB)]
        rows = [pl.ds(pl.multiple_of(c * CHUNK, CHUNK), CHUNK) for _, _, c in chains]
        mrows = [pl.ds(pl.multiple_of(c * GDN_DIM, GDN_DIM), GDN_DIM) for _, _, c in chains]
        hs = [slice(hh * GDN_DIM, (hh + 1) * GDN_DIM) for _, hh, _ in chains]
        n = range(len(chains))
        st = [s_ref[chains[i][0], chains[i][1]] for i in n]
        stb = [st[i].astype(BF16) for i in n]
        ms = [_dot(m_ref[chains[i][0], chains[i][1], mrows[i], :], stb[i]) for i in n]
        oq = [_dot(qt_ref[chains[i][0], rows[i], hs[i]], stb[i]) for i in n]
        for i in n:
            d, hh, c = chains[i]
            egl = egl_ref[d, hh, pl.ds(pl.multiple_of(c * 8, 8), 1), :]
            s_ref[d, hh] = st[i] * egl - ms[i] + b_ref[d, hh, mrows[i], :].astype(F32)
            o_ref[rows[i], hs[i]] += oq[i]
        return carry

    lax.fori_loop(0, NC, phase2, 0, unroll=4)

    nw = nw_ref[...]
    for ti in range(NT):
        rs = slice(ti * TT, (ti + 1) * TT)
        o = o_ref[rs, :]
        zz = _silu(z_ref[0, rs, :])
        for hh in range(HB):
            hs = slice(hh * GDN_DIM, (hh + 1) * GDN_DIM)
            oh = o[:, hs]
            oh = oh * lax.rsqrt(jnp.mean(oh * oh, axis=-1, keepdims=True) + RMS_EPS) * nw
            y_ref[0, rs, hs] = (oh * zz[:, hs]).astype(BF16)


def _gdn(p_main, conv_w, ab, alog_g, dtb_g, norm_w):
    nhb = GDN_WIDTH // HW
    blk = lambda off: pl.BlockSpec((1, T, HW), lambda b, g: (b, 0, off * nhb + g))
    cblk = lambda off: pl.BlockSpec((GDN_CONV, HW), lambda b, g: (0, off * nhb + g))
    return pl.pallas_call(
        _gdn_kernel,
        out_shape=jax.ShapeDtypeStruct((NB, T, GDN_WIDTH), BF16),
        grid=(NB, NG),
        in_specs=[
            blk(0), blk(1), blk(2), blk(3),
            cblk(0), cblk(1), cblk(2),
            pl.BlockSpec((1, T, LANES), lambda b, g: (b, 0, g)),
            pl.BlockSpec((1, 1, LANES), lambda b, g: (g, 0, 0)),
            pl.BlockSpec((1, 1, LANES), lambda b, g: (g, 0, 0)),
            pl.BlockSpec((1, GDN_DIM), lambda b, g: (0, 0)),
        ],
        out_specs=pl.BlockSpec((1, T, HW), lambda b, g: (b, 0, g)),
        scratch_shapes=[
            pltpu.VMEM((PAD_ROWS, HW), F32),
            pltpu.VMEM((T, HW), F32),
            pltpu.VMEM((T, HW), F32),
            pltpu.VMEM((T, HW), F32),
            pltpu.VMEM((2, HB, NC * GDN_DIM, GDN_DIM), BF16),
            pltpu.VMEM((2, HB, NC * GDN_DIM, GDN_DIM), BF16),
            pltpu.VMEM((2, T, HW), BF16),
            pltpu.VMEM((T, HW), F32),
            pltpu.VMEM((2, HB, NC * 8, LANES), F32),
            pltpu.VMEM((2, HB, GDN_DIM, GDN_DIM), F32),
        ],
        compiler_params=_cparams(("arbitrary", "arbitrary")),
        name="gdn",
    )(p_main, p_main, p_main, p_main, conv_w, conv_w, conv_w, ab, alog_g, dtb_g, norm_w)


NEG = -1e30


def _mix_kernel(n_act, ctx_tile0, *refs):
    act_refs = refs[:n_act]
    w_refs = refs[n_act:2 * n_act]
    rest = refs[2 * n_act:]
    b = pl.program_id(0)
    t = pl.program_id(1)
    if ctx_tile0:
        c_ref, x_ref, mod_ref, lng_ref, lnb_ref, wr_ref, br_ref, x1_ref, h2_ref, route_ref = rest
        row = jnp.where(t == 0, NB, b)
        xt = jnp.where(t == 0, c_ref[0], x_ref[0])
    else:
        x_ref, mod_ref, lng_ref, lnb_ref, wr_ref, br_ref, x1_ref, h2_ref, route_ref = rest
        row = b
        xt = x_ref[0]
    y = _dot(act_refs[0][0], w_refs[0][...])
    for a_ref, w_ref in zip(act_refs[1:], w_refs[1:]):
        y = y + _dot(a_ref[0], w_ref[...])
    x1 = _layer_norm(ALPHA * xt + _mod_slice(mod_ref, row, 2) * y, lng_ref[...], lnb_ref[...])
    x1_ref[0] = x1
    h2 = x1 * (1.0 + _mod_slice(mod_ref, row, 4)) + _mod_slice(mod_ref, row, 3)
    h2_ref[...] = _to_token_tiles(h2)
    logits = _dot3(h2, wr_ref[...]) + br_ref[...]
    lane = lax.broadcasted_iota(I32, (TT, LANES), 1)
    gl = jnp.where(lane < MOE_GROUPS, logits, NEG)
    gm = jnp.max(gl, axis=-1, keepdims=True)
    gsum = jnp.sum(jnp.exp(gl - gm), axis=-1, keepdims=True)
    g_val = 1.0 / gsum
    g_idx = jnp.min(jnp.where(gl == gm, lane, LANES), axis=-1, keepdims=True)
    sel = (lane >= MOE_GROUPS) & (lane < MOE_GROUPS + MOE_EXPERTS) & (((lane - MOE_GROUPS) // MOE_PER_GROUP) == g_idx)
    el = jnp.where(sel, logits, NEG)
    m1 = jnp.max(el, axis=-1, keepdims=True)
    i1 = jnp.min(jnp.where(el == m1, lane, LANES), axis=-1, keepdims=True)
    el2 = jnp.where(lane == i1, NEG, el)
    m2 = jnp.max(el2, axis=-1, keepdims=True)
    i2 = jnp.min(jnp.where(el2 == m2, lane, LANES), axis=-1, keepdims=True)
    ex2 = jnp.exp(m2 - m1)
    w1 = g_val / (1.0 + ex2)
    w2 = w1 * ex2
    e1 = (i1 - MOE_GROUPS).astype(F32)
    e2 = (i2 - MOE_GROUPS).astype(F32)
    route_ref[...] = jnp.where(lane == 0, e1, jnp.where(lane == 1, e2, jnp.where(lane == 2, w1, jnp.where(lane == 3, w2, 0.0))))


def _mix(acts, ws, streams, mod, ln_g, ln_b, wr, br, ctx_tile0):
    ntl = NT if ctx_tile0 else NTL
    if ctx_tile0:
        stream_specs = [pl.BlockSpec((1, CTX, D), lambda b, t: (b, 0, 0)),
                        pl.BlockSpec((1, TT, D), lambda b, t: (b, jnp.maximum(t - 1, 0), 0))]
    else:
        stream_specs = [pl.BlockSpec((1, TT, D), lambda b, t: (b, t + 1, 0))]
    n_tok = NB * ntl * TT
    act_specs = [pl.BlockSpec((1, TT, a.shape[2]), lambda b, t: (b, t, 0)) for a in acts]
    w_specs = [pl.BlockSpec(w.shape, lambda b, t: (0, 0)) for w in ws]
    return pl.pallas_call(
        functools.partial(_mix_kernel, len(acts), ctx_tile0),
        out_shape=(
            jax.ShapeDtypeStruct((NB, ntl * TT, D), F32),
            jax.ShapeDtypeStruct((n_tok, NBLK, LANES), F32),
            jax.ShapeDtypeStruct((n_tok, LANES), F32),
        ),
        grid=(NB, ntl),
        in_specs=act_specs + w_specs + stream_specs + [
            pl.BlockSpec((16, N_MOD * D), lambda b, t: (0, 0)),
            pl.BlockSpec((1, D), lambda b, t: (0, 0)),
            pl.BlockSpec((1, D), lambda b, t: (0, 0)),
            pl.BlockSpec((D, LANES), lambda b, t: (0, 0)),
            pl.BlockSpec((1, LANES), lambda b, t: (0, 0)),
        ],
        out_specs=(
            pl.BlockSpec((1, TT, D), lambda b, t: (b, t, 0)),
            pl.BlockSpec((TT, NBLK, LANES), lambda b, t: (b * ntl + t, 0, 0)),
            pl.BlockSpec((TT, LANES), lambda b, t: (b * ntl + t, 0)),
        ),
        compiler_params=_cparams(("arbitrary", "arbitrary")),
        name="mix_ctx" if ctx_tile0 else "mix_lat",
    )(*acts, *ws, *streams, mod, ln_g, ln_b, wr, br)


def _plan_kernel(route_ref, er_ref, cnt_ref, carry_ref):
    i = pl.program_id(0)

    @pl.when(i == 0)
    def _():
        carry_ref[...] = jnp.zeros(carry_ref.shape, F32)

    r = route_ref[...]
    lane = lax.broadcasted_iota(I32, (PB, LANES), 1)
    lanef = lane.astype(F32)
    ri = lax.broadcasted_iota(I32, (PB, PB), 0)
    ci = lax.broadcasted_iota(I32, (PB, PB), 1)
    before = (ci < ri).astype(BF16)
    oh1 = jnp.where(lanef == r[:, 0:1], 1.0, 0.0)
    oh2 = jnp.where(lanef == r[:, 1:2], 1.0, 0.0)
    c0 = carry_ref[0:1, :]
    tot1 = jnp.sum(oh1, axis=0, keepdims=True)
    tot2 = jnp.sum(oh2, axis=0, keepdims=True)
    r1 = _dot(before, oh1.astype(BF16)) + c0
    r2 = _dot(before, oh2.astype(BF16)) + (c0 + tot1)
    rank1 = jnp.sum(r1 * oh1, axis=-1, keepdims=True)
    rank2 = jnp.sum(r2 * oh2, axis=-1, keepdims=True)
    er = jnp.where(lane < 2, r, jnp.where(lane == 2, rank1, jnp.where(lane == 3, rank2, 0.0)))
    er_ref[...] = er.T[0:8, :]
    cnew = jnp.broadcast_to(c0 + tot1 + tot2, carry_ref.shape)
    carry_ref[...] = cnew
    cnt_ref[...] = cnew


def _plan(route):
    n_tok = route.shape[0]
    return pl.pallas_call(
        _plan_kernel,
        out_shape=(jax.ShapeDtypeStruct((8, n_tok), F32), jax.ShapeDtypeStruct((8, LANES), F32)),
        grid=(n_tok // PB,),
        in_specs=[pl.BlockSpec((PB, LANES), lambda i: (i, 0))],
        out_specs=(pl.BlockSpec((8, PB), lambda i: (0, i)), pl.BlockSpec((8, LANES), lambda i: (0, 0))),
        scratch_shapes=[pltpu.VMEM((8, LANES), F32)],
        compiler_params=_cparams(("arbitrary",)),
        name="moe_plan",
    )(route)


INV_UNROLL = 16


def _invmap_kernel(pos_ref, src_ref):
    n_rows = src_ref.shape[0]
    n_tok = pos_ref.shape[0] // 2

    def clear(g, c):
        for u in range(INV_UNROLL):
            src_ref[g * INV_UNROLL + u] = 0
        return c

    lax.fori_loop(0, n_rows // INV_UNROLL, clear, 0)

    def put(g, c):
        for u in range(INV_UNROLL):
            t = g * INV_UNROLL + u
            src_ref[pos_ref[t]] = t
            src_ref[pos_ref[n_tok + t]] = t
        return c

    lax.fori_loop(0, n_tok // INV_UNROLL, put, 0)


def _invmap(pos_flat, n_rows):
    return pl.pallas_call(
        _invmap_kernel,
        out_shape=jax.ShapeDtypeStruct((n_rows,), I32),
        in_specs=[pl.BlockSpec(memory_space=pltpu.SMEM)],
        out_specs=pl.BlockSpec(memory_space=pltpu.SMEM),
        name="moe_invmap",
    )(pos_flat)


def _ffn_kernel(te_ref, nu_ref, src_ref, h_hbm, wg_ref, wu_ref, wd_ref, y_ref, buf, wgb, wub, wdb, sem):
    i = pl.program_id(0)
    nu = nu_ref[0]

    def issue(tile, slot, lo, hi):
        base = tile * TM
        for r in range(lo, hi):
            pltpu.make_async_copy(h_hbm.at[src_ref[base + r]], buf.at[slot, r], sem.at[slot]).start(priority=r % 2)

    def ffn_tile(slot, next_tile, next_slot):
        per = TM // FFN_DMA_GROUPS
        sent = [0]

        def gather_some():
            if next_tile is not None and sent[0] < FFN_DMA_GROUPS:
                issue(next_tile, next_slot, sent[0] * per, (sent[0] + 1) * per)
                sent[0] += 1

        x = _from_token_tiles(buf[slot]).astype(BF16)
        hcol = MOE_FF // 2
        gather_some()
        parts = []
        for w in (wgb, wub):
            for c in range(2):
                parts.append(_dot(x, w[:, c * hcol:(c + 1) * hcol]))
                gather_some()
        a = jnp.concatenate(parts[0:2], axis=1)
        bb = jnp.concatenate(parts[2:4], axis=1)
        act = (_silu(a) * bb).astype(BF16)
        ys = []
        for c in range(D // hcol):
            ys.append(_dot(act, wdb[:, c * hcol:(c + 1) * hcol]))
            gather_some()
        assert next_tile is None or sent[0] == FFN_DMA_GROUPS
        y_ref[...] = _to_token_tiles(jnp.concatenate(ys, axis=1))

    @pl.when(i == 0)
    def _():
        issue(0, 0, 0, TM)

        for j in range(1, FFN_BUFS - 1):
            @pl.when(nu > j)
            def _():
                issue(j, j, 0, TM)

    @pl.when(i < nu)
    def _():
        slot = i % FFN_BUFS
        pltpu.make_async_copy(h_hbm.at[pl.ds(0, TM)], buf.at[slot], sem.at[slot]).wait()
        e = te_ref[i]
        ep = te_ref[jnp.maximum(i - 1, 0)]

        @pl.when((i == 0) | (e != ep))
        def _():
            wgb[...] = wg_ref[0, 0].astype(BF16)
            wub[...] = wu_ref[0, 0].astype(BF16)
            wdb[...] = wd_ref[0, 0].astype(BF16)

        ahead = i + (FFN_BUFS - 1)

        @pl.when(ahead < nu)
        def _():
            ffn_tile(slot, ahead, ahead % FFN_BUFS)

        @pl.when(ahead >= nu)
        def _():
            ffn_tile(slot, None, None)

    @pl.when(i >= nu)
    def _():
        y_ref[...] = jnp.zeros(y_ref.shape, F32)


def _ffn(tile_expert, n_used, src, h2, wg, wu, wd, layer):
    n_rows = src.shape[0]
    n_tiles = n_rows // TM
    wspec = lambda shape: pl.BlockSpec((1, 1) + shape, lambda i, te, nu, src: (layer, te[i], 0, 0))
    return pl.pallas_call(
        _ffn_kernel,
        out_shape=jax.ShapeDtypeStruct((n_rows, NBLK, LANES), F32),
        grid_spec=pltpu.PrefetchScalarGridSpec(
            num_scalar_prefetch=3,
            grid=(n_tiles,),
            in_specs=[
                pl.BlockSpec(memory_space=pl.ANY),
                wspec((D, MOE_FF)), wspec((D, MOE_FF)), wspec((MOE_FF, D)),
            ],
            out_specs=pl.BlockSpec((TM, NBLK, LANES), lambda i, te, nu, src: (i, 0, 0)),
            scratch_shapes=[
                pltpu.VMEM((FFN_BUFS, TM, NBLK, LANES), F32),
                pltpu.VMEM((D, MOE_FF), BF16),
                pltpu.VMEM((D, MOE_FF), BF16),
                pltpu.VMEM((MOE_FF, D), BF16),
                pltpu.SemaphoreType.DMA((FFN_BUFS,)),
            ],
        ),
        compiler_params=_cparams(("arbitrary",)),
        name="moe_ffn",
    )(tile_expert, n_used, src, h2, wg, wu, wd)


def _combine_kernel(ntl, ctx_tile0, pos_ref, y_hbm, x1_ref, route_ref, mod_ref, lng_ref, lnb_ref, o_ref, buf, sem):
    b = pl.program_id(0)
    t = pl.program_id(1)
    flat = b * ntl + t
    n_tiles = NB * ntl

    def issue(tile, slot):
        base = tile * TT
        for r in range(TT):
            for k in range(2):
                pltpu.make_async_copy(y_hbm.at[pos_ref[k * (n_tiles * TT) + base + r]], buf.at[slot, k * TT + r],
                                      sem.at[slot]).start(priority=k)

    @pl.when(flat == 0)
    def _():
        issue(0, 0)

    slot = flat % 2
    pltpu.make_async_copy(y_hbm.at[pl.ds(0, 2 * TT)], buf.at[slot], sem.at[slot]).wait()

    @pl.when(flat + 1 < n_tiles)
    def _():
        issue(flat + 1, 1 - slot)

    row = jnp.where(t == 0, NB, b) if ctx_tile0 else b
    r = route_ref[...]
    f = (r[:, 2:3] * _from_token_tiles(buf[slot, 0:TT]) + r[:, 3:4] * _from_token_tiles(buf[slot, TT:2 * TT]))
    o_ref[0] = _layer_norm(ALPHA * x1_ref[0] + _mod_slice(mod_ref, row, 5) * f, lng_ref[...], lnb_ref[...])


def _combine(pos_flat, ys, x1, route, mod, ln_g, ln_b, ctx_tile0):
    ntl = NT if ctx_tile0 else NTL
    return pl.pallas_call(
        functools.partial(_combine_kernel, ntl, ctx_tile0),
        out_shape=jax.ShapeDtypeStruct((NB, ntl * TT, D), F32),
        grid_spec=pltpu.PrefetchScalarGridSpec(
            num_scalar_prefetch=1,
            grid=(NB, ntl),
            in_specs=[
                pl.BlockSpec(memory_space=pl.ANY),
                pl.BlockSpec((1, TT, D), lambda b, t, pos: (b, t, 0)),
                pl.BlockSpec((TT, LANES), lambda b, t, pos: (b * ntl + t, 0)),
                pl.BlockSpec((16, N_MOD * D), lambda b, t, pos: (0, 0)),
                pl.BlockSpec((1, D), lambda b, t, pos: (0, 0)),
                pl.BlockSpec((1, D), lambda b, t, pos: (0, 0)),
            ],
            out_specs=pl.BlockSpec((1, TT, D), lambda b, t, pos: (b, t, 0)),
            scratch_shapes=[pltpu.VMEM((2, 2 * TT, NBLK, LANES), F32), pltpu.SemaphoreType.DMA((2,))],
        ),
        compiler_params=_cparams(("arbitrary", "arbitrary")),
        name="moe_combine_ctx" if ctx_tile0 else "moe_combine_lat",
    )(pos_flat, ys, x1, route, mod, ln_g, ln_b)


def _moe(h2, route, x1, mod, ln_g, ln_b, wg, wu, wd, layer, ctx_tile0):
    n_tok = h2.shape[0]
    n_tiles = (2 * n_tok) // TM + MOE_EXPERTS
    er, cnt = _plan(route)
    counts = cnt[0, :MOE_EXPERTS].astype(I32)
    tiles_e = (counts + TM - 1) // TM
    tile_end = jnp.cumsum(tiles_e)
    tile_start = tile_end - tiles_e
    n_used = tile_end[-1]
    tj = jnp.arange(n_tiles, dtype=I32)
    te = jnp.minimum(jnp.sum((tile_end[None, :] <= tj[:, None]).astype(I32), axis=1), MOE_EXPERTS - 1)
    te_last = jnp.max(jnp.where(tj < n_used, te, 0))
    tile_expert = jnp.where(tj < n_used, te, te_last)
    eid = er[0:2].astype(I32)
    first_row = jnp.sum(jnp.where(eid[None] == jnp.arange(MOE_EXPERTS, dtype=I32)[:, None, None],
                                  (tile_start * TM)[:, None, None], 0), axis=0)
    pos_flat = (first_row + er[2:4].astype(I32)).reshape(-1)
    src = _invmap(pos_flat, n_tiles * TM)
    ys = _ffn(tile_expert, n_used.reshape(1), src, h2, wg, wu, wd, layer)
    return _combine(pos_flat, ys, x1, route, mod, ln_g, ln_b, ctx_tile0)


W_IN1 = MLA_KV_LORA + MLA_ROPE + MLA_Q_LORA + MLA_ROPE
Q_OFF = MLA_KV_LORA + MLA_ROPE
KRS_OFF = Q_OFF + MLA_Q_LORA
SM_SCALE = MLA_QK ** -0.5


def _rms(v, w):
    return v * lax.rsqrt(jnp.mean(v * v, axis=-1, keepdims=True) + RMS_EPS) * w


def _mla_proj_kernel(x_ref, mod_ref, win_ref, kvn_ref, qnw_ref, wkn_ref, wkv_ref, wqn_ref, wqr_ref, wqrs_ref,
                     cosk_ref, sink_ref, cosq_ref, sinq_ref, q_out, k_out, v_out):
    b = pl.program_id(0)
    t = pl.program_id(1)
    row = jnp.where(t == 0, NB, b)
    h = x_ref[0] * (1.0 + _mod_slice(mod_ref, row, 1)) + _mod_slice(mod_ref, row, 0)
    p = _dot(h.astype(BF16), win_ref[...])
    ckv = _rms(p[:, :MLA_KV_LORA], kvn_ref[...]).astype(BF16)
    kn = _dot(ckv, wkn_ref[...])
    vv = _dot(ckv, wkv_ref[...])
    kr = p[:, MLA_KV_LORA:Q_OFF]
    krs = p[:, KRS_OFF:KRS_OFF + MLA_ROPE]
    kr = jnp.where(t > 0, kr * cosk_ref[...] + krs * sink_ref[...], kr).astype(BF16)
    ql = _rms(p[:, Q_OFF:KRS_OFF], qnw_ref[...]).astype(BF16)
    qn = _dot(ql, wqn_ref[...]) * SM_SCALE
    qr = (_dot(ql, wqr_ref[...]) * cosq_ref[...] + _dot(ql, wqrs_ref[...]) * sinq_ref[...]) * SM_SCALE
    for hd in range(MLA_HEADS):
        k_out[0, hd, :, 0:MLA_NOPE] = kn[:, hd * MLA_NOPE:(hd + 1) * MLA_NOPE].astype(BF16)
        k_out[0, hd, :, MLA_NOPE:MLA_QK] = kr
        v_out[0, hd] = vv[:, hd * MLA_V:(hd + 1) * MLA_V].T.astype(BF16)
        q_out[0, hd, :, 0:MLA_NOPE] = qn[:, hd * MLA_NOPE:(hd + 1) * MLA_NOPE].astype(BF16)
        q_out[0, hd, :, MLA_NOPE:MLA_QK] = qr[:, hd * MLA_ROPE:(hd + 1) * MLA_ROPE].astype(BF16)


def _mla_proj(xs, mod, win, kvn, qnw, wkn, wkv, wqn, wqr, wqrs, cosk, sink, cosq, sinq):
    full = lambda a: pl.BlockSpec(a.shape, lambda b, t: (0, 0))
    lat = lambda w: pl.BlockSpec((TT, w), lambda b, t: (jnp.maximum(t - 1, 0), 0))
    return pl.pallas_call(
        _mla_proj_kernel,
        out_shape=(
            jax.ShapeDtypeStruct((NB, MLA_HEADS, SEQ, MLA_QK), BF16),
            jax.ShapeDtypeStruct((NB, MLA_HEADS, T, MLA_QK), BF16),
            jax.ShapeDtypeStruct((NB, MLA_HEADS, MLA_V, T), BF16),
        ),
        grid=(NB, NT),
        in_specs=[
            pl.BlockSpec((1, TT, D), lambda b, t: (b, t, 0)),
            pl.BlockSpec((16, N_MOD * D), lambda b, t: (0, 0)),
            full(win), full(kvn), full(qnw), full(wkn), full(wkv), full(wqn), full(wqr), full(wqrs),
            lat(MLA_ROPE), lat(MLA_ROPE), lat(MLA_HEADS * MLA_ROPE), lat(MLA_HEADS * MLA_ROPE),
        ],
        out_specs=(
            pl.BlockSpec((1, MLA_HEADS, TT, MLA_QK), lambda b, t: (b, 0, jnp.maximum(t - 1, 0), 0)),
            pl.BlockSpec((1, MLA_HEADS, TT, MLA_QK), lambda b, t: (b, 0, t, 0)),
            pl.BlockSpec((1, MLA_HEADS, MLA_V, TT), lambda b, t: (b, 0, 0, t)),
        ),
        compiler_params=_cparams(("arbitrary", "arbitrary")),
        name="mla_proj",
    )(xs, mod, win, kvn, qnw, wkn, wkv, wqn, wqr, wqrs, cosk, sink, cosq, sinq)


AH = 4


def _attn_kernel(q_ref, k_ref, vt_ref, o_ref, s_ref):
    nq = SEQ // TQ

    def scores(j):
        for a in range(AH):
            s_ref[a, j % 2] = _dot_nt(k_ref[0, a], q_ref[0, a, j * TQ:(j + 1) * TQ, :])

    scores(0)
    for j in range(nq):
        if j + 1 < nq:
            scores(j + 1)
        ps, ls = [], []
        for a in range(AH):
            s = s_ref[a, j % 2]
            p = jnp.exp(s - jnp.max(s, axis=0, keepdims=True))
            ls.append(jnp.sum(p, axis=0, keepdims=True))
            ps.append(p.astype(BF16))
        ots = [_dot(vt_ref[0, a], ps[a]) / ls[a] for a in range(AH)]
        for a in range(AH):
            o_ref[0, j * TQ:(j + 1) * TQ, a * MLA_V:(a + 1) * MLA_V] = ots[a].T.astype(BF16)


def _attn(q, k, vt):
    return pl.pallas_call(
        _attn_kernel,
        out_shape=jax.ShapeDtypeStruct((NB, SEQ, MLA_HEADS * MLA_V), BF16),
        grid=(NB, MLA_HEADS // AH),
        in_specs=[
            pl.BlockSpec((1, AH, SEQ, MLA_QK), lambda b, h: (b, h, 0, 0)),
            pl.BlockSpec((1, AH, T, MLA_QK), lambda b, h: (b, h, 0, 0)),
            pl.BlockSpec((1, AH, MLA_V, T), lambda b, h: (b, h, 0, 0)),
        ],
        out_specs=pl.BlockSpec((1, SEQ, AH * MLA_V), lambda b, h: (b, 0, h)),
        scratch_shapes=[pltpu.VMEM((AH, 2, T, TQ), F32)],
        compiler_params=_cparams(("arbitrary", "arbitrary")),
        name="mla_attn",
    )(q, k, vt)


def _rope_tables():
    n = SEQ
    rowp = (jnp.arange(n) // GRID_W).astype(F32)
    colp = (jnp.arange(n) % GRID_W).astype(F32)
    inv_freq = ROPE_BASE ** (-jnp.arange(ROPE_F, dtype=F32) / ROPE_F)
    ar = rowp[:, None] * inv_freq
    ac = colp[:, None] * inv_freq
    cos = jnp.concatenate([jnp.cos(ar), jnp.cos(ar), jnp.cos(ac), jnp.cos(ac)], axis=1)
    sin = jnp.concatenate([-jnp.sin(ar), jnp.sin(ar), -jnp.sin(ac), jnp.sin(ac)], axis=1)
    return cos, sin


def _rope_swap_perm():
    f = ROPE_F
    return jnp.concatenate([jnp.arange(f, 2 * f), jnp.arange(0, f), jnp.arange(3 * f, 4 * f), jnp.arange(2 * f, 3 * f)])


def _router_params(w_group, b_group, w_expert, b_expert):
    pad = LANES - MOE_GROUPS - MOE_EXPERTS
    wr = jnp.concatenate([w_group, w_expert, jnp.zeros((D, pad), F32)], axis=1)
    br = jnp.concatenate([b_group, b_expert, jnp.zeros((pad,), F32)]).reshape(1, LANES)
    return wr, br


def kernel(x, c, ctx, c_ctx, ada_w, ada_b, ln_g, ln_b, ab_w_in, pool_w, pool_scale, gdn_conv_w, gdn_a_log, gdn_dt_bias, gdn_norm_w, ab_w_out, mla_w_in, mla_kv_norm, mla_w_ukv, mla_q_norm, mla_w_uq, mla_w_out, moe_w_group, moe_b_group, moe_w_expert, moe_b_expert, moe_w_gate, moe_w_up, moe_w_down):
    assert x.shape == (NB, SEQ, D) and ctx.shape == (NB, CTX, D)
    cv = jnp.concatenate([c, c_ctx[None, :], jnp.zeros((16 - NB - 1, D), F32)], axis=0)
    mod = _ada(cv, ada_w, ada_b)

    w_in = ab_w_in[0]
    w_main = w_in[:, :W_MAIN].astype(BF16)
    o_ab = W_MAIN
    ab_cols = []
    for g in range(NG):
        idx = [o_ab + kind * 2 * GDN_HEADS + d * GDN_HEADS + g * HB + hh
               for kind in range(2) for d in range(2) for hh in range(HB)]
        ab_cols.append(jnp.concatenate([w_in[:, jnp.array(idx)], jnp.zeros((D, LANES - len(idx)), F32)], axis=1))
    w_ab = jnp.concatenate(ab_cols, axis=1)

    def gate_rows(p):
        rows = []
        for g in range(NG):
            vals = jnp.stack([p[d, g * HB + hh] for d in range(2) for hh in range(HB)])
            rows.append(jnp.concatenate([vals, jnp.zeros((LANES - 2 * HB,), F32)]))
        return jnp.stack(rows).reshape(NG, 1, LANES)

    alog_g = gate_rows(gdn_a_log[0])
    dtb_g = gate_rows(gdn_dt_bias[0])
    wbd = jax.scipy.linalg.block_diag(*[pool_w[0, g] for g in range(len(POOL_WINDOWS))]).astype(BF16)

    pool_u, p_main, ab = _inproj0(ctx, x, mod[0], w_main, w_ab)
    pool_y = _pool(pool_u, wbd, pool_scale[0].reshape(1, POOL_WIDTH))
    gdn_y = _gdn(p_main, gdn_conv_w[0], ab, alog_g, dtb_g, gdn_norm_w[0].reshape(1, GDN_DIM))

    w_out0 = ab_w_out[0].astype(BF16)
    wr0, br0 = _router_params(moe_w_group[0], moe_b_group[0], moe_w_expert[0], moe_b_expert[0])
    x1, h2, route = _mix([pool_y, gdn_y], [w_out0[:POOL_WIDTH], w_out0[POOL_WIDTH:]], (ctx, x), mod[0],
                         ln_g[0, 0].reshape(1, D), ln_b[0, 0].reshape(1, D), wr0, br0, True)
    xs = _moe(h2, route, x1, mod[0], ln_g[0, 1].reshape(1, D), ln_b[0, 1].reshape(1, D),
              moe_w_gate, moe_w_up, moe_w_down, 0, True)

    perm = _rope_swap_perm()
    w_in1 = mla_w_in[0]
    win = jnp.concatenate([w_in1, w_in1[:, MLA_KV_LORA:Q_OFF][:, perm]], axis=1).astype(BF16)
    ukv = mla_w_ukv[0].reshape(MLA_KV_LORA, MLA_HEADS, MLA_NOPE + MLA_V)
    wkn = ukv[:, :, :MLA_NOPE].reshape(MLA_KV_LORA, MLA_HEADS * MLA_NOPE).astype(BF16)
    wkv = ukv[:, :, MLA_NOPE:].reshape(MLA_KV_LORA, MLA_HEADS * MLA_V).astype(BF16)
    uq = mla_w_uq[0].reshape(MLA_Q_LORA, MLA_HEADS, MLA_QK)
    wqn = uq[:, :, :MLA_NOPE].reshape(MLA_Q_LORA, MLA_HEADS * MLA_NOPE).astype(BF16)
    wqr = uq[:, :, MLA_NOPE:].reshape(MLA_Q_LORA, MLA_HEADS * MLA_ROPE).astype(BF16)
    wqrs = uq[:, :, MLA_NOPE:][:, :, perm].reshape(MLA_Q_LORA, MLA_HEADS * MLA_ROPE).astype(BF16)
    cosk, sink = _rope_tables()
    cosq = jnp.tile(cosk, (1, MLA_HEADS))
    sinq = jnp.tile(sink, (1, MLA_HEADS))

    q, k, v = _mla_proj(xs, mod[1], win, mla_kv_norm[0].reshape(1, MLA_KV_LORA), mla_q_norm[0].reshape(1, MLA_Q_LORA),
                        wkn, wkv, wqn, wqr, wqrs, cosk, sink, cosq, sinq)
    att = _attn(q, k, v)
    wr1, br1 = _router_params(moe_w_group[1], moe_b_group[1], moe_w_expert[1], moe_b_expert[1])
    x1, h2, route = _mix([att], [mla_w_out[0].astype(BF16)], (xs,), mod[1],
                         ln_g[1, 0].reshape(1, D), ln_b[1, 0].reshape(1, D), wr1, br1, False)
    return _moe(h2, route, x1, mod[1], ln_g[1, 1].reshape(1, D), ln_b[1, 1].reshape(1, D),
                moe_w_gate, moe_w_up, moe_w_down, 1, False)
```

```python
import functools
import math

import jax
import jax.numpy as jnp
from jax import lax
from jax.experimental import pallas as pl
from jax.experimental.pallas import tpu as pltpu

F32 = jnp.float32
BF16 = jnp.bfloat16
I32 = jnp.int32

D = 1024
NB = 8
SEQ = 2048
CTX = 256
T = SEQ + CTX
DEPTH = 2
N_MOD = 6
ALPHA = (2 * DEPTH) ** 0.25
LN_EPS = 1e-5
RMS_EPS = 1e-6

POOL_WINDOWS = (2, 4, 8, 16)
POOL_WIDTH = 256
POOL_GROUP_DIM = 64
GDN_HEADS = 6
GDN_DIM = 128
GDN_WIDTH = GDN_HEADS * GDN_DIM
GDN_CONV = 4
CHUNK = 64

MLA_HEADS = 8
MLA_NOPE = 128
MLA_ROPE = 64
MLA_V = 128
MLA_QK = MLA_NOPE + MLA_ROPE
MLA_Q_LORA = 384
MLA_KV_LORA = 256
GRID_W = 64
ROPE_BASE = 10000.0
ROPE_F = MLA_ROPE // 4

MOE_GROUPS = 4
MOE_PER_GROUP = 8
MOE_EXPERTS = 32
MOE_FF = 512

TT = CTX
NT = T // TT
NTL = SEQ // TT
TM = 128
FFN_DMA_GROUPS = 8
FFN_BUFS = 6
PB = 512
HB = 2
NG = GDN_HEADS // HB
NC = T // CHUNK
NCC = CTX // CHUNK
TQ = 256
LANES = 128
VMEM_LIMIT = 56 * 1024 * 1024


def _dot(a, b):
    return jnp.dot(a, b, preferred_element_type=F32)


def _dot_nt(a, b):
    return lax.dot_general(a, b, (((1,), (1,)), ((), ())), preferred_element_type=F32)


def _dot_tn(a, b):
    return lax.dot_general(a, b, (((0,), (0,)), ((), ())), preferred_element_type=F32)


def _split2(x):
    hi = x.astype(BF16)
    lo = (x - hi.astype(F32)).astype(BF16)
    return hi, lo


def _dot3(a, b):
    ah, al = _split2(a)
    bh, bl = _split2(b)
    return _dot(ah, bh) + (_dot(ah, bl) + _dot(al, bh))


def _silu(x):
    return x * jax.nn.sigmoid(x)


def _softplus(x):
    return jnp.maximum(x, 0.0) + jnp.log1p(jnp.exp(-jnp.abs(x)))


def _layer_norm(v, g, b):
    mu = jnp.mean(v, axis=-1, keepdims=True)
    c = v - mu
    var = jnp.mean(c * c, axis=-1, keepdims=True)
    return c * lax.rsqrt(var + LN_EPS) * g + b


def _mod_slice(mod_ref, row, k):
    return mod_ref[pl.ds(row, 1), k * D:(k + 1) * D]


NBLK = D // LANES


def _to_token_tiles(y):
    return jnp.transpose(jnp.stack([y[:, s * LANES:(s + 1) * LANES] for s in range(NBLK)], axis=0), (1, 0, 2))


def _from_token_tiles(x3):
    xt = jnp.transpose(x3, (1, 0, 2))
    return jnp.concatenate([xt[s] for s in range(NBLK)], axis=1)


def _cparams(sem, vmem=VMEM_LIMIT):
    return pltpu.CompilerParams(dimension_semantics=sem, vmem_limit_bytes=vmem)


def _ada_kernel(cv_ref, w_ref, b_ref, o_ref):
    s = _silu(cv_ref[...])
    o_ref[0] = _dot3(s, w_ref[0]) + b_ref[0]


def _ada(cv, ada_w, ada_b):
    nblk = N_MOD
    return pl.pallas_call(
        _ada_kernel,
        out_shape=jax.ShapeDtypeStruct((DEPTH, 16, N_MOD * D), F32),
        grid=(DEPTH, nblk),
        in_specs=[
            pl.BlockSpec((16, D), lambda l, j: (0, 0)),
            pl.BlockSpec((1, D, D), lambda l, j: (l, 0, j)),
            pl.BlockSpec((1, 1, D), lambda l, j: (l, 0, j)),
        ],
        out_specs=pl.BlockSpec((1, 16, D), lambda l, j: (l, 0, j)),
        compiler_params=_cparams(("arbitrary", "arbitrary")),
        name="ada_mod",
    )(cv, ada_w, ada_b.reshape(DEPTH, 1, N_MOD * D))


W_MAIN = POOL_WIDTH + 4 * GDN_WIDTH
W_AB = NG * LANES


def _inproj0_kernel(c_ref, x_ref, mod_ref, w_ref, wab_ref, pool_ref, main_ref, ab_ref):
    b = pl.program_id(0)
    t = pl.program_id(1)
    row = jnp.where(t == 0, NB, b)
    xt = jnp.where(t == 0, c_ref[0], x_ref[0])
    h = xt * (1.0 + _mod_slice(mod_ref, row, 1)) + _mod_slice(mod_ref, row, 0)
    p = _dot(h.astype(BF16), w_ref[...])
    pool_ref[0] = p[:, :POOL_WIDTH]
    main_ref[0] = p[:, POOL_WIDTH:]
    ab_ref[0] = _dot3(h, wab_ref[...])


def _inproj0(ctx, x, mod, w_main, w_ab):
    return pl.pallas_call(
        _inproj0_kernel,
        out_shape=(
            jax.ShapeDtypeStruct((NB, T, POOL_WIDTH), F32),
            jax.ShapeDtypeStruct((NB, T, 4 * GDN_WIDTH), F32),
            jax.ShapeDtypeStruct((NB, T, W_AB), F32),
        ),
        grid=(NB, NT),
        in_specs=[
            pl.BlockSpec((1, CTX, D), lambda b, t: (b, 0, 0)),
            pl.BlockSpec((1, TT, D), lambda b, t: (b, jnp.maximum(t - 1, 0), 0)),
            pl.BlockSpec((16, N_MOD * D), lambda b, t: (0, 0)),
            pl.BlockSpec((D, W_MAIN), lambda b, t: (0, 0)),
            pl.BlockSpec((D, W_AB), lambda b, t: (0, 0)),
        ],
        out_specs=(
            pl.BlockSpec((1, TT, POOL_WIDTH), lambda b, t: (b, t, 0)),
            pl.BlockSpec((1, TT, 4 * GDN_WIDTH), lambda b, t: (b, t, 0)),
            pl.BlockSpec((1, TT, W_AB), lambda b, t: (b, t, 0)),
        ),
        compiler_params=_cparams(("arbitrary", "arbitrary")),
        name="inproj0",
    )(ctx, x, mod, w_main, w_ab)


PAD_GAP = 16
PAD_CTX = PAD_GAP
PAD_LAT = PAD_CTX + CTX + 2 * PAD_GAP
PAD_ROWS = PAD_LAT + SEQ + PAD_GAP


def _fill_padded(pad_ref, src):
    w = pad_ref.shape[1]
    pad_ref[0:PAD_CTX, :] = jnp.zeros((PAD_CTX, w), F32)
    pad_ref[PAD_CTX + CTX:PAD_LAT, :] = jnp.zeros((2 * PAD_GAP, w), F32)
    pad_ref[PAD_LAT + SEQ:PAD_ROWS, :] = jnp.zeros((PAD_GAP, w), F32)
    pad_ref[PAD_CTX:PAD_CTX + CTX, :] = src(0, CTX)
    pad_ref[PAD_LAT:PAD_LAT + SEQ, :] = src(CTX, SEQ)


def _tile_pad_row(ti):
    return PAD_CTX if ti == 0 else PAD_LAT + (ti - 1) * TT


def _pool_kernel(u_ref, wbd_ref, scale_ref, o_ref, pad_ref):
    _fill_padded(pad_ref, lambda s, n: u_ref[0, s:s + n, :])
    lane = lax.broadcasted_iota(I32, (1, POOL_WIDTH), 1)
    grp = lane // POOL_GROUP_DIM
    win = jnp.zeros((1, POOL_WIDTH), I32)
    for g, w in enumerate(POOL_WINDOWS):
        win = jnp.where(grp == g, w, win)
    left = win // 2
    right = win - 1 - left
    for ti in range(NT):
        seg_len = CTX if ti == 0 else SEQ
        seg_t0 = 0 if ti == 0 else (ti - 1) * TT
        prow = _tile_pad_row(ti)
        tpos = seg_t0 + lax.broadcasted_iota(I32, (TT, 1), 0)
        acc = jnp.zeros((TT, POOL_WIDTH), F32)
        for j in range(-max(POOL_WINDOWS) // 2, max(POOL_WINDOWS) // 2):
            inwin = (j >= -left) & (j <= right)
            acc = acc + jnp.where(inwin, pad_ref[pl.ds(prow + j, TT), :], 0.0)
        cnt = jnp.minimum(tpos + right + 1, seg_len) - jnp.maximum(tpos - left, 0)
        dlt = acc / cnt.astype(F32) - pad_ref[pl.ds(prow, TT), :]
        y = _dot(dlt.astype(BF16), wbd_ref[...]) * scale_ref[...]
        o_ref[0, ti * TT:(ti + 1) * TT, :] = y.astype(BF16)


def _pool(pool_u, wbd, scale):
    return pl.pallas_call(
        _pool_kernel,
        out_shape=jax.ShapeDtypeStruct((NB, T, POOL_WIDTH), BF16),
        grid=(NB,),
        in_specs=[
            pl.BlockSpec((1, T, POOL_WIDTH), lambda b: (b, 0, 0)),
            pl.BlockSpec((POOL_WIDTH, POOL_WIDTH), lambda b: (0, 0)),
            pl.BlockSpec((1, POOL_WIDTH), lambda b: (0, 0)),
        ],
        out_specs=pl.BlockSpec((1, T, POOL_WIDTH), lambda b: (b, 0, 0)),
        scratch_shapes=[pltpu.VMEM((PAD_ROWS, POOL_WIDTH), F32)],
        compiler_params=_cparams(("arbitrary",)),
        name="pool",
    )(pool_u, wbd, scale)


HW = HB * GDN_DIM


NCH = 2 * HB
PW = NCH * CHUNK
GCH = 12


def _block_diag(xp, blk_masks):
    return jnp.concatenate([jnp.where(m, xp, 0.0) for m in blk_masks], axis=0).astype(BF16)


def _gdn_kernel(q_ref, k_ref, v_ref, z_ref, cwq_ref, cwk_ref, cwv_ref, ab_ref, alog_ref, dtb_ref, nw_ref,
                y_ref, pad_ref, qn_ref, kn_ref, vv_ref, m_ref, b_ref, qt_ref, o_ref, egl_ref, s_ref):
    def conv(x_ref, cw_ref, dst_ref, l2, scale):
        _fill_padded(pad_ref, lambda s, n: x_ref[0, s:s + n, :])
        cw = cw_ref[...]
        for ti in range(NT):
            prow = _tile_pad_row(ti)
            acc = jnp.zeros((TT, HW), F32)
            for j in range(GDN_CONV):
                acc = acc + pad_ref[pl.ds(prow - 2 + j, TT), :] * cw[j:j + 1, :]
            y = _silu(acc)
            if l2:
                for hh in range(HB):
                    yh = y[:, hh * GDN_DIM:(hh + 1) * GDN_DIM]
                    yh = yh * lax.rsqrt(jnp.sum(yh * yh, axis=-1, keepdims=True) + RMS_EPS)
                    dst_ref[ti * TT:(ti + 1) * TT, hh * GDN_DIM:(hh + 1) * GDN_DIM] = yh * scale
            else:
                dst_ref[ti * TT:(ti + 1) * TT, :] = y

    conv(q_ref, cwq_ref, qn_ref, True, GDN_DIM ** -0.5)
    conv(k_ref, cwk_ref, kn_ref, True, 1.0)
    conv(v_ref, cwv_ref, vv_ref, False, 1.0)

    ri = lax.broadcasted_iota(I32, (CHUNK, CHUNK), 0)
    ci = lax.broadcasted_iota(I32, (CHUNK, CHUNK), 1)
    tri2 = jnp.concatenate([(ci <= ri).astype(BF16), (ci >= ri).astype(BF16)], axis=0)
    rowp = lax.broadcasted_iota(I32, (CHUNK, PW), 0)
    lanep = lax.broadcasted_iota(I32, (CHUNK, PW), 1)
    blk = lanep // CHUNK
    colp = lanep - blk * CHUNK
    is_fwd = blk < HB
    ahead = jnp.where(is_fwd, rowp - colp, colp - rowp)
    incl_p = ahead >= 0
    strict_p = ahead > 0
    eye_p = (colp == rowp).astype(F32)
    eye_d = (lax.broadcasted_iota(I32, (GDN_DIM, GDN_DIM), 0)
             == lax.broadcasted_iota(I32, (GDN_DIM, GDN_DIM), 1)).astype(BF16)
    blk_masks = [blk == i for i in range(NCH)]
    alog = alog_ref[0]
    dtb = dtb_ref[0]

    def split3(v):
        h1 = v.astype(BF16)
        rem = v - h1.astype(F32)
        h2 = rem.astype(BF16)
        return h1, h2, (rem - h2.astype(F32)).astype(BF16)

    def phase1(it, carry):
        cs = [it * GCH + j for j in range(GCH)]
        r0 = [pl.multiple_of(c * CHUNK, CHUNK) for c in cs]
        m0 = [pl.multiple_of(c * GDN_DIM, GDN_DIM) for c in cs]
        e0 = [pl.multiple_of(c * 8, 8) for c in cs]
        G = range(GCH)
        abt = [ab_ref[0, pl.ds(r0[j], CHUNK), :] for j in G]
        g_all = [-jnp.exp(alog) * _softplus(abt[j] + dtb) for j in G]
        beta_all = [jax.nn.sigmoid(abt[j]) for j in G]
        g3 = [split3(g_all[j]) for j in G]
        gcs = [_dot(tri2, g3[j][0]) + (_dot(tri2, g3[j][1]) + _dot(tri2, g3[j][2])) for j in G]
        gb3 = [split3(jnp.where(strict_p, jnp.concatenate(
            [jnp.broadcast_to(g_all[j][:, i:i + 1], (CHUNK, CHUNK)) for i in range(NCH)], axis=1), 0.0)) for j in G]
        dif2 = [_dot(tri2, gb3[j][0]) + (_dot(tri2, gb3[j][1]) + _dot(tri2, gb3[j][2])) for j in G]
        kh = [[kn_ref[pl.ds(r0[j], CHUNK), hh * GDN_DIM:(hh + 1) * GDN_DIM] for hh in range(HB)] for j in G]
        qh = [[qn_ref[pl.ds(r0[j], CHUNK), hh * GDN_DIM:(hh + 1) * GDN_DIM] for hh in range(HB)] for j in G]
        vh = [[vv_ref[pl.ds(r0[j], CHUNK), hh * GDN_DIM:(hh + 1) * GDN_DIM] for hh in range(HB)] for j in G]
        gram = [[_dot_nt(jnp.concatenate([kh[j][hh].astype(BF16), qh[j][hh].astype(BF16), eye_d], axis=0),
                         kh[j][hh].astype(BF16)) for hh in range(HB)] for j in G]
        gcol1, bcol1, glast, egc1, kdsc = [], [], [], [], []
        for j in G:
            gcol1.append([]); bcol1.append([]); glast.append([]); egc1.append([]); kdsc.append([])
            for i in range(NCH):
                d = i // HB
                gc = gcs[j][d * CHUNK:(d + 1) * CHUNK]
                gcol1[j].append(gc[:, i:i + 1])
                bcol1[j].append(beta_all[j][:, NCH + i:NCH + i + 1])
                glast[j].append(gc[CHUNK - 1:CHUNK, i:i + 1] if d == 0 else gc[0:1, i:i + 1])
                egc1[j].append(jnp.exp(gcol1[j][i]))
                kdsc[j].append(jnp.exp(glast[j][i] - gcol1[j][i]))
                egl_ref[d, i % HB, pl.ds(e0[j], 8), :] = jnp.broadcast_to(jnp.exp(glast[j][i]), (8, LANES))
        lm, a_p = [], []
        for j in G:
            bcol_p = jnp.concatenate([jnp.broadcast_to(b, (CHUNK, CHUNK)) for b in bcol1[j]], axis=1)
            diff = jnp.where(is_fwd, dif2[j][:CHUNK], dif2[j][CHUNK:])
            dec = jnp.where(incl_p, jnp.exp(jnp.where(incl_p, diff, 0.0)), 0.0)
            kk_p = jnp.concatenate([gram[j][i % HB][:CHUNK] for i in range(NCH)], axis=1)
            qk_p = jnp.concatenate([gram[j][i % HB][CHUNK:2 * CHUNK] for i in range(NCH)], axis=1)
            lm.append(jnp.where(strict_p, bcol_p * kk_p * dec, 0.0))
            a_p.append(qk_p * dec)
        x = [eye_p - lm[j] for j in G]
        p = [_dot(lm[j].astype(BF16), _block_diag(lm[j], blk_masks)) for j in G]
        for _ in range(4):
            r = [_dot(jnp.concatenate([x[j], p[j]], axis=0).astype(BF16), _block_diag(p[j], blk_masks)) for j in G]
            x = [x[j] + r[j][:CHUNK] for j in G]
            p = [r[j][CHUNK:] for j in G]
        r = [_dot(x[j].astype(BF16), _block_diag(p[j], blk_masks)) for j in G]
        x = [x[j] + r[j] for j in G]
        rhs = [jnp.concatenate(
            [jnp.concatenate([vh[j][i % HB] * bcol1[j][i], kh[j][i % HB] * (bcol1[j][i] * egc1[j][i])], axis=1)
             for i in range(NCH)], axis=0).astype(BF16) for j in G]
        uw = [_dot(_block_diag(x[j], blk_masks), rhs[j]) for j in G]
        ao = [_dot(_block_diag(a_p[j], blk_masks), uw[j].astype(BF16)) for j in G]
        mb = [[_dot(gram[j][i % HB][2 * CHUNK:].astype(BF16),
                    (uw[j][i * CHUNK:(i + 1) * CHUNK] * kdsc[j][i]).astype(BF16)) for i in range(NCH)] for j in G]
        for j in G:
            for hh in range(HB):
                hs = slice(hh * GDN_DIM, (hh + 1) * GDN_DIM)
                o0 = jnp.zeros((CHUNK, GDN_DIM), F32)
                for d in range(2):
                    i = d * HB + hh
                    rs = slice(i * CHUNK, (i + 1) * CHUNK)
                    b_ref[d, hh, pl.ds(m0[j], GDN_DIM), :] = mb[j][i][:, :GDN_DIM].astype(BF16)
                    m_ref[d, hh, pl.ds(m0[j], GDN_DIM), :] = mb[j][i][:, GDN_DIM:].astype(BF16)
                    qt_ref[d, pl.ds(r0[j], CHUNK), hs] = (qh[j][hh] * egc1[j][i] - ao[j][rs, GDN_DIM:]).astype(BF16)
                    o0 = o0 + ao[j][rs, :GDN_DIM]
                o_ref[pl.ds(r0[j], CHUNK), hs] = o0
        return carry

    lax.fori_loop(0, NC // GCH, phase1, 0)

    s_ref[...] = jnp.zeros(s_ref.shape, F32)

    def phase2(step, carry):
        cb = jnp.where(step < NCC, NCC - 1 - step, NC + NCC - 1 - step)
        chains = [(d, hh, c) for d, c in ((0, step), (1, cb)) for hh in range(HB)]
        rows = [pl.ds(pl.multiple_of(c * CHUNK, CHUNK), CHUNK) for _, _, c in chains]
        mrows = [pl.ds(pl.multiple_of(c * GDN_DIM, GDN_DIM), GDN_DIM) for _, _, c in chains]
        hs = [slice(hh * GDN_DIM, (hh + 1) * GDN_DIM) for _, hh, _ in chains]
        n = range(len(chains))
        st = [s_ref[chains[i][0], chains[i][1]] for i in n]
        stb = [st[i].astype(BF16) for i in n]
        ms = [_dot(m_ref[chains[i][0], chains[i][1], mrows[i], :], stb[i]) for i in n]
        oq = [_dot(qt_ref[chains[i][0], rows[i], hs[i]], stb[i]) for i in n]
        for i in n:
            d, hh, c = chains[i]
            egl = egl_ref[d, hh, pl.ds(pl.multiple_of(c * 8, 8), 1), :]
            s_ref[d, hh] = st[i] * egl - ms[i] + b_ref[d, hh, mrows[i], :].astype(F32)
            o_ref[rows[i], hs[i]] += oq[i]
        return carry

    lax.fori_loop(0, NC, phase2, 0, unroll=4)

    nw = nw_ref[...]
    for ti in range(NT):
        rs = slice(ti * TT, (ti + 1) * TT)
        o = o_ref[rs, :]
        zz = _silu(z_ref[0, rs, :])
        for hh in range(HB):
            hs = slice(hh * GDN_DIM, (hh + 1) * GDN_DIM)
            oh = o[:, hs]
            oh = oh * lax.rsqrt(jnp.mean(oh * oh, axis=-1, keepdims=True) + RMS_EPS) * nw
            y_ref[0, rs, hs] = (oh * zz[:, hs]).astype(BF16)


def _gdn(p_main, conv_w, ab, alog_g, dtb_g, norm_w):
    nhb = GDN_WIDTH // HW
    blk = lambda off: pl.BlockSpec((1, T, HW), lambda b, g: (b, 0, off * nhb + g))
    cblk = lambda off: pl.BlockSpec((GDN_CONV, HW), lambda b, g: (0, off * nhb + g))
    return pl.pallas_call(
        _gdn_kernel,
        out_shape=jax.ShapeDtypeStruct((NB, T, GDN_WIDTH), BF16),
        grid=(NB, NG),
        in_specs=[
            blk(0), blk(1), blk(2), blk(3),
            cblk(0), cblk(1), cblk(2),
            pl.BlockSpec((1, T, LANES), lambda b, g: (b, 0, g)),
            pl.BlockSpec((1, 1, LANES), lambda b, g: (g, 0, 0)),
            pl.BlockSpec((1, 1, LANES), lambda b, g: (g, 0, 0)),
            pl.BlockSpec((1, GDN_DIM), lambda b, g: (0, 0)),
        ],
        out_specs=pl.BlockSpec((1, T, HW), lambda b, g: (b, 0, g)),
        scratch_shapes=[
            pltpu.VMEM((PAD_ROWS, HW), F32),
            pltpu.VMEM((T, HW), F32),
            pltpu.VMEM((T, HW), F32),
            pltpu.VMEM((T, HW), F32),
            pltpu.VMEM((2, HB, NC * GDN_DIM, GDN_DIM), BF16),
            pltpu.VMEM((2, HB, NC * GDN_DIM, GDN_DIM), BF16),
            pltpu.VMEM((2, T, HW), BF16),
            pltpu.VMEM((T, HW), F32),
            pltpu.VMEM((2, HB, NC * 8, LANES), F32),
            pltpu.VMEM((2, HB, GDN_DIM, GDN_DIM), F32),
        ],
        compiler_params=_cparams(("arbitrary", "arbitrary")),
        name="gdn",
    )(p_main, p_main, p_main, p_main, conv_w, conv_w, conv_w, ab, alog_g, dtb_g, norm_w)


NEG = -1e30


def _mix_kernel(n_act, ctx_tile0, *refs):
    act_refs = refs[:n_act]
    w_refs = refs[n_act:2 * n_act]
    rest = refs[2 * n_act:]
    b = pl.program_id(0)
    t = pl.program_id(1)
    if ctx_tile0:
        c_ref, x_ref, mod_ref, lng_ref, lnb_ref, wr_ref, br_ref, x1_ref, h2_ref, route_ref = rest
        row = jnp.where(t == 0, NB, b)
        xt = jnp.where(t == 0, c_ref[0], x_ref[0])
    else:
        x_ref, mod_ref, lng_ref, lnb_ref, wr_ref, br_ref, x1_ref, h2_ref, route_ref = rest
        row = b
        xt = x_ref[0]
    y = _dot(act_refs[0][0], w_refs[0][...])
    for a_ref, w_ref in zip(act_refs[1:], w_refs[1:]):
        y = y + _dot(a_ref[0], w_ref[...])
    x1 = _layer_norm(ALPHA * xt + _mod_slice(mod_ref, row, 2) * y, lng_ref[...], lnb_ref[...])
    x1_ref[0] = x1
    h2 = x1 * (1.0 + _mod_slice(mod_ref, row, 4)) + _mod_slice(mod_ref, row, 3)
    h2_ref[...] = _to_token_tiles(h2)
    logits = _dot3(h2, wr_ref[...]) + br_ref[...]
    lane = lax.broadcasted_iota(I32, (TT, LANES), 1)
    gl = jnp.where(lane < MOE_GROUPS, logits, NEG)
    gm = jnp.max(gl, axis=-1, keepdims=True)
    gsum = jnp.sum(jnp.exp(gl - gm), axis=-1, keepdims=True)
    g_val = 1.0 / gsum
    g_idx = jnp.min(jnp.where(gl == gm, lane, LANES), axis=-1, keepdims=True)
    sel = (lane >= MOE_GROUPS) & (lane < MOE_GROUPS + MOE_EXPERTS) & (((lane - MOE_GROUPS) // MOE_PER_GROUP) == g_idx)
    el = jnp.where(sel, logits, NEG)
    m1 = jnp.max(el, axis=-1, keepdims=True)
    i1 = jnp.min(jnp.where(el == m1, lane, LANES), axis=-1, keepdims=True)
    el2 = jnp.where(lane == i1, NEG, el)
    m2 = jnp.max(el2, axis=-1, keepdims=True)
    i2 = jnp.min(jnp.where(el2 == m2, lane, LANES), axis=-1, keepdims=True)
    ex2 = jnp.exp(m2 - m1)
    w1 = g_val / (1.0 + ex2)
    w2 = w1 * ex2
    e1 = (i1 - MOE_GROUPS).astype(F32)
    e2 = (i2 - MOE_GROUPS).astype(F32)
    route_ref[...] = jnp.where(lane == 0, e1, jnp.where(lane == 1, e2, jnp.where(lane == 2, w1, jnp.where(lane == 3, w2, 0.0))))


def _mix(acts, ws, streams, mod, ln_g, ln_b, wr, br, ctx_tile0):
    ntl = NT if ctx_tile0 else NTL
    if ctx_tile0:
        stream_specs = [pl.BlockSpec((1, CTX, D), lambda b, t: (b, 0, 0)),
                        pl.BlockSpec((1, TT, D), lambda b, t: (b, jnp.maximum(t - 1, 0), 0))]
    else:
        stream_specs = [pl.BlockSpec((1, TT, D), lambda b, t: (b, t + 1, 0))]
    n_tok = NB * ntl * TT
    act_specs = [pl.BlockSpec((1, TT, a.shape[2]), lambda b, t: (b, t, 0)) for a in acts]
    w_specs = [pl.BlockSpec(w.shape, lambda b, t: (0, 0)) for w in ws]
    return pl.pallas_call(
        functools.partial(_mix_kernel, len(acts), ctx_tile0),
        out_shape=(
            jax.ShapeDtypeStruct((NB, ntl * TT, D), F32),
            jax.ShapeDtypeStruct((n_tok, NBLK, LANES), F32),
            jax.ShapeDtypeStruct((n_tok, LANES), F32),
        ),
        grid=(NB, ntl),
        in_specs=act_specs + w_specs + stream_specs + [
            pl.BlockSpec((16, N_MOD * D), lambda b, t: (0, 0)),
            pl.BlockSpec((1, D), lambda b, t: (0, 0)),
            pl.BlockSpec((1, D), lambda b, t: (0, 0)),
            pl.BlockSpec((D, LANES), lambda b, t: (0, 0)),
            pl.BlockSpec((1, LANES), lambda b, t: (0, 0)),
        ],
        out_specs=(
            pl.BlockSpec((1, TT, D), lambda b, t: (b, t, 0)),
            pl.BlockSpec((TT, NBLK, LANES), lambda b, t: (b * ntl + t, 0, 0)),
            pl.BlockSpec((TT, LANES), lambda b, t: (b * ntl + t, 0)),
        ),
        compiler_params=_cparams(("arbitrary", "arbitrary")),
        name="mix_ctx" if ctx_tile0 else "mix_lat",
    )(*acts, *ws, *streams, mod, ln_g, ln_b, wr, br)


def _plan_kernel(route_ref, er_ref, cnt_ref, carry_ref):
    i = pl.program_id(0)

    @pl.when(i == 0)
    def _():
        carry_ref[...] = jnp.zeros(carry_ref.shape, F32)

    r = route_ref[...]
    lane = lax.broadcasted_iota(I32, (PB, LANES), 1)
    lanef = lane.astype(F32)
    ri = lax.broadcasted_iota(I32, (PB, PB), 0)
    ci = lax.broadcasted_iota(I32, (PB, PB), 1)
    before = (ci < ri).astype(BF16)
    oh1 = jnp.where(lanef == r[:, 0:1], 1.0, 0.0)
    oh2 = jnp.where(lanef == r[:, 1:2], 1.0, 0.0)
    c0 = carry_ref[0:1, :]
    tot1 = jnp.sum(oh1, axis=0, keepdims=True)
    tot2 = jnp.sum(oh2, axis=0, keepdims=True)
    r1 = _dot(before, oh1.astype(BF16)) + c0
    r2 = _dot(before, oh2.astype(BF16)) + (c0 + tot1)
    rank1 = jnp.sum(r1 * oh1, axis=-1, keepdims=True)
    rank2 = jnp.sum(r2 * oh2, axis=-1, keepdims=True)
    er = jnp.where(lane < 2, r, jnp.where(lane == 2, rank1, jnp.where(lane == 3, rank2, 0.0)))
    er_ref[...] = er.T[0:8, :]
    cnew = jnp.broadcast_to(c0 + tot1 + tot2, carry_ref.shape)
    carry_ref[...] = cnew
    cnt_ref[...] = cnew


def _plan(route):
    n_tok = route.shape[0]
    return pl.pallas_call(
        _plan_kernel,
        out_shape=(jax.ShapeDtypeStruct((8, n_tok), F32), jax.ShapeDtypeStruct((8, LANES), F32)),
        grid=(n_tok // PB,),
        in_specs=[pl.BlockSpec((PB, LANES), lambda i: (i, 0))],
        out_specs=(pl.BlockSpec((8, PB), lambda i: (0, i)), pl.BlockSpec((8, LANES), lambda i: (0, 0))),
        scratch_shapes=[pltpu.VMEM((8, LANES), F32)],
        compiler_params=_cparams(("arbitrary",)),
        name="moe_plan",
    )(route)


INV_UNROLL = 16


def _invmap_kernel(pos_ref, src_ref):
    n_rows = src_ref.shape[0]
    n_tok = pos_ref.shape[0] // 2

    def clear(g, c):
        for u in range(INV_UNROLL):
            src_ref[g * INV_UNROLL + u] = 0
        return c

    lax.fori_loop(0, n_rows // INV_UNROLL, clear, 0)

    def put(g, c):
        for u in range(INV_UNROLL):
            t = g * INV_UNROLL + u
            src_ref[pos_ref[t]] = t
            src_ref[pos_ref[n_tok + t]] = t
        return c

    lax.fori_loop(0, n_tok // INV_UNROLL, put, 0)


def _invmap(pos_flat, n_rows):
    return pl.pallas_call(
        _invmap_kernel,
        out_shape=jax.ShapeDtypeStruct((n_rows,), I32),
        in_specs=[pl.BlockSpec(memory_space=pltpu.SMEM)],
        out_specs=pl.BlockSpec(memory_space=pltpu.SMEM),
        name="moe_invmap",
    )(pos_flat)


def _ffn_kernel(te_ref, nu_ref, src_ref, h_hbm, wg_ref, wu_ref, wd_ref, y_ref, buf, wgb, wub, wdb, sem):
    i = pl.program_id(0)
    nu = nu_ref[0]

    def issue(tile, slot, lo, hi):
        base = tile * TM
        for r in range(lo, hi):
            pltpu.make_async_copy(h_hbm.at[src_ref[base + r]], buf.at[slot, r], sem.at[slot]).start(priority=r % 2)

    def ffn_tile(slot, next_tile, next_slot):
        per = TM // FFN_DMA_GROUPS
        sent = [0]

        def gather_some():
            if next_tile is not None and sent[0] < FFN_DMA_GROUPS:
                issue(next_tile, next_slot, sent[0] * per, (sent[0] + 1) * per)
                sent[0] += 1

        x = _from_token_tiles(buf[slot]).astype(BF16)
        hcol = MOE_FF // 2
        gather_some()
        parts = []
        for w in (wgb, wub):
            for c in range(2):
                parts.append(_dot(x, w[:, c * hcol:(c + 1) * hcol]))
                gather_some()
        a = jnp.concatenate(parts[0:2], axis=1)
        bb = jnp.concatenate(parts[2:4], axis=1)
        act = (_silu(a) * bb).astype(BF16)
        ys = []
        for c in range(D // hcol):
            ys.append(_dot(act, wdb[:, c * hcol:(c + 1) * hcol]))
            gather_some()
        assert next_tile is None or sent[0] == FFN_DMA_GROUPS
        y_ref[...] = _to_token_tiles(jnp.concatenate(ys, axis=1))

    @pl.when(i == 0)
    def _():
        issue(0, 0, 0, TM)

        for j in range(1, FFN_BUFS - 1):
            @pl.when(nu > j)
            def _():
                issue(j, j, 0, TM)

    @pl.when(i < nu)
    def _():
        slot = i % FFN_BUFS
        pltpu.make_async_copy(h_hbm.at[pl.ds(0, TM)], buf.at[slot], sem.at[slot]).wait()
        e = te_ref[i]
        ep = te_ref[jnp.maximum(i - 1, 0)]

        @pl.when((i == 0) | (e != ep))
        def _():
            wgb[...] = wg_ref[0, 0].astype(BF16)
            wub[...] = wu_ref[0, 0].astype(BF16)
            wdb[...] = wd_ref[0, 0].astype(BF16)

        ahead = i + (FFN_BUFS - 1)

        @pl.when(ahead < nu)
        def _():
            ffn_tile(slot, ahead, ahead % FFN_BUFS)

        @pl.when(ahead >= nu)
        def _():
            ffn_tile(slot, None, None)

    @pl.when(i >= nu)
    def _():
        y_ref[...] = jnp.zeros(y_ref.shape, F32)


def _ffn(tile_expert, n_used, src, h2, wg, wu, wd, layer):
    n_rows = src.shape[0]
    n_tiles = n_rows // TM
    wspec = lambda shape: pl.BlockSpec((1, 1) + shape, lambda i, te, nu, src: (layer, te[i], 0, 0))
    return pl.pallas_call(
        _ffn_kernel,
        out_shape=jax.ShapeDtypeStruct((n_rows, NBLK, LANES), F32),
        grid_spec=pltpu.PrefetchScalarGridSpec(
            num_scalar_prefetch=3,
            grid=(n_tiles,),
            in_specs=[
                pl.BlockSpec(memory_space=pl.ANY),
                wspec((D, MOE_FF)), wspec((D, MOE_FF)), wspec((MOE_FF, D)),
            ],
            out_specs=pl.BlockSpec((TM, NBLK, LANES), lambda i, te, nu, src: (i, 0, 0)),
            scratch_shapes=[
                pltpu.VMEM((FFN_BUFS, TM, NBLK, LANES), F32),
                pltpu.VMEM((D, MOE_FF), BF16),
                pltpu.VMEM((D, MOE_FF), BF16),
                pltpu.VMEM((MOE_FF, D), BF16),
                pltpu.SemaphoreType.DMA((FFN_BUFS,)),
            ],
        ),
        compiler_params=_cparams(("arbitrary",)),
        name="moe_ffn",
    )(tile_expert, n_used, src, h2, wg, wu, wd)


def _combine_kernel(ntl, ctx_tile0, pos_ref, y_hbm, x1_ref, route_ref, mod_ref, lng_ref, lnb_ref, o_ref, buf, sem):
    b = pl.program_id(0)
    t = pl.program_id(1)
    flat = b * ntl + t
    n_tiles = NB * ntl

    def issue(tile, slot):
        base = tile * TT
        for r in range(TT):
            for k in range(2):
                pltpu.make_async_copy(y_hbm.at[pos_ref[k * (n_tiles * TT) + base + r]], buf.at[slot, k * TT + r],
                                      sem.at[slot]).start(priority=k)

    @pl.when(flat == 0)
    def _():
        issue(0, 0)

    slot = flat % 2
    pltpu.make_async_copy(y_hbm.at[pl.ds(0, 2 * TT)], buf.at[slot], sem.at[slot]).wait()

    @pl.when(flat + 1 < n_tiles)
    def _():
        issue(flat + 1, 1 - slot)

    row = jnp.where(t == 0, NB, b) if ctx_tile0 else b
    r = route_ref[...]
    f = (r[:, 2:3] * _from_token_tiles(buf[slot, 0:TT]) + r[:, 3:4] * _from_token_tiles(buf[slot, TT:2 * TT]))
    o_ref[0] = _layer_norm(ALPHA * x1_ref[0] + _mod_slice(mod_ref, row, 5) * f, lng_ref[...], lnb_ref[...])


def _combine(pos_flat, ys, x1, route, mod, ln_g, ln_b, ctx_tile0):
    ntl = NT if ctx_tile0 else NTL
    return pl.pallas_call(
        functools.partial(_combine_kernel, ntl, ctx_tile0),
        out_shape=jax.ShapeDtypeStruct((NB, ntl * TT, D), F32),
        grid_spec=pltpu.PrefetchScalarGridSpec(
            num_scalar_prefetch=1,
            grid=(NB, ntl),
            in_specs=[
                pl.BlockSpec(memory_space=pl.ANY),
                pl.BlockSpec((1, TT, D), lambda b, t, pos: (b, t, 0)),
                pl.BlockSpec((TT, LANES), lambda b, t, pos: (b * ntl + t, 0)),
                pl.BlockSpec((16, N_MOD * D), lambda b, t, pos: (0, 0)),
                pl.BlockSpec((1, D), lambda b, t, pos: (0, 0)),
                pl.BlockSpec((1, D), lambda b, t, pos: (0, 0)),
            ],
            out_specs=pl.BlockSpec((1, TT, D), lambda b, t, pos: (b, t, 0)),
            scratch_shapes=[pltpu.VMEM((2, 2 * TT, NBLK, LANES), F32), pltpu.SemaphoreType.DMA((2,))],
        ),
        compiler_params=_cparams(("arbitrary", "arbitrary")),
        name="moe_combine_ctx" if ctx_tile0 else "moe_combine_lat",
    )(pos_flat, ys, x1, route, mod, ln_g, ln_b)


def _moe(h2, route, x1, mod, ln_g, ln_b, wg, wu, wd, layer, ctx_tile0):
    n_tok = h2.shape[0]
    n_tiles = (2 * n_tok) // TM + MOE_EXPERTS
    er, cnt = _plan(route)
    counts = cnt[0, :MOE_EXPERTS].astype(I32)
    tiles_e = (counts + TM - 1) // TM
    tile_end = jnp.cumsum(tiles_e)
    tile_start = tile_end - tiles_e
    n_used = tile_end[-1]
    tj = jnp.arange(n_tiles, dtype=I32)
    te = jnp.minimum(jnp.sum((tile_end[None, :] <= tj[:, None]).astype(I32), axis=1), MOE_EXPERTS - 1)
    te_last = jnp.max(jnp.where(tj < n_used, te, 0))
    tile_expert = jnp.where(tj < n_used, te, te_last)
    eid = er[0:2].astype(I32)
    first_row = jnp.sum(jnp.where(eid[None] == jnp.arange(MOE_EXPERTS, dtype=I32)[:, None, None],
                                  (tile_start * TM)[:, None, None], 0), axis=0)
    pos_flat = (first_row + er[2:4].astype(I32)).reshape(-1)
    src = _invmap(pos_flat, n_tiles * TM)
    ys = _ffn(tile_expert, n_used.reshape(1), src, h2, wg, wu, wd, layer)
    return _combine(pos_flat, ys, x1, route, mod, ln_g, ln_b, ctx_tile0)


W_IN1 = MLA_KV_LORA + MLA_ROPE + MLA_Q_LORA + MLA_ROPE
Q_OFF = MLA_KV_LORA + MLA_ROPE
KRS_OFF = Q_OFF + MLA_Q_LORA
SM_SCALE = MLA_QK ** -0.5


def _rms(v, w):
    return v * lax.rsqrt(jnp.mean(v * v, axis=-1, keepdims=True) + RMS_EPS) * w


def _mla_proj_kernel(x_ref, mod_ref, win_ref, kvn_ref, qnw_ref, wkn_ref, wkv_ref, wqn_ref, wqr_ref, wqrs_ref,
                     cosk_ref, sink_ref, cosq_ref, sinq_ref, q_out, k_out, v_out):
    b = pl.program_id(0)
    t = pl.program_id(1)
    row = jnp.where(t == 0, NB, b)
    h = x_ref[0] * (1.0 + _mod_slice(mod_ref, row, 1)) + _mod_slice(mod_ref, row, 0)
    p = _dot(h.astype(BF16), win_ref[...])
    ckv = _rms(p[:, :MLA_KV_LORA], kvn_ref[...]).astype(BF16)
    kn = _dot(ckv, wkn_ref[...])
    vv = _dot(ckv, wkv_ref[...])
    kr = p[:, MLA_KV_LORA:Q_OFF]
    krs = p[:, KRS_OFF:KRS_OFF + MLA_ROPE]
    kr = jnp.where(t > 0, kr * cosk_ref[...] + krs * sink_ref[...], kr).astype(BF16)
    ql = _rms(p[:, Q_OFF:KRS_OFF], qnw_ref[...]).astype(BF16)
    qn = _dot(ql, wqn_ref[...]) * SM_SCALE
    qr = (_dot(ql, wqr_ref[...]) * cosq_ref[...] + _dot(ql, wqrs_ref[...]) * sinq_ref[...]) * SM_SCALE
    for hd in range(MLA_HEADS):
        k_out[0, hd, :, 0:MLA_NOPE] = kn[:, hd * MLA_NOPE:(hd + 1) * MLA_NOPE].astype(BF16)
        k_out[0, hd, :, MLA_NOPE:MLA_QK] = kr
        v_out[0, hd] = vv[:, hd * MLA_V:(hd + 1) * MLA_V].T.astype(BF16)
        q_out[0, hd, :, 0:MLA_NOPE] = qn[:, hd * MLA_NOPE:(hd + 1) * MLA_NOPE].astype(BF16)
        q_out[0, hd, :, MLA_NOPE:MLA_QK] = qr[:, hd * MLA_ROPE:(hd + 1) * MLA_ROPE].astype(BF16)


def _mla_proj(xs, mod, win, kvn, qnw, wkn, wkv, wqn, wqr, wqrs, cosk, sink, cosq, sinq):
    full = lambda a: pl.BlockSpec(a.shape, lambda b, t: (0, 0))
    lat = lambda w: pl.BlockSpec((TT, w), lambda b, t: (jnp.maximum(t - 1, 0), 0))
    return pl.pallas_call(
        _mla_proj_kernel,
        out_shape=(
            jax.ShapeDtypeStruct((NB, MLA_HEADS, SEQ, MLA_QK), BF16),
            jax.ShapeDtypeStruct((NB, MLA_HEADS, T, MLA_QK), BF16),
            jax.ShapeDtypeStruct((NB, MLA_HEADS, MLA_V, T), BF16),
        ),
        grid=(NB, NT),
        in_specs=[
            pl.BlockSpec((1, TT, D), lambda b, t: (b, t, 0)),
            pl.BlockSpec((16, N_MOD * D), lambda b, t: (0, 0)),
            full(win), full(kvn), full(qnw), full(wkn), full(wkv), full(wqn), full(wqr), full(wqrs),
            lat(MLA_ROPE), lat(MLA_ROPE), lat(MLA_HEADS * MLA_ROPE), lat(MLA_HEADS * MLA_ROPE),
        ],
        out_specs=(
            pl.BlockSpec((1, MLA_HEADS, TT, MLA_QK), lambda b, t: (b, 0, jnp.maximum(t - 1, 0), 0)),
            pl.BlockSpec((1, MLA_HEADS, TT, MLA_QK), lambda b, t: (b, 0, t, 0)),
            pl.BlockSpec((1, MLA_HEADS, MLA_V, TT), lambda b, t: (b, 0, 0, t)),
        ),
        compiler_params=_cparams(("arbitrary", "arbitrary")),
        name="mla_proj",
    )(xs, mod, win, kvn, qnw, wkn, wkv, wqn, wqr, wqrs, cosk, sink, cosq, sinq)


AH = 4


def _attn_kernel(q_ref, k_ref, vt_ref, o_ref, s_ref):
    nq = SEQ // TQ

    def scores(j):
        for a in range(AH):
            s_ref[a, j % 2] = _dot_nt(k_ref[0, a], q_ref[0, a, j * TQ:(j + 1) * TQ, :])

    scores(0)
    for j in range(nq):
        if j + 1 < nq:
            scores(j + 1)
        ps, ls = [], []
        for a in range(AH):
            s = s_ref[a, j % 2]
            p = jnp.exp(s - jnp.max(s, axis=0, keepdims=True))
            ls.append(jnp.sum(p, axis=0, keepdims=True))
            ps.append(p.astype(BF16))
        ots = [_dot(vt_ref[0, a], ps[a]) / ls[a] for a in range(AH)]
        for a in range(AH):
            o_ref[0, j * TQ:(j + 1) * TQ, a * MLA_V:(a + 1) * MLA_V] = ots[a].T.astype(BF16)


def _attn(q, k, vt):
    return pl.pallas_call(
        _attn_kernel,
        out_shape=jax.ShapeDtypeStruct((NB, SEQ, MLA_HEADS * MLA_V), BF16),
        grid=(NB, MLA_HEADS // AH),
        in_specs=[
            pl.BlockSpec((1, AH, SEQ, MLA_QK), lambda b, h: (b, h, 0, 0)),
            pl.BlockSpec((1, AH, T, MLA_QK), lambda b, h: (b, h, 0, 0)),
            pl.BlockSpec((1, AH, MLA_V, T), lambda b, h: (b, h, 0, 0)),
        ],
        out_specs=pl.BlockSpec((1, SEQ, AH * MLA_V), lambda b, h: (b, 0, h)),
        scratch_shapes=[pltpu.VMEM((AH, 2, T, TQ), F32)],
        compiler_params=_cparams(("arbitrary", "arbitrary")),
        name="mla_attn",
    )(q, k, vt)


def _rope_tables():
    n = SEQ
    rowp = (jnp.arange(n) // GRID_W).astype(F32)
    colp = (jnp.arange(n) % GRID_W).astype(F32)
    inv_freq = ROPE_BASE ** (-jnp.arange(ROPE_F, dtype=F32) / ROPE_F)
    ar = rowp[:, None] * inv_freq
    ac = colp[:, None] * inv_freq
    cos = jnp.concatenate([jnp.cos(ar), jnp.cos(ar), jnp.cos(ac), jnp.cos(ac)], axis=1)
    sin = jnp.concatenate([-jnp.sin(ar), jnp.sin(ar), -jnp.sin(ac), jnp.sin(ac)], axis=1)
    return cos, sin


def _rope_swap_perm():
    f = ROPE_F
    return jnp.concatenate([jnp.arange(f, 2 * f), jnp.arange(0, f), jnp.arange(3 * f, 4 * f), jnp.arange(2 * f, 3 * f)])


def _router_params(w_group, b_group, w_expert, b_expert):
    pad = LANES - MOE_GROUPS - MOE_EXPERTS
    wr = jnp.concatenate([w_group, w_expert, jnp.zeros((D, pad), F32)], axis=1)
    br = jnp.concatenate([b_group, b_expert, jnp.zeros((pad,), F32)]).reshape(1, LANES)
    return wr, br


def kernel(x, c, ctx, c_ctx, ada_w, ada_b, ln_g, ln_b, ab_w_in, pool_w, pool_scale, gdn_conv_w, gdn_a_log, gdn_dt_bias, gdn_norm_w, ab_w_out, mla_w_in, mla_kv_norm, mla_w_ukv, mla_q_norm, mla_w_uq, mla_w_out, moe_w_group, moe_b_group, moe_w_expert, moe_b_expert, moe_w_gate, moe_w_up, moe_w_down):
    assert x.shape == (NB, SEQ, D) and ctx.shape == (NB, CTX, D)
    cv = jnp.concatenate([c, c_ctx[None, :], jnp.zeros((16 - NB - 1, D), F32)], axis=0)
    mod = _ada(cv, ada_w, ada_b)

    w_in = ab_w_in[0]
    w_main = w_in[:, :W_MAIN].astype(BF16)
    o_ab = W_MAIN
    ab_cols = []
    for g in range(NG):
        idx = [o_ab + kind * 2 * GDN_HEADS + d * GDN_HEADS + g * HB + hh
               for kind in range(2) for d in range(2) for hh in range(HB)]
        ab_cols.append(jnp.concatenate([w_in[:, jnp.array(idx)], jnp.zeros((D, LANES - len(idx)), F32)], axis=1))
    w_ab = jnp.concatenate(ab_cols, axis=1)

    def gate_rows(p):
        rows = []
        for g in range(NG):
            vals = jnp.stack([p[d, g * HB + hh] for d in range(2) for hh in range(HB)])
            rows.append(jnp.concatenate([vals, jnp.zeros((LANES - 2 * HB,), F32)]))
        return jnp.stack(rows).reshape(NG, 1, LANES)

    alog_g = gate_rows(gdn_a_log[0])
    dtb_g = gate_rows(gdn_dt_bias[0])
    wbd = jax.scipy.linalg.block_diag(*[pool_w[0, g] for g in range(len(POOL_WINDOWS))]).astype(BF16)

    pool_u, p_main, ab = _inproj0(ctx, x, mod[0], w_main, w_ab)
    pool_y = _pool(pool_u, wbd, pool_scale[0].reshape(1, POOL_WIDTH))
    gdn_y = _gdn(p_main, gdn_conv_w[0], ab, alog_g, dtb_g, gdn_norm_w[0].reshape(1, GDN_DIM))

    w_out0 = ab_w_out[0].astype(BF16)
    wr0, br0 = _router_params(moe_w_group[0], moe_b_group[0], moe_w_expert[0], moe_b_expert[0])
    x1, h2, route = _mix([pool_y, gdn_y], [w_out0[:POOL_WIDTH], w_out0[POOL_WIDTH:]], (ctx, x), mod[0],
                         ln_g[0, 0].reshape(1, D), ln_b[0, 0].reshape(1, D), wr0, br0, True)
    xs = _moe(h2, route, x1, mod[0], ln_g[0, 1].reshape(1, D), ln_b[0, 1].reshape(1, D),
              moe_w_gate, moe_w_up, moe_w_down, 0, True)

    perm = _rope_swap_perm()
    w_in1 = mla_w_in[0]
    win = jnp.concatenate([w_in1, w_in1[:, MLA_KV_LORA:Q_OFF][:, perm]], axis=1).astype(BF16)
    ukv = mla_w_ukv[0].reshape(MLA_KV_LORA, MLA_HEADS, MLA_NOPE + MLA_V)
    wkn = ukv[:, :, :MLA_NOPE].reshape(MLA_KV_LORA, MLA_HEADS * MLA_NOPE).astype(BF16)
    wkv = ukv[:, :, MLA_NOPE:].reshape(MLA_KV_LORA, MLA_HEADS * MLA_V).astype(BF16)
    uq = mla_w_uq[0].reshape(MLA_Q_LORA, MLA_HEADS, MLA_QK)
    wqn = uq[:, :, :MLA_NOPE].reshape(MLA_Q_LORA, MLA_HEADS * MLA_NOPE).astype(BF16)
    wqr = uq[:, :, MLA_NOPE:].reshape(MLA_Q_LORA, MLA_HEADS * MLA_ROPE).astype(BF16)
    wqrs = uq[:, :, MLA_NOPE:][:, :, perm].reshape(MLA_Q_LORA, MLA_HEADS * MLA_ROPE).astype(BF16)
    cosk, sink = _rope_tables()
    cosq = jnp.tile(cosk, (1, MLA_HEADS))
    sinq = jnp.tile(sink, (1, MLA_HEADS))

    q, k, v = _mla_proj(xs, mod[1], win, mla_kv_norm[0].reshape(1, MLA_KV_LORA), mla_q_norm[0].reshape(1, MLA_Q_LORA),
                        wkn, wkv, wqn, wqr, wqrs, cosk, sink, cosq, sinq)
    att = _attn(q, k, v)
    wr1, br1 = _router_params(moe_w_group[1], moe_b_group[1], moe_w_expert[1], moe_b_expert[1])
    x1, h2, route = _mix([att], [mla_w_out[0].astype(BF16)], (xs,), mod[1],
                         ln_g[1, 0].reshape(1, D), ln_b[1, 0].reshape(1, D), wr1, br1, False)
    return _moe(h2, route, x1, mod[1], ln_g[1, 1].reshape(1, D), ln_b[1, 1].reshape(1, D),
                moe_w_gate, moe_w_up, moe_w_down, 1, False)
```
